```python
import jax, jax.numpy as jnp
from jax import lax
import numpy as np

D_MODEL = 2048
BATCH = 2
SEQ = 4096
DEPTH = 1
DEC_BATCH = 32
DEC_SEQ = 1
PAST_LEN = 8192
PAGE_SIZE = 128

N_HEADS = 16
HEAD_DIM = 64
ATTN_WIDTH = N_HEADS * HEAD_DIM
IDX_HEADS = 16
IDX_DIM = 64
TOPK_MAX = 256
POOL_WINDOWS = (2, 4, 8, 16)
POOL_GROUPS = 4
POOL_WIDTH = D_MODEL // 2
POOL_GROUP_WIDTH = POOL_WIDTH // POOL_GROUPS
POOL_STATE_LEN = 15
D_FF = 4 * D_MODEL
Q_BLOCK = 128
EPS = 1e-6
IN_WIDTHS = (ATTN_WIDTH, ATTN_WIDTH, ATTN_WIDTH, IDX_HEADS * IDX_DIM, IDX_DIM, IDX_HEADS, POOL_WIDTH, D_MODEL, D_MODEL)
N_IN = 3 * ATTN_WIDTH + IDX_HEADS * IDX_DIM + IDX_DIM + IDX_HEADS + POOL_WIDTH + 2 * D_MODEL

kernel_name = 'dsa_pool_gated_hybrid'


def rms_norm(x, g):
    xf = x.astype(jnp.float32)
    y = xf * lax.rsqrt(jnp.mean(xf * xf, axis=-1, keepdims=True) + EPS)
    return (y * g.astype(jnp.float32)).astype(x.dtype)


def in_project(z, w_in):
    b, t, _ = z.shape
    p = jnp.einsum('btd,dn->btn', z, w_in)
    bounds = [int(c) for c in np.cumsum(IN_WIDTHS)[:-1]]
    q, k, v, qi, ki, wi, u, ga, gb = jnp.split(p, bounds, axis=-1)
    return (q.reshape(b, t, N_HEADS, HEAD_DIM), k.reshape(b, t, N_HEADS, HEAD_DIM),
            v.reshape(b, t, N_HEADS, HEAD_DIM), qi.reshape(b, t, IDX_HEADS, IDX_DIM),
            ki, wi * (IDX_HEADS ** -0.5), u, ga, gb)


def indexer_scores(qi, wi, ki):
    s = jnp.einsum('bthi,bsi->bths', qi.astype(jnp.float32), ki.astype(jnp.float32)) * (IDX_DIM ** -0.5)
    return jnp.einsum('bths,bth->bts', jax.nn.relu(s), wi.astype(jnp.float32))


def gather_rows(rows, idx):
    return jax.vmap(lambda r, i: r[i])(rows, idx)


def attend_selected(q, k_sel, v_sel, valid):
    s = jnp.einsum('bthd,btkhd->bthk', q.astype(jnp.float32), k_sel.astype(jnp.float32)) * (HEAD_DIM ** -0.5)
    s = jnp.where(valid[:, :, None, :], s, -jnp.inf)
    p = jax.nn.softmax(s, axis=-1)
    o = jnp.einsum('bthk,btkhd->bthd', p, v_sel.astype(jnp.float32))
    return o.astype(q.dtype)


def prompt_sparse_attention(q, k, v, qi, wi, ki):
    b, t = q.shape[:2]
    topk = min(TOPK_MAX, t // 4)
    nb = t // Q_BLOCK

    def blocks(a):
        return a.reshape((b, nb, Q_BLOCK) + a.shape[2:]).swapaxes(0, 1)

    key_pos = jnp.arange(t)

    def one_block(args):
        i, q_blk, qi_blk, wi_blk = args
        q_pos = i * Q_BLOCK + jnp.arange(Q_BLOCK)
        causal = key_pos[None, None, :] <= q_pos[None, :, None]
        sc = jnp.where(causal, indexer_scores(qi_blk, wi_blk, ki), -jnp.inf)
        _, idx = lax.top_k(sc, topk)
        valid = idx <= q_pos[None, :, None]
        return attend_selected(q_blk, gather_rows(k, idx), gather_rows(v, idx), valid)

    out = lax.map(one_block, (jnp.arange(nb), blocks(q), blocks(qi), blocks(wi)))
    return out.swapaxes(0, 1).reshape(b, t, N_HEADS, HEAD_DIM)


def sample_sparse_attention(q, k_new, v_new, qi, wi, ki_new, cache_k, cache_v, cache_kidx, page_table):
    bd, tn = q.shape[:2]
    n_pages = page_table.shape[1]
    page = cache_k.shape[1]
    past = n_pages * page
    topk = min(TOPK_MAX, (past + tn) // 4)
    ki_past = cache_kidx[page_table].reshape(bd, past, IDX_DIM)
    ki_all = jnp.concatenate([ki_past.astype(ki_new.dtype), ki_new], axis=1)
    key_pos = jnp.arange(past + tn)
    q_pos = past + jnp.arange(tn)
    causal = key_pos[None, None, :] <= q_pos[None, :, None]
    sc = jnp.where(causal, indexer_scores(qi, wi, ki_all), -jnp.inf)
    _, idx = lax.top_k(sc, topk)
    valid = idx <= q_pos[None, :, None]
    in_past = (idx < past)[..., None, None]
    p_idx = jnp.minimum(idx, past - 1)
    phys = gather_rows(page_table, p_idx // page)
    off = p_idx % page
    n_idx = jnp.clip(idx - past, 0, tn - 1)
    k_sel = jnp.where(in_past, cache_k[phys, off].astype(k_new.dtype), gather_rows(k_new, n_idx))
    v_sel = jnp.where(in_past, cache_v[phys, off].astype(v_new.dtype), gather_rows(v_new, n_idx))
    return attend_selected(q, k_sel, v_sel, valid)


def pool_mix(u_ext, n_out, w_group, scale):
    b, n, c = u_ext.shape
    uf = u_ext.astype(jnp.float32)
    csum = jnp.concatenate([jnp.zeros((b, 1, c), jnp.float32), lax.cumsum(uf, axis=1)], axis=1)
    j = jnp.arange(n - n_out, n)
    cur = uf[:, n - n_out:]
    parts = []
    for g, w in enumerate(POOL_WINDOWS):
        sl = slice(g * POOL_GROUP_WIDTH, (g + 1) * POOL_GROUP_WIDTH)
        lo = jnp.maximum(j + 1 - w, 0)
        win = csum[:, j + 1, sl] - csum[:, lo, sl]
        cnt = jnp.minimum(j + 1, w).astype(jnp.float32)
        parts.append(win / cnt[None, :, None] - cur[..., sl])
    mixed = jnp.stack(parts, axis=2)
    y = jnp.einsum('btgc,gce->btge', mixed, w_group.astype(jnp.float32)).reshape(b, n_out, c)
    return (y * scale.astype(jnp.float32)).astype(u_ext.dtype)


def merge_and_ffn(h, attn_o, pool_o, ga, gb, b_gate, w_attn_out, w_pool_out, w_out, norm_ffn_g, w_up, w_down):
    b, t = h.shape[:2]
    ya = jnp.einsum('btc,cd->btd', attn_o.reshape(b, t, ATTN_WIDTH), w_attn_out)
    yb = jnp.einsum('btc,cd->btd', pool_o, w_pool_out)
    merged = jax.nn.sigmoid(ga + b_gate[0]) * ya + jax.nn.sigmoid(gb + b_gate[1]) * yb
    h = h + jnp.einsum('btd,de->bte', merged, w_out)
    z = rms_norm(h, norm_ffn_g)
    a = jnp.square(jax.nn.relu(jnp.einsum('btd,df->btf', z, w_up)))
    return h + jnp.einsum('btf,fd->btd', a, w_down)


def setup_inputs(seed: int = 0) -> dict:
    key = jax.random.key(seed)
    ks = jax.random.split(key, 20)
    f32 = jnp.float32
    n_pages = PAST_LEN // PAGE_SIZE
    n_used = DEC_BATCH * n_pages
    n_pool = n_used + max(1, n_used // 4)

    def nrm(k, shape, scale):
        return jax.random.normal(k, shape, f32) * scale

    page_table = jax.random.permutation(ks[6], n_pool)[:n_used].reshape(DEC_BATCH, n_pages).astype(jnp.int32)
    return {
        'x_prompt': nrm(ks[0], (BATCH, SEQ, D_MODEL), 1.0),
        'x_sample': nrm(ks[1], (DEC_BATCH, DEC_SEQ, D_MODEL), 1.0),
        'cache_k': nrm(ks[2], (DEPTH, n_pool, PAGE_SIZE, N_HEADS, HEAD_DIM), 1.0),
        'cache_v': nrm(ks[3], (DEPTH, n_pool, PAGE_SIZE, N_HEADS, HEAD_DIM), 1.0),
        'cache_kidx': nrm(ks[4], (DEPTH, n_pool, PAGE_SIZE, IDX_DIM), 1.0),
        'state_pool': nrm(ks[5], (DEPTH, DEC_BATCH, POOL_STATE_LEN, POOL_WIDTH), 1.0),
        'page_table': page_table,
        'norm_mix_g': 1.0 + nrm(ks[7], (DEPTH, D_MODEL), 0.1),
        'w_in': nrm(ks[8], (DEPTH, D_MODEL, N_IN), D_MODEL ** -0.5),
        'b_gate': nrm(ks[9], (DEPTH, 2, D_MODEL), 0.1),
        'w_attn_out': nrm(ks[10], (DEPTH, ATTN_WIDTH, D_MODEL), ATTN_WIDTH ** -0.5),
        'w_pool_group': nrm(ks[11], (DEPTH, POOL_GROUPS, POOL_GROUP_WIDTH, POOL_GROUP_WIDTH), POOL_GROUP_WIDTH ** -0.5),
        'pool_scale': 1.0 + nrm(ks[12], (DEPTH, POOL_WIDTH), 0.1),
        'w_pool_out': nrm(ks[13], (DEPTH, POOL_WIDTH, D_MODEL), POOL_WIDTH ** -0.5),
        'w_out': nrm(ks[14], (DEPTH, D_MODEL, D_MODEL), D_MODEL ** -0.5),
        'norm_ffn_g': 1.0 + nrm(ks[15], (DEPTH, D_MODEL), 0.1),
        'w_up': nrm(ks[16], (DEPTH, D_MODEL, D_FF), D_MODEL ** -0.5),
        'w_down': nrm(ks[17], (DEPTH, D_FF, D_MODEL), D_FF ** -0.5),
        'norm_final_g': 1.0 + nrm(ks[18], (D_MODEL,), 0.1),
    }


def reference(x_prompt, x_sample, cache_k, cache_v, cache_kidx, state_pool, page_table,
              norm_mix_g, w_in, b_gate, w_attn_out, w_pool_group, pool_scale, w_pool_out, w_out,
              norm_ffn_g, w_up, w_down, norm_final_g):
    hp, hs = x_prompt, x_sample
    k_p, v_p, ki_p, pool_p = [], [], [], []
    k_s, v_s, ki_s, pool_s = [], [], [], []
    for l in range(DEPTH):
        zp = rms_norm(hp, norm_mix_g[l])
        q, k, v, qi, ki, wi, u, ga, gb = in_project(zp, w_in[l])
        attn_o = prompt_sparse_attention(q, k, v, qi, wi, ki)
        pool_o = pool_mix(u, u.shape[1], w_pool_group[l], pool_scale[l])
        hp = merge_and_ffn(hp, attn_o, pool_o, ga, gb, b_gate[l], w_attn_out[l], w_pool_out[l], w_out[l],
                           norm_ffn_g[l], w_up[l], w_down[l])
        k_p.append(k)
        v_p.append(v)
        ki_p.append(ki)
        pool_p.append(u[:, -POOL_STATE_LEN:])
        zs = rms_norm(hs, norm_mix_g[l])
        q, k, v, qi, ki, wi, u, ga, gb = in_project(zs, w_in[l])
        attn_o = sample_sparse_attention(q, k, v, qi, wi, ki, cache_k[l], cache_v[l], cache_kidx[l], page_table)
        u_ext = jnp.concatenate([state_pool[l].astype(u.dtype), u], axis=1)
        pool_o = pool_mix(u_ext, u.shape[1], w_pool_group[l], pool_scale[l])
        hs = merge_and_ffn(hs, attn_o, pool_o, ga, gb, b_gate[l], w_attn_out[l], w_pool_out[l], w_out[l],
                           norm_ffn_g[l], w_up[l], w_down[l])
        k_s.append(k)
        v_s.append(v)
        ki_s.append(ki)
        pool_s.append(u_ext[:, -POOL_STATE_LEN:])
    y_prompt = rms_norm(hp, norm_final_g)
    y_sample = rms_norm(hs, norm_final_g)
    return (y_prompt, y_sample, jnp.stack(k_p), jnp.stack(v_p), jnp.stack(ki_p), jnp.stack(pool_p),
            jnp.stack(k_s), jnp.stack(v_s), jnp.stack(ki_s), jnp.stack(pool_s))
```

```python
import functools

import jax
import jax.numpy as jnp
from jax import lax
from jax.experimental import pallas as pl
from jax.experimental.pallas import tpu as pltpu

F32 = jnp.float32
BF16 = jnp.bfloat16
I32 = jnp.int32

N_HEADS = 16
HEAD_DIM = 64
ATTN_WIDTH = N_HEADS * HEAD_DIM
IDX_HEADS = 16
IDX_DIM = 64
TOPK_MAX = 256
POOL_WINDOWS = (2, 4, 8, 16)
POOL_GROUP_WIDTH = 256
POOL_HALO = 16
EPS = 1e-6
INT_MIN = -(2 ** 31)
LANES = 128
NEG_BIG = -1e30

VMEM_LIMIT_BYTES = 56 * 1024 * 1024


def _params(*sem):
    return pltpu.CompilerParams(dimension_semantics=sem, vmem_limit_bytes=VMEM_LIMIT_BYTES)


def _sortable_key(x):
    b = lax.bitcast_convert_type(x, I32)
    return jnp.where(b >= 0, b, b ^ jnp.int32(0x7FFFFFFF))


def _rms_matmul_body(x_ref, g_ref, w_ref, o_ref, z_ref):
    @pl.when(pl.program_id(1) == 0)
    def _():
        x = x_ref[...]
        ms = jnp.mean(x * x, axis=-1, keepdims=True)
        z_ref[...] = (x * lax.rsqrt(ms + EPS) * g_ref[...]).astype(BF16)

    o_ref[...] = jnp.dot(z_ref[...], w_ref[...], preferred_element_type=F32)


def _rms_matmul(x, g, w, tm, tn):
    m, d = x.shape
    n = w.shape[1]
    return pl.pallas_call(
        _rms_matmul_body,
        grid=(m // tm, n // tn),
        in_specs=[
            pl.BlockSpec((tm, d), lambda i, j: (i, 0)),
            pl.BlockSpec((1, d), lambda i, j: (0, 0)),
            pl.BlockSpec((d, tn), lambda i, j: (0, j)),
        ],
        out_specs=pl.BlockSpec((tm, tn), lambda i, j: (i, j)),
        out_shape=jax.ShapeDtypeStruct((m, n), F32),
        scratch_shapes=[pltpu.VMEM((tm, d), BF16)],
        compiler_params=_params("parallel", "arbitrary"),
        name="rms_in_proj",
    )(x, g, w)


def _prompt_attn_body(qi_ref, wi_ref, q_ref, ki_ref, k_ref, vt_ref, o_ref,
                      keys_ref, qit_ref, qtp_ref, ot_ref, *, tq, kc, topk):
    i = pl.program_id(1)
    nchunk = (i + 1) * (tq // kc)
    q0 = i * tq

    qit_ref[...] = qi_ref[...].T.astype(BF16).reshape(IDX_HEADS, IDX_DIM, tq)
    w16 = wi_ref[...].T[IDX_DIM:IDX_DIM + IDX_HEADS, :] * (IDX_HEADS ** -0.5 * IDX_DIM ** -0.5)
    qt = (q_ref[...] * (HEAD_DIM ** -0.5)).T.reshape(N_HEADS // 2, 2 * HEAD_DIM, tq)
    first_half = lax.broadcasted_iota(I32, (2 * HEAD_DIM, tq), 0) < HEAD_DIM
    for j in range(N_HEADS // 2):
        qtp_ref[2 * j] = jnp.where(first_half, qt[j], 0.0).astype(BF16)
        qtp_ref[2 * j + 1] = jnp.where(first_half, 0.0, qt[j]).astype(BF16)

    def idx_chunk(c, carry):
        r0 = pl.multiple_of(c * kc, kc)
        kic = ki_ref[pl.ds(r0, kc), :]
        acc = jnp.zeros((kc, tq), F32)
        for h in range(IDX_HEADS):
            s = jnp.dot(kic, qit_ref[h], preferred_element_type=F32)
            acc = acc + jnp.maximum(s, 0.0) * w16[h:h + 1, :]
        key = _sortable_key(acc)
        krow = r0 + lax.broadcasted_iota(I32, (kc, tq), 0)
        qcol = q0 + lax.broadcasted_iota(I32, (kc, tq), 1)
        keys_ref[pl.ds(r0, kc), :] = jnp.where(krow <= qcol, key, INT_MIN)
        return carry

    lax.fori_loop(0, nchunk, idx_chunk, 0)

    def count_ge(cand):
        def body(c, cnt):
            r0 = pl.multiple_of(c * kc, kc)
            ge = (keys_ref[pl.ds(r0, kc), :] >= cand).astype(I32)
            return cnt + jnp.sum(ge.reshape(kc // 8, 8, tq), axis=0)

        cnt8 = lax.fori_loop(0, nchunk, body, jnp.zeros((8, tq), I32))
        return jnp.sum(cnt8, axis=0, keepdims=True)

    def bit_iter(bi, tu):
        cand_u = tu | jnp.left_shift(jnp.int32(1), 31 - bi)
        cnt = count_ge(cand_u ^ INT_MIN)
        return jnp.where(cnt >= topk, cand_u, tu)

    tu = lax.fori_loop(0, 32, bit_iter, jnp.zeros((1, tq), I32))
    thr = tu ^ INT_MIN
    tsel = jnp.maximum(thr, INT_MIN + 1)

    n_ge = count_ge(tsel)
    has_tie = jnp.max((n_ge > topk).astype(I32)) > 0

    @pl.when(has_tie)
    def _():
        n_gt = count_ge(tsel + 1)
        need = topk - n_gt

        def count_eq_below(bound):
            def body(c, cnt):
                r0 = pl.multiple_of(c * kc, kc)
                krow = r0 + lax.broadcasted_iota(I32, (kc, tq), 0)
                hit = ((keys_ref[pl.ds(r0, kc), :] == tsel) & (krow < bound)).astype(I32)
                return cnt + jnp.sum(hit.reshape(kc // 8, 8, tq), axis=0)

            cnt8 = lax.fori_loop(0, nchunk, body, jnp.zeros((8, tq), I32))
            return jnp.sum(cnt8, axis=0, keepdims=True)

        def jbit(bi, jlo):
            cand = jlo | jnp.left_shift(jnp.int32(1), 30 - bi)
            return jnp.where(count_eq_below(cand) < need, cand, jlo)

        jstar = lax.fori_loop(0, 31, jbit, jnp.zeros((1, tq), I32))

        def drop(c, carry):
            r0 = pl.multiple_of(c * kc, kc)
            krow = r0 + lax.broadcasted_iota(I32, (kc, tq), 0)
            kk = keys_ref[pl.ds(r0, kc), :]
            keys_ref[pl.ds(r0, kc), :] = jnp.where((kk == tsel) & (krow > jstar), INT_MIN, kk)
            return carry

        lax.fori_loop(0, nchunk, drop, 0)

    def head_body(h, carry):
        qp = qtp_ref[h]

        def chunk_body(c, st):
            m, l, acc = st
            r0 = pl.multiple_of(c * kc, kc)
            s = jnp.dot(k_ref[h // 2, pl.ds(r0, kc), :], qp, preferred_element_type=F32)
            sel = keys_ref[pl.ds(r0, kc), :] >= tsel
            s = jnp.where(sel, s, -jnp.inf)
            m_new = jnp.maximum(m, jnp.max(s, axis=0, keepdims=True))
            alpha = jnp.exp(m - m_new)
            p = jnp.exp(s - m_new)
            l = alpha * l + jnp.sum(p, axis=0, keepdims=True)
            acc = alpha * acc + jnp.dot(vt_ref[h, c], p.astype(BF16), preferred_element_type=F32)
            return m_new, l, acc

        init = (jnp.full((1, tq), NEG_BIG, F32), jnp.zeros((1, tq), F32),
                jnp.zeros((HEAD_DIM, tq), F32))
        _, l, acc = lax.fori_loop(0, nchunk, chunk_body, init)
        ot_ref[h] = acc / l
        return carry

    lax.fori_loop(0, N_HEADS, head_body, 0)
    o_ref[...] = ot_ref[...].reshape(ATTN_WIDTH, tq).T.astype(BF16)


def _prompt_attention(p_main, p_small, ki, k_hp, vt, *, batch, seq, tq, kc, topk,
                      q_col, qi_col):
    nq = seq // tq
    body = functools.partial(_prompt_attn_body, tq=tq, kc=kc, topk=topk)
    return pl.pallas_call(
        body,
        grid=(batch, nq),
        in_specs=[
            pl.BlockSpec((tq, ATTN_WIDTH), lambda b, i: (b * nq + i, qi_col)),
            pl.BlockSpec((tq, LANES), lambda b, i: (b * nq + i, 0)),
            pl.BlockSpec((tq, ATTN_WIDTH), lambda b, i: (b * nq + i, q_col)),
            pl.BlockSpec((None, seq, IDX_DIM), lambda b, i: (b, 0, 0)),
            pl.BlockSpec((None, N_HEADS // 2, seq, 2 * HEAD_DIM), lambda b, i: (b, 0, 0, 0)),
            pl.BlockSpec((None, N_HEADS, seq // kc, HEAD_DIM, kc), lambda b, i: (b, 0, 0, 0, 0)),
        ],
        out_specs=pl.BlockSpec((tq, ATTN_WIDTH), lambda b, i: (b * nq + i, 0)),
        out_shape=jax.ShapeDtypeStruct((batch * seq, ATTN_WIDTH), BF16),
        scratch_shapes=[
            pltpu.VMEM((seq, tq), I32),
            pltpu.VMEM((IDX_HEADS, IDX_DIM, tq), BF16),
            pltpu.VMEM((N_HEADS, 2 * HEAD_DIM, tq), BF16),
            pltpu.VMEM((N_HEADS, HEAD_DIM, tq), F32),
        ],
        compiler_params=_params("parallel", "arbitrary"),
        name="prompt_sparse_attn",
    )(p_main, p_small, p_main, ki, k_hp, vt)


def _pool_prompt_body(u_ref, wg_ref, sc_ref, o_ref, ue_ref, *, tm):
    i = pl.program_id(1)

    @pl.when(i == 0)
    def _():
        ue_ref[0:POOL_HALO, :] = jnp.zeros((POOL_HALO, ue_ref.shape[1]), F32)

    u = u_ref[...]
    ue_ref[POOL_HALO:POOL_HALO + tm, :] = u
    pos = i * tm + lax.broadcasted_iota(I32, (tm, 1), 0)
    outs = []
    for g, w in enumerate(POOL_WINDOWS):
        cols = slice(g * POOL_GROUP_WIDTH, (g + 1) * POOL_GROUP_WIDTH)
        win = u[:, cols]
        for back in range(1, w):
            win = win + ue_ref[POOL_HALO - back:POOL_HALO - back + tm, cols]
        cnt = jnp.minimum(pos + 1, w).astype(F32)
        mixed = win / cnt - u[:, cols]
        outs.append(jnp.dot(mixed.astype(BF16), wg_ref[g], preferred_element_type=F32))
    o_ref[...] = (jnp.concatenate(outs, axis=1) * sc_ref[...]).astype(BF16)
    ue_ref[0:POOL_HALO, :] = u[tm - POOL_HALO:, :]


def _pool_prompt(p_main, wg, scale, *, batch, seq, tm, u_col):
    nblk = seq // tm
    width = wg.shape[0] * wg.shape[1]
    return pl.pallas_call(
        functools.partial(_pool_prompt_body, tm=tm),
        grid=(batch, nblk),
        in_specs=[
            pl.BlockSpec((tm, width), lambda b, i: (b * nblk + i, u_col)),
            pl.BlockSpec(wg.shape, lambda b, i: (0, 0, 0)),
            pl.BlockSpec((1, width), lambda b, i: (0, 0)),
        ],
        out_specs=pl.BlockSpec((tm, width), lambda b, i: (b * nblk + i, 0)),
        out_shape=jax.ShapeDtypeStruct((batch * seq, width), BF16),
        scratch_shapes=[pltpu.VMEM((POOL_HALO + tm, width), F32)],
        compiler_params=_params("parallel", "arbitrary"),
        name="pool_prompt",
    )(p_main, wg, scale)


def _pool_sample_body(st_ref, u_ref, wg_ref, sc_ref, o_ref):
    st = st_ref[...]
    u = u_ref[...]
    n_state = st.shape[1]
    outs = []
    for g, w in enumerate(POOL_WINDOWS):
        cols = slice(g * POOL_GROUP_WIDTH, (g + 1) * POOL_GROUP_WIDTH)
        win = u[:, cols] + jnp.sum(st[:, n_state - (w - 1):, cols], axis=1)
        mixed = win / float(w) - u[:, cols]
        outs.append(jnp.dot(mixed.astype(BF16), wg_ref[g], preferred_element_type=F32))
    o_ref[...] = (jnp.concatenate(outs, axis=1) * sc_ref[...]).astype(BF16)


def _pool_sample(state, u_new, wg, scale):
    nb, width = u_new.shape
    return pl.pallas_call(
        _pool_sample_body,
        out_shape=jax.ShapeDtypeStruct((nb, width), BF16),
        compiler_params=pltpu.CompilerParams(vmem_limit_bytes=VMEM_LIMIT_BYTES),
        name="pool_sample",
    )(state, u_new, wg, scale)


def _merge_body(x_ref, ao_ref, po_ref, ga_ref, gb_ref, bg_ref, wa_ref, wp_ref, wo_ref, g2_ref,
                h_ref, z_ref):
    ya = jnp.dot(ao_ref[...], wa_ref[...], preferred_element_type=F32)
    yb = jnp.dot(po_ref[...], wp_ref[...], preferred_element_type=F32)
    merged = (jax.nn.sigmoid(ga_ref[...] + bg_ref[0:1, :]) * ya
              + jax.nn.sigmoid(gb_ref[...] + bg_ref[1:2, :]) * yb)
    h = x_ref[...] + jnp.dot(merged.astype(BF16), wo_ref[...], preferred_element_type=F32)
    h_ref[...] = h
    ms = jnp.mean(h * h, axis=-1, keepdims=True)
    z_ref[...] = (h * lax.rsqrt(ms + EPS) * g2_ref[...]).astype(BF16)


def _merge(x, attn_o, pool_o, p_main, b_gate, wa, wp, wo, g2, *, tm, ga_col, gb_col):
    m, d = x.shape
    const = lambda i: (0, 0)
    return pl.pallas_call(
        _merge_body,
        grid=(m // tm,),
        in_specs=[
            pl.BlockSpec((tm, d), lambda i: (i, 0)),
            pl.BlockSpec((tm, attn_o.shape[1]), lambda i: (i, 0)),
            pl.BlockSpec((tm, pool_o.shape[1]), lambda i: (i, 0)),
            pl.BlockSpec((tm, d), lambda i: (i, ga_col)),
            pl.BlockSpec((tm, d), lambda i: (i, gb_col)),
            pl.BlockSpec(b_gate.shape, const),
            pl.BlockSpec(wa.shape, const),
            pl.BlockSpec(wp.shape, const),
            pl.BlockSpec(wo.shape, const),
            pl.BlockSpec((1, d), const),
        ],
        out_specs=[pl.BlockSpec((tm, d), lambda i: (i, 0)), pl.BlockSpec((tm, d), lambda i: (i, 0))],
        out_shape=[jax.ShapeDtypeStruct((m, d), F32), jax.ShapeDtypeStruct((m, d), BF16)],
        compiler_params=_params("parallel"),
        name="gated_merge",
    )(x, attn_o, pool_o, p_main, p_main, b_gate, wa, wp, wo, g2)


def _ffn_body(z_ref, h_ref, wu_ref, wd_ref, gf_ref, y_ref, acc_ref):
    f = pl.program_id(1)

    @pl.when(f == 0)
    def _():
        acc_ref[...] = h_ref[...]

    a = jnp.dot(z_ref[...], wu_ref[...], preferred_element_type=F32)
    a = jnp.square(jnp.maximum(a, 0.0))
    acc_ref[...] += jnp.dot(a.astype(BF16), wd_ref[...], preferred_element_type=F32)

    @pl.when(f == pl.num_programs(1) - 1)
    def _():
        h2 = acc_ref[...]
        ms = jnp.mean(h2 * h2, axis=-1, keepdims=True)
        y_ref[...] = h2 * lax.rsqrt(ms + EPS) * gf_ref[...]


def _ffn(z, h, wu, wd, gf, *, tm, tf):
    m, d = h.shape
    dff = wu.shape[1]
    return pl.pallas_call(
        _ffn_body,
        grid=(m // tm, dff // tf),
        in_specs=[
            pl.BlockSpec((tm, d), lambda i, f: (i, 0)),
            pl.BlockSpec((tm, d), lambda i, f: (i, 0)),
            pl.BlockSpec((d, tf), lambda i, f: (0, f)),
            pl.BlockSpec((tf, d), lambda i, f: (f, 0)),
            pl.BlockSpec((1, d), lambda i, f: (0, 0)),
        ],
        out_specs=pl.BlockSpec((tm, d), lambda i, f: (i, 0)),
        out_shape=jax.ShapeDtypeStruct((m, d), F32),
        scratch_shapes=[pltpu.VMEM((tm, d), F32)],
        compiler_params=_params("parallel", "arbitrary"),
        name="ffn_final_norm",
    )(z, h, wu, wd, gf)


def _sample_scores_body(pt_ref, qi_ref, w_ref, kin_ref, ck_ref, o_ref, buf, sem, *, n_pages, page):
    b = pl.program_id(0)
    nb = pl.num_programs(0)
    slot = b % 2

    def page_copy(bb, sl, p):
        return pltpu.make_async_copy(ck_ref.at[0, pt_ref[bb, p]], buf.at[sl, p], sem.at[sl])

    def start(bb, sl):
        def body(p, carry):
            page_copy(bb, sl, p).start()
            return carry

        lax.fori_loop(0, n_pages, body, 0)

    @pl.when(b == 0)
    def _():
        start(0, 0)

    @pl.when(b + 1 < nb)
    def _():
        start(b + 1, 1 - slot)

    def wait_body(p, carry):
        page_copy(b, slot, p).wait()
        return carry

    lax.fori_loop(0, n_pages, wait_body, 0)

    row = lax.broadcasted_iota(I32, (page, IDX_DIM), 0)
    buf[slot, n_pages] = jnp.where(row == 0, kin_ref[...], 0.0)

    kb = buf[slot].reshape((n_pages + 1) * page, IDX_DIM).astype(BF16)
    s = lax.dot_general(qi_ref[...].astype(BF16), kb, (((1,), (1,)), ((), ())),
                        preferred_element_type=F32)
    wcol = w_ref[...] * (IDX_HEADS ** -0.5 * IDX_DIM ** -0.5)
    o_ref[...] = jnp.sum(jnp.maximum(s, 0.0) * wcol, axis=0, keepdims=True)


def _sample_scores(page_table, qi, w, ki_new, cache_kidx):
    nb, n_pages = page_table.shape
    page = cache_kidx.shape[2]
    length = (n_pages + 1) * page
    grid_spec = pltpu.PrefetchScalarGridSpec(
        num_scalar_prefetch=1,
        grid=(nb,),
        in_specs=[
            pl.BlockSpec((None, IDX_HEADS, IDX_DIM), lambda b, pt: (b, 0, 0)),
            pl.BlockSpec((None, IDX_HEADS, 1), lambda b, pt: (b, 0, 0)),
            pl.BlockSpec((None, 1, IDX_DIM), lambda b, pt: (b, 0, 0)),
            pl.BlockSpec(memory_space=pl.ANY),
        ],
        out_specs=pl.BlockSpec((None, 1, length), lambda b, pt: (b, 0, 0)),
        scratch_shapes=[
            pltpu.VMEM((2, n_pages + 1, page, IDX_DIM), F32),
            pltpu.SemaphoreType.DMA((2,)),
        ],
    )
    return pl.pallas_call(
        functools.partial(_sample_scores_body, n_pages=n_pages, page=page),
        grid_spec=grid_spec,
        out_shape=jax.ShapeDtypeStruct((nb, 1, length), F32),
        compiler_params=_params("arbitrary"),
        name="sample_indexer_scores",
    )(page_table, qi, w, ki_new, cache_kidx)


def _prefix_exclusive(mask_f32, tri):
    rows, length = mask_f32.shape
    off = jnp.zeros((rows, 1), F32)
    parts = []
    for c in range(length // LANES):
        mc = mask_f32[:, c * LANES:(c + 1) * LANES]
        inc = jnp.dot(mc.astype(BF16), tri, preferred_element_type=F32)
        parts.append(inc - mc + off)
        off = off + inc[:, LANES - 1:LANES]
    return jnp.concatenate(parts, axis=1)


def _sample_select_body(sc_ref, idx_ref, meta_ref, rank_ref, *, past, topk, lane_chunk):
    sc = sc_ref[...]
    nb, length = sc.shape
    pos = lax.broadcasted_iota(I32, (nb, length), 1)
    key = jnp.where(pos <= past, _sortable_key(sc), INT_MIN)

    def bit_iter(bi, tu):
        cand_u = tu | jnp.left_shift(jnp.int32(1), 31 - bi)
        cnt = jnp.sum((key >= (cand_u ^ INT_MIN)).astype(I32), axis=1, keepdims=True)
        return jnp.where(cnt >= topk, cand_u, tu)

    tu = lax.fori_loop(0, 32, bit_iter, jnp.zeros((nb, 1), I32))
    thr = tu ^ INT_MIN
    valid = key > INT_MIN
    gt = key > thr
    eq = key == thr
    tri = (lax.broadcasted_iota(I32, (LANES, LANES), 0)
           <= lax.broadcasted_iota(I32, (LANES, LANES), 1)).astype(BF16)
    need = (topk - jnp.sum(gt.astype(I32), axis=1, keepdims=True)).astype(F32)
    eq_rank = _prefix_exclusive(eq.astype(F32), tri)
    sel = (gt | (eq & (eq_rank < need))) & valid

    sel_past = sel & (pos < past)
    sel_past_f = sel_past.astype(F32)
    rank = _prefix_exclusive(sel_past_f, tri)
    rank_ref[...] = jnp.where(sel_past, rank, -1.0)
    n_past = jnp.sum(sel_past_f, axis=1, keepdims=True).astype(I32)
    new_flag = jnp.sum((sel & (pos == past)).astype(I32), axis=1, keepdims=True)
    lane = lax.broadcasted_iota(I32, (nb, LANES), 1)
    meta_ref[...] = jnp.where(lane == 0, n_past, jnp.where(lane == 1, new_flag, 0))

    slot_id = lax.broadcasted_iota(I32, (topk, lane_chunk), 0).astype(F32)
    sub = lax.broadcasted_iota(I32, (16, lane_chunk), 0)

    def per_sample(b, carry):
        acc = jnp.zeros((16, topk), F32)
        for c in range(length // lane_chunk):
            lo = c * lane_chunk
            r = rank_ref[pl.ds(b, 1), lo:lo + lane_chunk]
            onehot = (jnp.broadcast_to(r, (topk, lane_chunk)) == slot_id).astype(BF16)
            p = lo + lax.broadcasted_iota(I32, (16, lane_chunk), 1)
            coords = jnp.where(sub == 0, p // LANES, jnp.where(sub == 1, p % LANES, 0))
            acc = acc + lax.dot_general(coords.astype(F32).astype(BF16), onehot,
                                        (((1,), (1,)), ((), ())), preferred_element_type=F32)
        idx_ref[pl.ds(b, 1), :] = (acc[0:1, :] * float(LANES) + acc[1:2, :]).astype(I32)
        return carry

    lax.fori_loop(0, nb, per_sample, 0)


def _sample_select(scores, *, past, topk):
    nb, length = scores.shape
    n_lane_tiles = length // LANES
    tiles = max(t for t in range(1, 9) if n_lane_tiles % t == 0)
    return pl.pallas_call(
        functools.partial(_sample_select_body, past=past, topk=topk, lane_chunk=tiles * LANES),
        out_shape=[jax.ShapeDtypeStruct((nb, topk), I32), jax.ShapeDtypeStruct((nb, LANES), I32)],
        scratch_shapes=[pltpu.VMEM((nb, length), F32)],
        compiler_params=pltpu.CompilerParams(vmem_limit_bytes=VMEM_LIMIT_BYTES),
        name="sample_topk_select",
    )(scores)


def _sample_attend_body(idx_ref, meta_ref, pt_ref, q_ref, kn_ref, vn_ref, ck_ref, cv_ref, o_ref,
                        kbuf, vbuf, sem, *, topk, page_shift, page_mask):
    b = pl.program_id(0)
    nb = pl.num_programs(0)
    slot = b % 2

    def row_copies(bb, sl, j):
        pos = idx_ref[bb, j]
        pg = pt_ref[bb, lax.shift_right_logical(pos, page_shift)]
        off = pos & page_mask
        return (pltpu.make_async_copy(ck_ref.at[0, pg, off], kbuf.at[sl, j], sem.at[0, sl]),
                pltpu.make_async_copy(cv_ref.at[0, pg, off], vbuf.at[sl, j], sem.at[1, sl]))

    def start(bb, sl):
        def body(j, carry):
            ck, cv = row_copies(bb, sl, j)
            ck.start()
            cv.start()
            return carry

        lax.fori_loop(0, topk, body, 0)

    @pl.when(b == 0)
    def _():
        start(0, 0)

    @pl.when(b + 1 < nb)
    def _():
        start(b + 1, 1 - slot)

    def wait_body(j, carry):
        ck, cv = row_copies(b, slot, j)
        ck.wait()
        cv.wait()
        return carry

    lax.fori_loop(0, topk, wait_body, 0)

    n_past = meta_ref[b, 0]
    new_flag = meta_ref[b, 1]
    scale = HEAD_DIM ** -0.5
    q = q_ref[...]
    s = jnp.sum(kbuf[slot] * q[None], axis=-1, keepdims=True) * scale
    jj = lax.broadcasted_iota(I32, s.shape, 0)
    s = jnp.where(jj < n_past, s, -jnp.inf)
    s_new = jnp.sum(kn_ref[...] * q, axis=-1, keepdims=True) * scale
    s_new = jnp.where(new_flag > 0, s_new, -jnp.inf)
    m = jnp.maximum(jnp.max(s, axis=0), s_new)
    m = jnp.maximum(m, NEG_BIG)
    p = jnp.exp(s - m[None])
    p_new = jnp.exp(s_new - m)
    l = jnp.sum(p, axis=0) + p_new
    o = jnp.sum(p * vbuf[slot], axis=0) + p_new * vn_ref[...]
    o_ref[...] = o / l


def _sample_attend(idx, meta, page_table, q, k_new, v_new, cache_k, cache_v, *, topk):
    nb = q.shape[0]
    page = cache_k.shape[2]
    assert page & (page - 1) == 0, "page size must be a power of two"
    head_block = pl.BlockSpec((None, N_HEADS, HEAD_DIM), lambda b, *_: (b, 0, 0))
    grid_spec = pltpu.PrefetchScalarGridSpec(
        num_scalar_prefetch=3,
        grid=(nb,),
        in_specs=[head_block, head_block, head_block,
                  pl.BlockSpec(memory_space=pl.ANY), pl.BlockSpec(memory_space=pl.ANY)],
        out_specs=head_block,
        scratch_shapes=[
            pltpu.VMEM((2, topk, N_HEADS, HEAD_DIM), F32),
            pltpu.VMEM((2, topk, N_HEADS, HEAD_DIM), F32),
            pltpu.SemaphoreType.DMA((2, 2)),
        ],
    )
    return pl.pallas_call(
        functools.partial(_sample_attend_body, topk=topk, page_shift=page.bit_length() - 1,
                          page_mask=page - 1),
        grid_spec=grid_spec,
        out_shape=jax.ShapeDtypeStruct((nb, N_HEADS, HEAD_DIM), F32),
        compiler_params=_params("arbitrary"),
        name="sample_gather_attend",
    )(idx, meta, page_table, q, k_new, v_new, cache_k, cache_v)


def _row_tile(m, pref):
    return pref if m % pref == 0 else m


def kernel(x_prompt, x_sample, cache_k, cache_v, cache_kidx, state_pool, page_table, norm_mix_g,
           w_in, b_gate, w_attn_out, w_pool_group, pool_scale, w_pool_out, w_out, norm_ffn_g,
           w_up, w_down, norm_final_g):
    depth = w_in.shape[0]
    assert depth == 1, "single-layer step"
    batch, seq, d_model = x_prompt.shape
    dec_batch, dec_seq, _ = x_sample.shape
    assert dec_seq == 1, "one new token per sample"
    n_pages = page_table.shape[1]
    page = cache_k.shape[2]
    past = n_pages * page
    pool_width = pool_scale.shape[1]
    l = 0

    c_ki = 3 * ATTN_WIDTH + IDX_HEADS * IDX_DIM
    c_u = c_ki + IDX_DIM + IDX_HEADS
    c_ga = c_u + pool_width
    w_main = jnp.concatenate([w_in[l, :, :c_ki], w_in[l, :, c_ga:], w_in[l, :, c_u:c_ga]],
                             axis=1).astype(BF16)
    w_side = jnp.pad(w_in[l, :, c_ki:c_u], ((0, 0), (0, LANES - (c_u - c_ki)))).astype(BF16)
    tn = ATTN_WIDTH
    assert pool_width == tn and d_model == 2 * tn and IDX_HEADS * IDX_DIM == tn
    Q_COL, K_COL, V_COL, QI_COL, U_COL = 0, 1, 2, 3, 8
    GA_COL, GB_COL = 2, 3
    g_mix = norm_mix_g[l][None, :]
    wa = w_attn_out[l].astype(BF16)
    wg = w_pool_group[l].astype(BF16)
    wp = w_pool_out[l].astype(BF16)
    wo = w_out[l].astype(BF16)
    wu = w_up[l].astype(BF16)
    wd = w_down[l].astype(BF16)
    g_ffn = norm_ffn_g[l][None, :]
    g_fin = norm_final_g[None, :]
    scale = pool_scale[l][None, :]

    def col(p, c, width=tn):
        return p[:, c * tn:c * tn + width]

    xp = x_prompt.reshape(batch * seq, d_model)
    tm_p = _row_tile(batch * seq, 512)
    pp = _rms_matmul(xp, g_mix, w_main, tm_p, tn)
    pp_side = _rms_matmul(xp, g_mix, w_side, tm_p, LANES)

    tq, kc = 256, 128
    topk_p = min(TOPK_MAX, seq // 4)
    k_p = col(pp, K_COL)
    v_p = col(pp, V_COL)
    ki_p = pp_side[:, :IDX_DIM]
    ki_bf = ki_p.astype(BF16).reshape(batch, seq, IDX_DIM)
    k_hp = k_p.astype(BF16).reshape(batch, seq, N_HEADS // 2, 2 * HEAD_DIM).transpose(0, 2, 1, 3)
    vt = v_p.astype(BF16).reshape(batch, seq // kc, kc, N_HEADS, HEAD_DIM).transpose(0, 3, 1, 4, 2)
    attn_p = _prompt_attention(pp, pp_side, ki_bf, k_hp, vt, batch=batch, seq=seq, tq=tq, kc=kc,
                               topk=topk_p, q_col=Q_COL, qi_col=QI_COL)
    pool_p = _pool_prompt(pp, wg, scale, batch=batch, seq=seq, tm=256, u_col=U_COL)
    h_p, z_p = _merge(xp, attn_p, pool_p, pp, b_gate[l], wa, wp, wo, g_ffn, tm=256,
                      ga_col=GA_COL, gb_col=GB_COL)
    y_p = _ffn(z_p, h_p, wu, wd, g_fin, tm=512, tf=512)

    xs = x_sample.reshape(dec_batch, d_model)
    ps = _rms_matmul(xs, g_mix, w_main, dec_batch, tn)
    ps_side = _rms_matmul(xs, g_mix, w_side, dec_batch, LANES)
    topk_s = min(TOPK_MAX, (past + dec_seq) // 4)
    q_s = col(ps, Q_COL).reshape(dec_batch, N_HEADS, HEAD_DIM)
    k_s = col(ps, K_COL).reshape(dec_batch, N_HEADS, HEAD_DIM)
    v_s = col(ps, V_COL).reshape(dec_batch, N_HEADS, HEAD_DIM)
    qi_s = col(ps, QI_COL).reshape(dec_batch, IDX_HEADS, IDX_DIM)
    ki_s = ps_side[:, :IDX_DIM]
    wi_s = ps_side[:, IDX_DIM:IDX_DIM + IDX_HEADS]
    u_s = col(ps, U_COL)
    scores = _sample_scores(page_table, qi_s, wi_s[:, :, None], ki_s[:, None, :], cache_kidx[l][None])
    idx, meta = _sample_select(scores.reshape(dec_batch, -1), past=past, topk=topk_s)
    attn_s = _sample_attend(idx, meta, page_table, q_s, k_s, v_s, cache_k[l][None], cache_v[l][None],
                            topk=topk_s)
    pool_s = _pool_sample(state_pool[l], u_s, wg, scale)
    h_s, z_s = _merge(xs, attn_s.reshape(dec_batch, ATTN_WIDTH).astype(BF16), pool_s, ps,
                      b_gate[l], wa, wp, wo, g_ffn, tm=dec_batch, ga_col=GA_COL, gb_col=GB_COL)
    y_s = _ffn(z_s, h_s, wu, wd, g_fin, tm=dec_batch, tf=512)

    n_state = state_pool.shape[2]
    u_p = col(pp, U_COL).reshape(batch, seq, pool_width)
    pool_state_s = jnp.concatenate([state_pool[l].astype(F32), u_s[:, None, :]], axis=1)[:, -n_state:]
    return (
        y_p.reshape(batch, seq, d_model),
        y_s.reshape(dec_batch, dec_seq, d_model),
        k_p.reshape(1, batch, seq, N_HEADS, HEAD_DIM),
        v_p.reshape(1, batch, seq, N_HEADS, HEAD_DIM),
        ki_p.reshape(1, batch, seq, IDX_DIM),
        u_p[:, -n_state:][None],
        k_s.reshape(1, dec_batch, dec_seq, N_HEADS, HEAD_DIM),
        v_s.reshape(1, dec_batch, dec_seq, N_HEADS, HEAD_DIM),
        ki_s.reshape(1, dec_batch, dec_seq, IDX_DIM),
        pool_state_s[None],
    )
```

```python
import functools

import jax
import jax.numpy as jnp
from jax import lax
from jax.experimental import pallas as pl
from jax.experimental.pallas import tpu as pltpu

F32 = jnp.float32
BF16 = jnp.bfloat16
I32 = jnp.int32

N_HEADS = 16
HEAD_DIM = 64
ATTN_WIDTH = N_HEADS * HEAD_DIM
IDX_HEADS = 16
IDX_DIM = 64
TOPK_MAX = 256
POOL_WINDOWS = (2, 4, 8, 16)
POOL_GROUP_WIDTH = 256
POOL_HALO = 16
EPS = 1e-6
INT_MIN = -(2 ** 31)
LANES = 128
NEG_BIG = -1e30

VMEM_LIMIT_BYTES = 56 * 1024 * 1024

PROJ_ROWS = 512
PROJ_COLS = 1024
ATTN_QUERIES = 256
ATTN_KEYS = 128
POOL_ROWS = 256
MERGE_ROWS = 256
FFN_ROWS = 512
FFN_COLS = 512
SCORE_PAGES = 16
ATTEND_PAGES = 8


def _params(*sem):
    return pltpu.CompilerParams(dimension_semantics=sem, vmem_limit_bytes=VMEM_LIMIT_BYTES)


def _sortable_key(x):
    b = lax.bitcast_convert_type(x, I32)
    return jnp.where(b >= 0, b, b ^ jnp.int32(0x7FFFFFFF))


def _rms_matmul_body(x_ref, g_ref, w_ref, o_ref, z_ref):
    @pl.when(pl.program_id(1) == 0)
    def _():
        x = x_ref[...]
        ms = jnp.mean(x * x, axis=-1, keepdims=True)
        z_ref[...] = (x * lax.rsqrt(ms + EPS) * g_ref[...]).astype(BF16)

    o_ref[...] = jnp.dot(z_ref[...], w_ref[...], preferred_element_type=F32)


def _rms_matmul(x, g, w, tm, tn):
    m, d = x.shape
    n = w.shape[1]
    return pl.pallas_call(
        _rms_matmul_body,
        grid=(m // tm, n // tn),
        in_specs=[
            pl.BlockSpec((tm, d), lambda i, j: (i, 0)),
            pl.BlockSpec((1, d), lambda i, j: (0, 0)),
            pl.BlockSpec((d, tn), lambda i, j: (0, j)),
        ],
        out_specs=pl.BlockSpec((tm, tn), lambda i, j: (i, j)),
        out_shape=jax.ShapeDtypeStruct((m, n), F32),
        scratch_shapes=[pltpu.VMEM((tm, d), BF16)],
        compiler_params=_params("parallel", "arbitrary"),
        name="rms_in_proj",
    )(x, g, w)


def _prompt_attn_body(qi_ref, wi_ref, q_ref, ki_ref, k_ref, vt_ref, o_ref,
                      keys_ref, qit_ref, qtp_ref, ot_ref, m_ref, l_ref, a_ref, s_ref, p_ref,
                      *, tq, kc, topk):
    i = pl.program_id(1)
    nchunk = (i + 1) * (tq // kc)
    q0 = i * tq

    qit_ref[...] = qi_ref[...].T.astype(BF16).reshape(IDX_HEADS, IDX_DIM, tq)
    w16 = wi_ref[...].T[IDX_DIM:IDX_DIM + IDX_HEADS, :] * (IDX_HEADS ** -0.5 * IDX_DIM ** -0.5)
    qt = (q_ref[...] * (HEAD_DIM ** -0.5)).T.reshape(N_HEADS // 2, 2 * HEAD_DIM, tq)
    first_half = lax.broadcasted_iota(I32, (2 * HEAD_DIM, tq), 0) < HEAD_DIM
    for j in range(N_HEADS // 2):
        qtp_ref[2 * j] = jnp.where(first_half, qt[j], 0.0).astype(BF16)
        qtp_ref[2 * j + 1] = jnp.where(first_half, 0.0, qt[j]).astype(BF16)

    def idx_chunk(c, carry):
        r0 = pl.multiple_of(c * kc, kc)
        kic = ki_ref[pl.ds(r0, kc), :]
        acc = jnp.zeros((kc, tq), F32)
        for h in range(IDX_HEADS):
            s = jnp.dot(kic, qit_ref[h], preferred_element_type=F32)
            acc = acc + jnp.maximum(s, 0.0) * w16[h:h + 1, :]
        key = _sortable_key(acc)
        krow = r0 + lax.broadcasted_iota(I32, (kc, tq), 0)
        qcol = q0 + lax.broadcasted_iota(I32, (kc, tq), 1)
        keys_ref[pl.ds(r0, kc), :] = jnp.where(krow <= qcol, key, INT_MIN)
        return carry

    lax.fori_loop(0, nchunk, idx_chunk, 0)

    def count_ge(cand):
        def body(c, cnt):
            r0 = pl.multiple_of(c * kc, kc)
            ge = (keys_ref[pl.ds(r0, kc), :] >= cand).astype(I32)
            return cnt + jnp.sum(ge.reshape(kc // 8, 8, tq), axis=0)

        cnt8 = lax.fori_loop(0, nchunk, body, jnp.zeros((8, tq), I32))
        return jnp.sum(cnt8, axis=0, keepdims=True)

    def bit_iter(bi, tu):
        cand_u = tu | jnp.left_shift(jnp.int32(1), 31 - bi)
        cnt = count_ge(cand_u ^ INT_MIN)
        return jnp.where(cnt >= topk, cand_u, tu)

    tu = lax.fori_loop(0, 32, bit_iter, jnp.zeros((1, tq), I32))
    thr = tu ^ INT_MIN
    tsel = jnp.maximum(thr, INT_MIN + 1)

    n_ge = count_ge(tsel)
    has_tie = jnp.max((n_ge > topk).astype(I32)) > 0

    def count_eq_below(bound):
        def body(c, cnt):
            r0 = pl.multiple_of(c * kc, kc)
            krow = r0 + lax.broadcasted_iota(I32, (kc, tq), 0)
            hit = ((keys_ref[pl.ds(r0, kc), :] == tsel) & (krow < bound)).astype(I32)
            return cnt + jnp.sum(hit.reshape(kc // 8, 8, tq), axis=0)

        cnt8 = lax.fori_loop(0, nchunk, body, jnp.zeros((8, tq), I32))
        return jnp.sum(cnt8, axis=0, keepdims=True)

    @pl.when(has_tie)
    def _():
        need = topk - (n_ge - count_eq_below(jnp.int32(2 ** 30)))

        def jbit(bi, jlo):
            cand = jlo | jnp.left_shift(jnp.int32(1), 29 - bi)
            return jnp.where(count_eq_below(cand) < need, cand, jlo)

        jstar = lax.fori_loop(0, 30, jbit, jnp.zeros((1, tq), I32))

        def drop(c, carry):
            r0 = pl.multiple_of(c * kc, kc)
            krow = r0 + lax.broadcasted_iota(I32, (kc, tq), 0)
            kk = keys_ref[pl.ds(r0, kc), :]
            keys_ref[pl.ds(r0, kc), :] = jnp.where((kk == tsel) & (krow > jstar), INT_MIN, kk)
            return carry

        lax.fori_loop(0, nchunk, drop, 0)

    m_ref[...] = jnp.full(m_ref.shape, NEG_BIG, F32)
    l_ref[...] = jnp.zeros(l_ref.shape, F32)
    ot_ref[...] = jnp.zeros(ot_ref.shape, F32)

    def chunk_body(c, carry):
        r0 = pl.multiple_of(c * kc, kc)
        bias = jnp.where(keys_ref[pl.ds(r0, kc), :] >= tsel, 0.0, -jnp.inf)
        for h in range(N_HEADS):
            s_ref[h] = jnp.dot(k_ref[h // 2, pl.ds(r0, kc), :], qtp_ref[h],
                               preferred_element_type=F32) + bias
        for h in range(N_HEADS):
            s = s_ref[h]
            m_old = m_ref[h]
            m_new = jnp.maximum(m_old, jnp.max(s, axis=0, keepdims=True))
            alpha = jnp.exp(m_old - m_new)
            p = jnp.exp(s - m_new)
            l_ref[h] = alpha * l_ref[h] + jnp.sum(p, axis=0, keepdims=True)
            p_ref[h] = p.astype(BF16)
            a_ref[h] = alpha
            m_ref[h] = m_new
        for h in range(N_HEADS):
            ot_ref[h] = a_ref[h] * ot_ref[h] + jnp.dot(vt_ref[h, c], p_ref[h],
                                                       preferred_element_type=F32)
        return carry

    lax.fori_loop(0, nchunk, chunk_body, 0)
    o = ot_ref[...] / l_ref[...]
    o_ref[...] = o.reshape(ATTN_WIDTH, tq).T.astype(BF16)


def _prompt_attention(p_main, p_small, ki, k_hp, vt, *, batch, seq, tq, kc, topk,
                      q_col, qi_col):
    nq = seq // tq
    body = functools.partial(_prompt_attn_body, tq=tq, kc=kc, topk=topk)
    return pl.pallas_call(
        body,
        grid=(batch, nq),
        in_specs=[
            pl.BlockSpec((tq, ATTN_WIDTH), lambda b, i: (b * nq + i, qi_col)),
            pl.BlockSpec((tq, LANES), lambda b, i: (b * nq + i, 0)),
            pl.BlockSpec((tq, ATTN_WIDTH), lambda b, i: (b * nq + i, q_col)),
            pl.BlockSpec((None, seq, IDX_DIM), lambda b, i: (b, 0, 0)),
            pl.BlockSpec((None, N_HEADS // 2, seq, 2 * HEAD_DIM), lambda b, i: (b, 0, 0, 0)),
            pl.BlockSpec((None, N_HEADS, seq // kc, HEAD_DIM, kc), lambda b, i: (b, 0, 0, 0, 0)),
        ],
        out_specs=pl.BlockSpec((tq, ATTN_WIDTH), lambda b, i: (b * nq + i, 0)),
        out_shape=jax.ShapeDtypeStruct((batch * seq, ATTN_WIDTH), BF16),
        scratch_shapes=[
            pltpu.VMEM((seq, tq), I32),
            pltpu.VMEM((IDX_HEADS, IDX_DIM, tq), BF16),
            pltpu.VMEM((N_HEADS, 2 * HEAD_DIM, tq), BF16),
            pltpu.VMEM((N_HEADS, HEAD_DIM, tq), F32),
            pltpu.VMEM((N_HEADS, 1, tq), F32),
            pltpu.VMEM((N_HEADS, 1, tq), F32),
            pltpu.VMEM((N_HEADS, 1, tq), F32),
            pltpu.VMEM((N_HEADS, kc, tq), F32),
            pltpu.VMEM((N_HEADS, kc, tq), BF16),
        ],
        compiler_params=_params("parallel", "arbitrary"),
        name="prompt_sparse_attn",
    )(p_main, p_small, p_main, ki, k_hp, vt)


def _pool_prompt_body(u_ref, wg_ref, sc_ref, o_ref, ue_ref, *, tm):
    i = pl.program_id(1)

    @pl.when(i == 0)
    def _():
        ue_ref[0:POOL_HALO, :] = jnp.zeros((POOL_HALO, ue_ref.shape[1]), F32)

    u = u_ref[...]
    ue_ref[POOL_HALO:POOL_HALO + tm, :] = u
    pos = i * tm + lax.broadcasted_iota(I32, (tm, 1), 0)
    outs = []
    for g, w in enumerate(POOL_WINDOWS):
        cols = slice(g * POOL_GROUP_WIDTH, (g + 1) * POOL_GROUP_WIDTH)
        win = u[:, cols]
        for back in range(1, w):
            win = win + ue_ref[POOL_HALO - back:POOL_HALO - back + tm, cols]
        cnt = jnp.minimum(pos + 1, w).astype(F32)
        mixed = win / cnt - u[:, cols]
        outs.append(jnp.dot(mixed.astype(BF16), wg_ref[g], preferred_element_type=F32))
    o_ref[...] = (jnp.concatenate(outs, axis=1) * sc_ref[...]).astype(BF16)
    ue_ref[0:POOL_HALO, :] = u[tm - POOL_HALO:, :]


def _pool_prompt(p_main, wg, scale, *, batch, seq, tm, u_col):
    nblk = seq // tm
    width = wg.shape[0] * wg.shape[1]
    return pl.pallas_call(
        functools.partial(_pool_prompt_body, tm=tm),
        grid=(batch, nblk),
        in_specs=[
            pl.BlockSpec((tm, width), lambda b, i: (b * nblk + i, u_col)),
            pl.BlockSpec(wg.shape, lambda b, i: (0, 0, 0)),
            pl.BlockSpec((1, width), lambda b, i: (0, 0)),
        ],
        out_specs=pl.BlockSpec((tm, width), lambda b, i: (b * nblk + i, 0)),
        out_shape=jax.ShapeDtypeStruct((batch * seq, width), BF16),
        scratch_shapes=[pltpu.VMEM((POOL_HALO + tm, width), F32)],
        compiler_params=_params("parallel", "arbitrary"),
        name="pool_prompt",
    )(p_main, wg, scale)


def _pool_sample_body(st_ref, u_ref, wg_ref, sc_ref, o_ref):
    st = st_ref[...]
    u = u_ref[...]
    n_state = st.shape[1]
    outs = []
    for g, w in enumerate(POOL_WINDOWS):
        cols = slice(g * POOL_GROUP_WIDTH, (g + 1) * POOL_GROUP_WIDTH)
        win = u[:, cols] + jnp.sum(st[:, n_state - (w - 1):, cols], axis=1)
        mixed = win / float(w) - u[:, cols]
        outs.append(jnp.dot(mixed.astype(BF16), wg_ref[g], preferred_element_type=F32))
    o_ref[...] = (jnp.concatenate(outs, axis=1) * sc_ref[...]).astype(BF16)


def _pool_sample(state, u_new, wg, scale):
    nb, width = u_new.shape
    return pl.pallas_call(
        _pool_sample_body,
        out_shape=jax.ShapeDtypeStruct((nb, width), BF16),
        compiler_params=pltpu.CompilerParams(vmem_limit_bytes=VMEM_LIMIT_BYTES),
        name="pool_sample",
    )(state, u_new, wg, scale)


def _merge_body(x_ref, ao_ref, po_ref, ga_ref, gb_ref, bg_ref, wa_ref, wp_ref, wo_ref, g2_ref,
                h_ref, z_ref):
    ya = jnp.dot(ao_ref[...], wa_ref[...], preferred_element_type=F32)
    yb = jnp.dot(po_ref[...], wp_ref[...], preferred_element_type=F32)
    merged = (jax.nn.sigmoid(ga_ref[...] + bg_ref[0:1, :]) * ya
              + jax.nn.sigmoid(gb_ref[...] + bg_ref[1:2, :]) * yb)
    h = x_ref[...] + jnp.dot(merged.astype(BF16), wo_ref[...], preferred_element_type=F32)
    h_ref[...] = h
    ms = jnp.mean(h * h, axis=-1, keepdims=True)
    z_ref[...] = (h * lax.rsqrt(ms + EPS) * g2_ref[...]).astype(BF16)


def _merge(x, attn_o, pool_o, p_main, b_gate, wa, wp, wo, g2, *, tm, ga_col, gb_col):
    m, d = x.shape
    const = lambda i: (0, 0)
    return pl.pallas_call(
        _merge_body,
        grid=(m // tm,),
        in_specs=[
            pl.BlockSpec((tm, d), lambda i: (i, 0)),
            pl.BlockSpec((tm, attn_o.shape[1]), lambda i: (i, 0)),
            pl.BlockSpec((tm, pool_o.shape[1]), lambda i: (i, 0)),
            pl.BlockSpec((tm, d), lambda i: (i, ga_col)),
            pl.BlockSpec((tm, d), lambda i: (i, gb_col)),
            pl.BlockSpec(b_gate.shape, const),
            pl.BlockSpec(wa.shape, const),
            pl.BlockSpec(wp.shape, const),
            pl.BlockSpec(wo.shape, const),
            pl.BlockSpec((1, d), const),
        ],
        out_specs=[pl.BlockSpec((tm, d), lambda i: (i, 0)), pl.BlockSpec((tm, d), lambda i: (i, 0))],
        out_shape=[jax.ShapeDtypeStruct((m, d), F32), jax.ShapeDtypeStruct((m, d), BF16)],
        compiler_params=_params("parallel"),
        name="gated_merge",
    )(x, attn_o, pool_o, p_main, p_main, b_gate, wa, wp, wo, g2)


def _ffn_body(z_ref, h_ref, wu_ref, wd_ref, gf_ref, y_ref, acc_ref):
    f = pl.program_id(1)

    @pl.when(f == 0)
    def _():
        acc_ref[...] = h_ref[...]

    a = jnp.dot(z_ref[...], wu_ref[...], preferred_element_type=F32)
    a = jnp.square(jnp.maximum(a, 0.0))
    acc_ref[...] += jnp.dot(a.astype(BF16), wd_ref[...], preferred_element_type=F32)

    @pl.when(f == pl.num_programs(1) - 1)
    def _():
        h2 = acc_ref[...]
        ms = jnp.mean(h2 * h2, axis=-1, keepdims=True)
        y_ref[...] = h2 * lax.rsqrt(ms + EPS) * gf_ref[...]


def _ffn(z, h, wu, wd, gf, *, tm, tf):
    m, d = h.shape
    dff = wu.shape[1]
    return pl.pallas_call(
        _ffn_body,
        grid=(m // tm, dff // tf),
        in_specs=[
            pl.BlockSpec((tm, d), lambda i, f: (i, 0)),
            pl.BlockSpec((tm, d), lambda i, f: (i, 0)),
            pl.BlockSpec((d, tf), lambda i, f: (0, f)),
            pl.BlockSpec((tf, d), lambda i, f: (f, 0)),
            pl.BlockSpec((1, d), lambda i, f: (0, 0)),
        ],
        out_specs=pl.BlockSpec((tm, d), lambda i, f: (i, 0)),
        out_shape=jax.ShapeDtypeStruct((m, d), F32),
        scratch_shapes=[pltpu.VMEM((tm, d), F32)],
        compiler_params=_params("parallel", "arbitrary"),
        name="ffn_final_norm",
    )(z, h, wu, wd, gf)


def _sample_scores_body(pt_ref, qi_ref, w_ref, newpage_ref, *rest, n_pages_step):
    page_refs = rest[:n_pages_step]
    o_ref, onew_ref = rest[n_pages_step:]
    qi = qi_ref[...].astype(BF16)
    wcol = w_ref[...] * (IDX_HEADS ** -0.5 * IDX_DIM ** -0.5)

    def score(pages_t):
        s = jnp.dot(qi, pages_t.astype(BF16), preferred_element_type=F32)
        return jnp.sum(jnp.maximum(s, 0.0) * wcol, axis=0, keepdims=True)

    o_ref[...] = score(jnp.concatenate([r[...] for r in page_refs], axis=1))
    onew_ref[...] = score(newpage_ref[...])


def _sample_scores(page_table, qi, w, newpage, kidx_t, *, pages_step):
    nb, n_pages = page_table.shape
    page = kidx_t.shape[3]
    steps = n_pages // pages_step

    def page_spec(j):
        return pl.BlockSpec((None, None, IDX_DIM, page),
                            lambda b, g, pt: (0, pt[b, g * pages_step + j], 0, 0))

    grid_spec = pltpu.PrefetchScalarGridSpec(
        num_scalar_prefetch=1,
        grid=(nb, steps),
        in_specs=[
            pl.BlockSpec((None, IDX_HEADS, IDX_DIM), lambda b, g, pt: (b, 0, 0)),
            pl.BlockSpec((None, IDX_HEADS, 1), lambda b, g, pt: (b, 0, 0)),
            pl.BlockSpec((None, IDX_DIM, page), lambda b, g, pt: (b, 0, 0)),
        ] + [page_spec(j) for j in range(pages_step)],
        out_specs=[
            pl.BlockSpec((None, 1, pages_step * page), lambda b, g, pt: (b, 0, g)),
            pl.BlockSpec((None, 1, page), lambda b, g, pt: (b, 0, 0)),
        ],
    )
    return pl.pallas_call(
        functools.partial(_sample_scores_body, n_pages_step=pages_step),
        grid_spec=grid_spec,
        out_shape=[jax.ShapeDtypeStruct((nb, 1, n_pages * page), F32),
                   jax.ShapeDtypeStruct((nb, 1, page), F32)],
        compiler_params=_params("parallel", "arbitrary"),
        name="sample_indexer_scores",
    )(page_table, qi, w, newpage, *([kidx_t] * pages_step))


def _prefix_exclusive(mask_f32, tri):
    rows, length = mask_f32.shape
    off = jnp.zeros((rows, 1), F32)
    parts = []
    for c in range(length // LANES):
        mc = mask_f32[:, c * LANES:(c + 1) * LANES]
        inc = jnp.dot(mc.astype(BF16), tri, preferred_element_type=F32)
        parts.append(inc - mc + off)
        off = off + inc[:, LANES - 1:LANES]
    return jnp.concatenate(parts, axis=1)


def _sample_select_body(sc_ref, scnew_ref, bias_ref, biasnew_ref, *, topk):
    past = sc_ref.shape[1]
    sc = jnp.concatenate([sc_ref[...], scnew_ref[...]], axis=1)
    nb, length = sc.shape
    pos = lax.broadcasted_iota(I32, (nb, length), 1)
    key = jnp.where(pos <= past, _sortable_key(sc), INT_MIN)

    def bit_iter(bi, tu):
        cand_u = tu | jnp.left_shift(jnp.int32(1), 31 - bi)
        cnt = jnp.sum((key >= (cand_u ^ INT_MIN)).astype(I32), axis=1, keepdims=True)
        return jnp.where(cnt >= topk, cand_u, tu)

    tu = lax.fori_loop(0, 32, bit_iter, jnp.zeros((nb, 1), I32))
    thr = tu ^ INT_MIN
    valid = key > INT_MIN
    gt = key > thr
    eq = key == thr
    tri = (lax.broadcasted_iota(I32, (LANES, LANES), 0)
           <= lax.broadcasted_iota(I32, (LANES, LANES), 1)).astype(BF16)
    need = (topk - jnp.sum(gt.astype(I32), axis=1, keepdims=True)).astype(F32)
    eq_rank = _prefix_exclusive(eq.astype(F32), tri)
    sel = (gt | (eq & (eq_rank < need))) & valid
    bias = jnp.where(sel, 0.0, -jnp.inf)
    bias_ref[...] = bias[:, :past]
    biasnew_ref[...] = bias[:, past:]


def _sample_select(scores, score_new, *, topk):
    nb, past = scores.shape
    return pl.pallas_call(
        functools.partial(_sample_select_body, topk=topk),
        out_shape=[jax.ShapeDtypeStruct((nb, past), F32),
                   jax.ShapeDtypeStruct(score_new.shape, F32)],
        compiler_params=pltpu.CompilerParams(vmem_limit_bytes=VMEM_LIMIT_BYTES),
        name="sample_topk_select",
    )(scores, score_new)


def _sample_attend_body(pt_ref, q_ref, kn_ref, vn_ref, bias_ref, biasnew_ref, *rest,
                        n_pages_step, page):
    k_refs = rest[:n_pages_step]
    v_refs = rest[n_pages_step:2 * n_pages_step]
    o_ref, m_ref, l_ref, acc_ref = rest[2 * n_pages_step:]
    g = pl.program_id(1)

    @pl.when(g == 0)
    def _():
        m_ref[...] = jnp.full(m_ref.shape, NEG_BIG, F32)
        l_ref[...] = jnp.zeros(l_ref.shape, F32)
        acc_ref[...] = jnp.zeros(acc_ref.shape, F32)

    q = q_ref[...] * (HEAD_DIM ** -0.5)

    def update(scores, values):
        m_old = m_ref[...]
        m_new = m_old
        for s in scores:
            m_new = jnp.maximum(m_new, s)
        alpha = jnp.exp(m_old - m_new)
        l = alpha * l_ref[...]
        acc = alpha * acc_ref[...]
        for s, v in zip(scores, values):
            p = jnp.exp(s - m_new)
            l = l + p
            acc = acc + p * v
        m_ref[...] = m_new
        l_ref[...] = l
        acc_ref[...] = acc

    scores = []
    for j, k_ref in enumerate(k_refs):
        s = jnp.sum(k_ref[...] * q, axis=1, keepdims=True)
        scores.append(s + bias_ref[:, j * page:(j + 1) * page][None])
    update(scores, [v_ref[...] for v_ref in v_refs])

    @pl.when(g == pl.num_programs(1) - 1)
    def _():
        s_new = jnp.sum(kn_ref[...] * q, axis=1, keepdims=True)
        s_new = jnp.broadcast_to(s_new, (N_HEADS, 1, page)) + biasnew_ref[...][None]
        update([s_new], [jnp.broadcast_to(vn_ref[...], (N_HEADS, HEAD_DIM, page))])
        m = m_ref[...]
        f = jnp.exp(m - jnp.max(m, axis=-1, keepdims=True))
        denom = jnp.sum(l_ref[...] * f, axis=-1, keepdims=True)
        o_ref[...] = jnp.sum(acc_ref[...] * f, axis=-1, keepdims=True) / denom


def _sample_attend(page_table, q, k_new, v_new, bias, bias_new, ck_t, cv_t, *, pages_step):
    nb, n_pages = page_table.shape
    page = ck_t.shape[4]
    steps = n_pages // pages_step
    head_block = pl.BlockSpec((None, N_HEADS, HEAD_DIM, 1), lambda b, g, pt: (b, 0, 0, 0))

    def page_spec(j):
        return pl.BlockSpec((None, None, N_HEADS, HEAD_DIM, page),
                            lambda b, g, pt: (0, pt[b, g * pages_step + j], 0, 0, 0))

    pages = [page_spec(j) for j in range(pages_step)]
    grid_spec = pltpu.PrefetchScalarGridSpec(
        num_scalar_prefetch=1,
        grid=(nb, steps),
        in_specs=[head_block, head_block, head_block,
                  pl.BlockSpec((None, 1, pages_step * page), lambda b, g, pt: (b, 0, g)),
                  pl.BlockSpec((None, 1, page), lambda b, g, pt: (b, 0, 0))] + pages + pages,
        out_specs=head_block,
        scratch_shapes=[
            pltpu.VMEM((N_HEADS, 1, page), F32),
            pltpu.VMEM((N_HEADS, 1, page), F32),
            pltpu.VMEM((N_HEADS, HEAD_DIM, page), F32),
        ],
    )
    return pl.pallas_call(
        functools.partial(_sample_attend_body, n_pages_step=pages_step, page=page),
        grid_spec=grid_spec,
        out_shape=jax.ShapeDtypeStruct((nb, N_HEADS, HEAD_DIM, 1), F32),
        compiler_params=_params("parallel", "arbitrary"),
        name="sample_paged_attend",
    )(page_table, q, k_new, v_new, bias, bias_new, *([ck_t] * pages_step), *([cv_t] * pages_step))


def _row_tile(m, pref):
    return pref if m % pref == 0 else m


def kernel(x_prompt, x_sample, cache_k, cache_v, cache_kidx, state_pool, page_table, norm_mix_g,
           w_in, b_gate, w_attn_out, w_pool_group, pool_scale, w_pool_out, w_out, norm_ffn_g,
           w_up, w_down, norm_final_g):
    depth = w_in.shape[0]
    assert depth == 1, "single-layer step"
    batch, seq, d_model = x_prompt.shape
    dec_batch, dec_seq, _ = x_sample.shape
    assert dec_seq == 1, "one new token per sample"
    n_pages = page_table.shape[1]
    page = cache_k.shape[2]
    past = n_pages * page
    pool_width = pool_scale.shape[1]
    assert page == LANES and n_pages % SCORE_PAGES == 0 and n_pages % ATTEND_PAGES == 0
    l = 0

    c_ki = 3 * ATTN_WIDTH + IDX_HEADS * IDX_DIM
    c_u = c_ki + IDX_DIM + IDX_HEADS
    c_ga = c_u + pool_width
    w_main = jnp.concatenate([w_in[l, :, :c_ki], w_in[l, :, c_ga:], w_in[l, :, c_u:c_ga]],
                             axis=1).astype(BF16)
    w_side = jnp.pad(w_in[l, :, c_ki:c_u], ((0, 0), (0, LANES - (c_u - c_ki)))).astype(BF16)
    tn = PROJ_COLS
    assert ATTN_WIDTH == tn and pool_width == tn and d_model == 2 * tn and IDX_HEADS * IDX_DIM == tn
    Q_COL, K_COL, V_COL, QI_COL, U_COL = 0, 1, 2, 3, 8
    GA_COL, GB_COL = 2, 3
    g_mix = norm_mix_g[l][None, :]
    wa = w_attn_out[l].astype(BF16)
    wg = w_pool_group[l].astype(BF16)
    wp = w_pool_out[l].astype(BF16)
    wo = w_out[l].astype(BF16)
    wu = w_up[l].astype(BF16)
    wd = w_down[l].astype(BF16)
    g_ffn = norm_ffn_g[l][None, :]
    g_fin = norm_final_g[None, :]
    scale = pool_scale[l][None, :]

    def col(p, c, width=tn):
        return p[:, c * tn:c * tn + width]

    xp = x_prompt.reshape(batch * seq, d_model)
    tm_p = _row_tile(batch * seq, PROJ_ROWS)
    pp = _rms_matmul(xp, g_mix, w_main, tm_p, tn)
    pp_side = _rms_matmul(xp, g_mix, w_side, tm_p, LANES)

    tq, kc = ATTN_QUERIES, ATTN_KEYS
    topk_p = min(TOPK_MAX, seq // 4)
    k_p = col(pp, K_COL)
    v_p = col(pp, V_COL)
    ki_p = pp_side[:, :IDX_DIM]
    ki_bf = ki_p.astype(BF16).reshape(batch, seq, IDX_DIM)
    k_hp = k_p.astype(BF16).reshape(batch, seq, N_HEADS // 2, 2 * HEAD_DIM).transpose(0, 2, 1, 3)
    vt = v_p.astype(BF16).reshape(batch, seq // kc, kc, N_HEADS, HEAD_DIM).transpose(0, 3, 1, 4, 2)
    attn_p = _prompt_attention(pp, pp_side, ki_bf, k_hp, vt, batch=batch, seq=seq, tq=tq, kc=kc,
                               topk=topk_p, q_col=Q_COL, qi_col=QI_COL)
    pool_p = _pool_prompt(pp, wg, scale, batch=batch, seq=seq, tm=POOL_ROWS, u_col=U_COL)
    h_p, z_p = _merge(xp, attn_p, pool_p, pp, b_gate[l], wa, wp, wo, g_ffn, tm=MERGE_ROWS,
                      ga_col=GA_COL, gb_col=GB_COL)
    y_p = _ffn(z_p, h_p, wu, wd, g_fin, tm=FFN_ROWS, tf=FFN_COLS)

    xs = x_sample.reshape(dec_batch, d_model)
    ps = _rms_matmul(xs, g_mix, w_main, dec_batch, tn)
    ps_side = _rms_matmul(xs, g_mix, w_side, dec_batch, LANES)
    topk_s = min(TOPK_MAX, (past + dec_seq) // 4)
    k_s = col(ps, K_COL)
    v_s = col(ps, V_COL)
    ki_s = ps_side[:, :IDX_DIM]
    wi_s = ps_side[:, IDX_DIM:IDX_DIM + IDX_HEADS]
    u_s = col(ps, U_COL)
    qi_s = col(ps, QI_COL).reshape(dec_batch, IDX_HEADS, IDX_DIM)
    kidx_t = jnp.swapaxes(cache_kidx, 2, 3)
    ck_t = jnp.transpose(cache_k, (0, 1, 3, 4, 2))
    cv_t = jnp.transpose(cache_v, (0, 1, 3, 4, 2))
    newpage = jnp.zeros((dec_batch, IDX_DIM, page), F32).at[:, :, 0].set(ki_s)
    scores, score_new = _sample_scores(page_table, qi_s, wi_s[:, :, None], newpage, kidx_t,
                                       pages_step=SCORE_PAGES)
    bias, bias_new = _sample_select(scores.reshape(dec_batch, past),
                                    score_new.reshape(dec_batch, page), topk=topk_s)
    head4 = (dec_batch, N_HEADS, HEAD_DIM, 1)
    attn_s = _sample_attend(page_table, col(ps, Q_COL).reshape(head4), k_s.reshape(head4),
                            v_s.reshape(head4), bias[:, None, :], bias_new[:, None, :],
                            ck_t, cv_t, pages_step=ATTEND_PAGES)
    pool_s = _pool_sample(state_pool[l], u_s, wg, scale)
    h_s, z_s = _merge(xs, attn_s.reshape(dec_batch, ATTN_WIDTH).astype(BF16), pool_s, ps,
                      b_gate[l], wa, wp, wo, g_ffn, tm=dec_batch, ga_col=GA_COL, gb_col=GB_COL)
    y_s = _ffn(z_s, h_s, wu, wd, g_fin, tm=dec_batch, tf=FFN_COLS)

    n_state = state_pool.shape[2]
    u_p = col(pp, U_COL).reshape(batch, seq, pool_width)
    pool_state_s = jnp.concatenate([state_pool[l].astype(F32), u_s[:, None, :]], axis=1)[:, -n_state:]
    return (
        y_p.reshape(batch, seq, d_model),
        y_s.reshape(dec_batch, dec_seq, d_model),
        k_p.reshape(1, batch, seq, N_HEADS, HEAD_DIM),
        v_p.reshape(1, batch, seq, N_HEADS, HEAD_DIM),
        ki_p.reshape(1, batch, seq, IDX_DIM),
        u_p[:, -n_state:][None],
        k_s.reshape(1, dec_batch, dec_seq, N_HEADS, HEAD_DIM),
        v_s.reshape(1, dec_batch, dec_seq, N_HEADS, HEAD_DIM),
        ki_s.reshape(1, dec_batch, dec_seq, IDX_DIM),
        pool_state_s[None],
    )
```

```python
import functools

import jax
import jax.numpy as jnp
from jax import lax
from jax.experimental import pallas as pl
from jax.experimental.pallas import tpu as pltpu

F32 = jnp.float32
BF16 = jnp.bfloat16
I32 = jnp.int32

N_HEADS = 16
HEAD_DIM = 64
ATTN_WIDTH = N_HEADS * HEAD_DIM
IDX_HEADS = 16
IDX_DIM = 64
TOPK_MAX = 256
POOL_WINDOWS = (2, 4, 8, 16)
POOL_GROUP_WIDTH = 256
POOL_HALO = 16
EPS = 1e-6
INT_MIN = -(2 ** 31)
LANES = 128
NEG_BIG = -1e30
Q_SCALE = HEAD_DIM ** -0.5 * 1.4426950408889634
V_ROWS = HEAD_DIM + 16

VMEM_LIMIT_BYTES = 56 * 1024 * 1024

PROJ_ROWS = 512
PROJ_COLS = 1024
ATTN_QUERIES = 256
ATTN_KEYS = 128
POOL_ROWS = 256
MERGE_ROWS = 256
FFN_ROWS = 512
FFN_COLS = 512
SCORE_PAGES = 32
ATTEND_PAGES = 8


def _params(*sem):
    return pltpu.CompilerParams(dimension_semantics=sem, vmem_limit_bytes=VMEM_LIMIT_BYTES)


def _sortable_key(x):
    b = lax.bitcast_convert_type(x, I32)
    return jnp.where(b >= 0, b, b ^ jnp.int32(0x7FFFFFFF))


_CONTRACT_LAST = (((1,), (1,)), ((), ()))


def _rms_to_bf16(x_ref, g_ref, z_ref):
    x = x_ref[...]
    ms = jnp.mean(x * x, axis=-1, keepdims=True)
    z_ref[...] = (x * lax.rsqrt(ms + EPS) * g_ref[...]).astype(BF16)


def _proj_nat_body(x_ref, g_ref, wt_ref, o_ref, z_ref):
    @pl.when(pl.program_id(1) == 0)
    def _():
        _rms_to_bf16(x_ref, g_ref, z_ref)

    o_ref[...] = lax.dot_general(z_ref[...], wt_ref[...], _CONTRACT_LAST,
                                 preferred_element_type=F32)


def _proj_nat(x, g, wt, tm, tn):
    m, d = x.shape
    n = wt.shape[0]
    return pl.pallas_call(
        _proj_nat_body,
        grid=(m // tm, n // tn),
        in_specs=[
            pl.BlockSpec((tm, d), lambda i, j: (i, 0)),
            pl.BlockSpec((1, d), lambda i, j: (0, 0)),
            pl.BlockSpec((tn, d), lambda i, j: (j, 0)),
        ],
        out_specs=pl.BlockSpec((tm, tn), lambda i, j: (i, j)),
        out_shape=jax.ShapeDtypeStruct((m, n), F32),
        scratch_shapes=[pltpu.VMEM((tm, d), BF16)],
        compiler_params=_params("parallel", "arbitrary"),
        name="rms_proj_rows",
    )(x, g, wt)


def _proj_qqi_body(x_ref, g_ref, wt_ref, o_ref, z_ref):
    @pl.when(pl.program_id(1) == 0)
    def _():
        _rms_to_bf16(x_ref, g_ref, z_ref)

    o_ref[...] = lax.dot_general(wt_ref[...], z_ref[...], _CONTRACT_LAST,
                                 preferred_element_type=F32).astype(BF16)


def _proj_qqi(x, g, wt_head, *, batch, seq, tm, row_blocks):
    d = x.shape[1]
    nblk = seq // tm
    blocks = tuple(row_blocks)
    assert blocks == (0, 3)
    return pl.pallas_call(
        _proj_qqi_body,
        grid=(batch * nblk, len(blocks)),
        in_specs=[
            pl.BlockSpec((tm, d), lambda i, j: (i, 0)),
            pl.BlockSpec((1, d), lambda i, j: (0, 0)),
            pl.BlockSpec((ATTN_WIDTH, d), lambda i, j: (3 * j, 0)),
        ],
        out_specs=pl.BlockSpec((None, None, ATTN_WIDTH, tm),
                               lambda i, j: (j, i // nblk, 0, i % nblk)),
        out_shape=jax.ShapeDtypeStruct((len(blocks), batch, ATTN_WIDTH, seq), BF16),
        scratch_shapes=[pltpu.VMEM((tm, d), BF16)],
        compiler_params=_params("parallel", "arbitrary"),
        name="rms_proj_q_qi",
    )(x, g, wt_head)


def _proj_kv_body(x_ref, g_ref, wt_ref, ws_ref, kvt_ref, knat_ref, vtb_ref, sidet_ref, kinat_ref,
                  z_ref):
    j = pl.program_id(1)

    @pl.when(j == 0)
    def _():
        _rms_to_bf16(x_ref, g_ref, z_ref)
        side = lax.dot_general(ws_ref[...], z_ref[...], _CONTRACT_LAST,
                               preferred_element_type=F32)
        sidet_ref[...] = side
        kinat_ref[...] = side.T.astype(BF16)

    o = lax.dot_general(wt_ref[...], z_ref[...], _CONTRACT_LAST, preferred_element_type=F32)
    kvt_ref[...] = o

    @pl.when(j == 0)
    def _():
        knat_ref[...] = o.T.astype(BF16)

    @pl.when(j == 1)
    def _():
        tm = o.shape[1]
        vtb_ref[:, 0:HEAD_DIM, :] = o.reshape(N_HEADS, HEAD_DIM, tm).astype(BF16)
        row = lax.broadcasted_iota(I32, (N_HEADS, V_ROWS - HEAD_DIM, tm), 1)
        vtb_ref[:, HEAD_DIM:V_ROWS, :] = jnp.where(row == 0, 1.0, 0.0).astype(BF16)


def _proj_kv(x, g, wt_head, *, batch, seq, tm, side_block):
    d = x.shape[1]
    nblk = seq // tm
    tok = batch * seq
    bi = lambda i, j: (i // nblk, 0, i % nblk)
    return pl.pallas_call(
        _proj_kv_body,
        grid=(batch * nblk, 2),
        in_specs=[
            pl.BlockSpec((tm, d), lambda i, j: (i, 0)),
            pl.BlockSpec((1, d), lambda i, j: (0, 0)),
            pl.BlockSpec((ATTN_WIDTH, d), lambda i, j: (1 + j, 0)),
            pl.BlockSpec((LANES, d), lambda i, j: (side_block, 0)),
        ],
        out_specs=[
            pl.BlockSpec((None, None, ATTN_WIDTH, tm), lambda i, j: (j, i // nblk, 0, i % nblk)),
            pl.BlockSpec((tm, ATTN_WIDTH), lambda i, j: (i, 0)),
            pl.BlockSpec((None, N_HEADS, V_ROWS, tm), lambda i, j: (i // nblk, 0, 0, i % nblk)),
            pl.BlockSpec((None, LANES, tm), bi),
            pl.BlockSpec((tm, LANES), lambda i, j: (i, 0)),
        ],
        out_shape=[
            jax.ShapeDtypeStruct((2, batch, ATTN_WIDTH, seq), F32),
            jax.ShapeDtypeStruct((tok, ATTN_WIDTH), BF16),
            jax.ShapeDtypeStruct((batch, N_HEADS, V_ROWS, seq), BF16),
            jax.ShapeDtypeStruct((batch, LANES, seq), F32),
            jax.ShapeDtypeStruct((tok, LANES), BF16),
        ],
        scratch_shapes=[pltpu.VMEM((tm, d), BF16)],
        compiler_params=_params("parallel", "arbitrary"),
        name="rms_proj_k_v",
    )(x, g, wt_head, wt_head)


def _prompt_attn_body(qit_ref, side_ref, qt_ref, ki_ref, k_ref, vt_ref, o_ref,
                      keys_ref, qtp_ref, ot_ref, m_ref, l_ref, a_ref, s_ref, p_ref,
                      *, tq, kc, topk):
    i = pl.program_id(1)
    nchunk = (i + 1) * (tq // kc)
    q0 = i * tq

    w16 = side_ref[IDX_DIM:IDX_DIM + IDX_HEADS, :] * (IDX_HEADS ** -0.5 * IDX_DIM ** -0.5)
    qt = qt_ref[...].astype(F32).reshape(N_HEADS // 2, 2 * HEAD_DIM, tq)
    first_half = lax.broadcasted_iota(I32, (2 * HEAD_DIM, tq), 0) < HEAD_DIM
    for j in range(N_HEADS // 2):
        qtp_ref[2 * j] = jnp.where(first_half, qt[j], 0.0).astype(BF16)
        qtp_ref[2 * j + 1] = jnp.where(first_half, 0.0, qt[j]).astype(BF16)

    def idx_chunk(c, carry):
        r0 = pl.multiple_of(c * kc, kc)
        kic = ki_ref[pl.ds(r0, kc), 0:IDX_DIM]
        acc = jnp.zeros((kc, tq), F32)
        for h in range(IDX_HEADS):
            s = jnp.dot(kic, qit_ref[h * IDX_DIM:(h + 1) * IDX_DIM, :],
                        preferred_element_type=F32)
            acc = acc + jnp.maximum(s, 0.0) * w16[h:h + 1, :]
        key = _sortable_key(acc)
        krow = r0 + lax.broadcasted_iota(I32, (kc, tq), 0)
        qcol = q0 + lax.broadcasted_iota(I32, (kc, tq), 1)
        keys_ref[pl.ds(r0, kc), :] = jnp.where(krow <= qcol, key, INT_MIN)
        return carry

    lax.fori_loop(0, nchunk, idx_chunk, 0)

    def count_ge(cand):
        def body(c, cnt):
            r0 = pl.multiple_of(c * tq, tq)
            ge = (keys_ref[pl.ds(r0, tq), :] >= cand).astype(I32)
            return cnt + jnp.sum(ge.reshape(tq // 8, 8, tq), axis=0)

        cnt8 = lax.fori_loop(0, i + 1, body, jnp.zeros((8, tq), I32))
        return jnp.sum(cnt8, axis=0, keepdims=True)

    def bit_iter(bi, tu):
        cand_u = tu | jnp.left_shift(jnp.int32(1), 31 - bi)
        cnt = count_ge(cand_u ^ INT_MIN)
        return jnp.where(cnt >= topk, cand_u, tu)

    tu = lax.fori_loop(0, 32, bit_iter, jnp.zeros((1, tq), I32))
    thr = tu ^ INT_MIN
    tsel = jnp.maximum(thr, INT_MIN + 1)

    n_ge = count_ge(tsel)
    has_tie = jnp.max((n_ge > topk).astype(I32)) > 0

    def count_eq_below(bound):
        def body(c, cnt):
            r0 = pl.multiple_of(c * kc, kc)
            krow = r0 + lax.broadcasted_iota(I32, (kc, tq), 0)
            hit = ((keys_ref[pl.ds(r0, kc), :] == tsel) & (krow < bound)).astype(I32)
            return cnt + jnp.sum(hit.reshape(kc // 8, 8, tq), axis=0)

        cnt8 = lax.fori_loop(0, nchunk, body, jnp.zeros((8, tq), I32))
        return jnp.sum(cnt8, axis=0, keepdims=True)

    @pl.when(has_tie)
    def _():
        need = topk - (n_ge - count_eq_below(jnp.int32(2 ** 30)))

        def jbit(bi, jlo):
            cand = jlo | jnp.left_shift(jnp.int32(1), 29 - bi)
            return jnp.where(count_eq_below(cand) < need, cand, jlo)

        jstar = lax.fori_loop(0, 30, jbit, jnp.zeros((1, tq), I32))

        def drop(c, carry):
            r0 = pl.multiple_of(c * kc, kc)
            krow = r0 + lax.broadcasted_iota(I32, (kc, tq), 0)
            kk = keys_ref[pl.ds(r0, kc), :]
            keys_ref[pl.ds(r0, kc), :] = jnp.where((kk == tsel) & (krow > jstar), INT_MIN, kk)
            return carry

        lax.fori_loop(0, nchunk, drop, 0)

    m_ref[...] = jnp.full(m_ref.shape, NEG_BIG, F32)
    l_ref[...] = jnp.zeros(l_ref.shape, F32)
    ot_ref[...] = jnp.zeros(ot_ref.shape, F32)

    def chunk_body(c, carry):
        r0 = pl.multiple_of(c * kc, kc)
        bias = jnp.where(keys_ref[pl.ds(r0, kc), :] >= tsel, 0.0, -jnp.inf)
        for h in range(N_HEADS):
            pair = slice((h // 2) * 2 * HEAD_DIM, (h // 2 + 1) * 2 * HEAD_DIM)
            s_ref[h] = jnp.dot(k_ref[pl.ds(r0, kc), pair], qtp_ref[h],
                               preferred_element_type=F32) + bias
        for h in range(N_HEADS):
            s = s_ref[h]
            m_old = m_ref[h]
            m_new = jnp.maximum(m_old, jnp.max(s, axis=0, keepdims=True))
            p_ref[h] = jnp.exp2(s - m_new).astype(BF16)
            a_ref[h] = jnp.exp2(m_old - m_new)
            m_ref[h] = m_new
        for h in range(N_HEADS):
            pv = jnp.dot(vt_ref[h, :, pl.ds(r0, kc)], p_ref[h], preferred_element_type=F32)
            ot_ref[h] = a_ref[h] * ot_ref[h] + pv[0:HEAD_DIM]
            l_ref[h] = a_ref[h] * l_ref[h] + pv[HEAD_DIM:HEAD_DIM + 1]
        return carry

    lax.fori_loop(0, nchunk, chunk_body, 0)
    o = ot_ref[...] / l_ref[...]
    o_ref[...] = o.reshape(ATTN_WIDTH, tq).T.astype(BF16)


def _prompt_attention(qqi_t, side_t, ki_nat, k_nat, vt, *, batch, seq, tq, kc, topk):
    nq = seq // tq
    body = functools.partial(_prompt_attn_body, tq=tq, kc=kc, topk=topk)
    return pl.pallas_call(
        body,
        grid=(batch, nq),
        in_specs=[
            pl.BlockSpec((None, None, ATTN_WIDTH, tq), lambda b, i: (1, b, 0, i)),
            pl.BlockSpec((None, LANES, tq), lambda b, i: (b, 0, i)),
            pl.BlockSpec((None, None, ATTN_WIDTH, tq), lambda b, i: (0, b, 0, i)),
            pl.BlockSpec((seq, LANES), lambda b, i: (b, 0)),
            pl.BlockSpec((seq, ATTN_WIDTH), lambda b, i: (b, 0)),
            pl.BlockSpec((None, N_HEADS, V_ROWS, seq), lambda b, i: (b, 0, 0, 0)),
        ],
        out_specs=pl.BlockSpec((tq, ATTN_WIDTH), lambda b, i: (b * nq + i, 0)),
        out_shape=jax.ShapeDtypeStruct((batch * seq, ATTN_WIDTH), BF16),
        scratch_shapes=[
            pltpu.VMEM((seq, tq), I32),
            pltpu.VMEM((N_HEADS, 2 * HEAD_DIM, tq), BF16),
            pltpu.VMEM((N_HEADS, HEAD_DIM, tq), F32),
            pltpu.VMEM((N_HEADS, 1, tq), F32),
            pltpu.VMEM((N_HEADS, 1, tq), F32),
            pltpu.VMEM((N_HEADS, 1, tq), F32),
            pltpu.VMEM((N_HEADS, kc, tq), F32),
            pltpu.VMEM((N_HEADS, kc, tq), BF16),
        ],
        compiler_params=_params("parallel", "arbitrary"),
        name="prompt_sparse_attn",
    )(qqi_t, side_t, qqi_t, ki_nat, k_nat, vt)


def _pool_prompt_body(u_ref, wg_ref, sc_ref, o_ref, ue_ref, *, tm):
    i = pl.program_id(1)

    @pl.when(i == 0)
    def _():
        ue_ref[0:POOL_HALO, :] = jnp.zeros((POOL_HALO, ue_ref.shape[1]), F32)

    u = u_ref[...]
    ue_ref[POOL_HALO:POOL_HALO + tm, :] = u
    pos = i * tm + lax.broadcasted_iota(I32, (tm, 1), 0)
    outs = []
    for g, w in enumerate(POOL_WINDOWS):
        cols = slice(g * POOL_GROUP_WIDTH, (g + 1) * POOL_GROUP_WIDTH)
        win = u[:, cols]
        for back in range(1, w):
            win = win + ue_ref[POOL_HALO - back:POOL_HALO - back + tm, cols]
        cnt = jnp.minimum(pos + 1, w).astype(F32)
        mixed = win / cnt - u[:, cols]
        outs.append(jnp.dot(mixed.astype(BF16), wg_ref[g], preferred_element_type=F32))
    o_ref[...] = (jnp.concatenate(outs, axis=1) * sc_ref[...]).astype(BF16)
    ue_ref[0:POOL_HALO, :] = u[tm - POOL_HALO:, :]


def _pool_prompt(p_main, wg, scale, *, batch, seq, tm, u_col):
    nblk = seq // tm
    width = wg.shape[0] * wg.shape[1]
    return pl.pallas_call(
        functools.partial(_pool_prompt_body, tm=tm),
        grid=(batch, nblk),
        in_specs=[
            pl.BlockSpec((tm, width), lambda b, i: (b * nblk + i, u_col)),
            pl.BlockSpec(wg.shape, lambda b, i: (0, 0, 0)),
            pl.BlockSpec((1, width), lambda b, i: (0, 0)),
        ],
        out_specs=pl.BlockSpec((tm, width), lambda b, i: (b * nblk + i, 0)),
        out_shape=jax.ShapeDtypeStruct((batch * seq, width), BF16),
        scratch_shapes=[pltpu.VMEM((POOL_HALO + tm, width), F32)],
        compiler_params=_params("parallel", "arbitrary"),
        name="pool_prompt",
    )(p_main, wg, scale)


def _pool_sample_body(st_ref, u_ref, wg_ref, sc_ref, o_ref):
    st = st_ref[...]
    u = u_ref[...]
    n_state = st.shape[1]
    outs = []
    for g, w in enumerate(POOL_WINDOWS):
        cols = slice(g * POOL_GROUP_WIDTH, (g + 1) * POOL_GROUP_WIDTH)
        win = u[:, cols] + jnp.sum(st[:, n_state - (w - 1):, cols], axis=1)
        mixed = win / float(w) - u[:, cols]
        outs.append(jnp.dot(mixed.astype(BF16), wg_ref[g], preferred_element_type=F32))
    o_ref[...] = (jnp.concatenate(outs, axis=1) * sc_ref[...]).astype(BF16)


def _pool_sample(state, u_new, wg, scale):
    nb, width = u_new.shape
    return pl.pallas_call(
        _pool_sample_body,
        out_shape=jax.ShapeDtypeStruct((nb, width), BF16),
        compiler_params=pltpu.CompilerParams(vmem_limit_bytes=VMEM_LIMIT_BYTES),
        name="pool_sample",
    )(state, u_new, wg, scale)


def _merge_body(x_ref, ao_ref, po_ref, ga_ref, gb_ref, bg_ref, wa_ref, wp_ref, wo_ref, g2_ref,
                h_ref, z_ref):
    ya = jnp.dot(ao_ref[...], wa_ref[...], preferred_element_type=F32)
    yb = jnp.dot(po_ref[...], wp_ref[...], preferred_element_type=F32)
    merged = (jax.nn.sigmoid(ga_ref[...] + bg_ref[0:1, :]) * ya
              + jax.nn.sigmoid(gb_ref[...] + bg_ref[1:2, :]) * yb)
    h = x_ref[...] + jnp.dot(merged.astype(BF16), wo_ref[...], preferred_element_type=F32)
    h_ref[...] = h
    ms = jnp.mean(h * h, axis=-1, keepdims=True)
    z_ref[...] = (h * lax.rsqrt(ms + EPS) * g2_ref[...]).astype(BF16)


def _merge(x, attn_o, pool_o, p_main, b_gate, wa, wp, wo, g2, *, tm, ga_col, gb_col):
    m, d = x.shape
    const = lambda i: (0, 0)
    return pl.pallas_call(
        _merge_body,
        grid=(m // tm,),
        in_specs=[
            pl.BlockSpec((tm, d), lambda i: (i, 0)),
            pl.BlockSpec((tm, attn_o.shape[1]), lambda i: (i, 0)),
            pl.BlockSpec((tm, pool_o.shape[1]), lambda i: (i, 0)),
            pl.BlockSpec((tm, d), lambda i: (i, ga_col)),
            pl.BlockSpec((tm, d), lambda i: (i, gb_col)),
            pl.BlockSpec(b_gate.shape, const),
            pl.BlockSpec(wa.shape, const),
            pl.BlockSpec(wp.shape, const),
            pl.BlockSpec(wo.shape, const),
            pl.BlockSpec((1, d), const),
        ],
        out_specs=[pl.BlockSpec((tm, d), lambda i: (i, 0)), pl.BlockSpec((tm, d), lambda i: (i, 0))],
        out_shape=[jax.ShapeDtypeStruct((m, d), F32), jax.ShapeDtypeStruct((m, d), BF16)],
        compiler_params=_params("parallel"),
        name="gated_merge",
    )(x, attn_o, pool_o, p_main, p_main, b_gate, wa, wp, wo, g2)


def _ffn_body(z_ref, h_ref, wu_ref, wd_ref, gf_ref, y_ref, acc_ref):
    f = pl.program_id(1)

    @pl.when(f == 0)
    def _():
        acc_ref[...] = h_ref[...]

    a = jnp.dot(z_ref[...], wu_ref[...], preferred_element_type=F32)
    a = jnp.square(jnp.maximum(a, 0.0))
    acc_ref[...] += jnp.dot(a.astype(BF16), wd_ref[...], preferred_element_type=F32)

    @pl.when(f == pl.num_programs(1) - 1)
    def _():
        h2 = acc_ref[...]
        ms = jnp.mean(h2 * h2, axis=-1, keepdims=True)
        y_ref[...] = h2 * lax.rsqrt(ms + EPS) * gf_ref[...]


def _ffn(z, h, wu, wd, gf, *, tm, tf):
    m, d = h.shape
    dff = wu.shape[1]
    return pl.pallas_call(
        _ffn_body,
        grid=(m // tm, dff // tf),
        in_specs=[
            pl.BlockSpec((tm, d), lambda i, f: (i, 0)),
            pl.BlockSpec((tm, d), lambda i, f: (i, 0)),
            pl.BlockSpec((d, tf), lambda i, f: (0, f)),
            pl.BlockSpec((tf, d), lambda i, f: (f, 0)),
            pl.BlockSpec((1, d), lambda i, f: (0, 0)),
        ],
        out_specs=pl.BlockSpec((tm, d), lambda i, f: (i, 0)),
        out_shape=jax.ShapeDtypeStruct((m, d), F32),
        scratch_shapes=[pltpu.VMEM((tm, d), F32)],
        compiler_params=_params("parallel", "arbitrary"),
        name="ffn_final_norm",
    )(z, h, wu, wd, gf)


def _sample_scores_body(pt_ref, qi_ref, w_ref, newpage_ref, *rest, n_pages_step):
    page_refs = rest[:n_pages_step]
    o_ref, onew_ref = rest[n_pages_step:]
    qi = qi_ref[...].astype(BF16)
    wcol = w_ref[...] * (IDX_HEADS ** -0.5 * IDX_DIM ** -0.5)

    def score(pages_t):
        s = jnp.dot(qi, pages_t.astype(BF16), preferred_element_type=F32)
        return jnp.sum(jnp.maximum(s, 0.0) * wcol, axis=0, keepdims=True)

    o_ref[...] = score(jnp.concatenate([r[...] for r in page_refs], axis=1))
    onew_ref[...] = score(newpage_ref[...])


def _sample_scores(page_table, qi, w, newpage, kidx_t, *, pages_step):
    nb, n_pages = page_table.shape
    page = kidx_t.shape[3]
    steps = n_pages // pages_step

    def page_spec(j):
        return pl.BlockSpec((None, None, IDX_DIM, page),
                            lambda b, g, pt: (0, pt[b, g * pages_step + j], 0, 0))

    grid_spec = pltpu.PrefetchScalarGridSpec(
        num_scalar_prefetch=1,
        grid=(nb, steps),
        in_specs=[
            pl.BlockSpec((None, IDX_HEADS, IDX_DIM), lambda b, g, pt: (b, 0, 0)),
            pl.BlockSpec((None, IDX_HEADS, 1), lambda b, g, pt: (b, 0, 0)),
            pl.BlockSpec((None, IDX_DIM, page), lambda b, g, pt: (b, 0, 0)),
        ] + [page_spec(j) for j in range(pages_step)],
        out_specs=[
            pl.BlockSpec((None, 1, pages_step * page), lambda b, g, pt: (b, 0, g)),
            pl.BlockSpec((None, 1, page), lambda b, g, pt: (b, 0, 0)),
        ],
    )
    return pl.pallas_call(
        functools.partial(_sample_scores_body, n_pages_step=pages_step),
        grid_spec=grid_spec,
        out_shape=[jax.ShapeDtypeStruct((nb, 1, n_pages * page), F32),
                   jax.ShapeDtypeStruct((nb, 1, page), F32)],
        compiler_params=_params("parallel", "arbitrary"),
        name="sample_indexer_scores",
    )(page_table, qi, w, newpage, *([kidx_t] * pages_step))


def _prefix_exclusive(mask_f32, tri):
    rows, length = mask_f32.shape
    off = jnp.zeros((rows, 1), F32)
    parts = []
    for c in range(length // LANES):
        mc = mask_f32[:, c * LANES:(c + 1) * LANES]
        inc = jnp.dot(mc.astype(BF16), tri, preferred_element_type=F32)
        parts.append(inc - mc + off)
        off = off + inc[:, LANES - 1:LANES]
    return jnp.concatenate(parts, axis=1)


def _sample_select_body(sc_ref, scnew_ref, bias_ref, biasnew_ref, *, topk):
    past = sc_ref.shape[1]
    sc = jnp.concatenate([sc_ref[...], scnew_ref[...]], axis=1)
    nb, length = sc.shape
    pos = lax.broadcasted_iota(I32, (nb, length), 1)
    key = jnp.where(pos <= past, _sortable_key(sc), INT_MIN)

    def bit_iter(bi, tu):
        cand_u = tu | jnp.left_shift(jnp.int32(1), 31 - bi)
        cnt = jnp.sum((key >= (cand_u ^ INT_MIN)).astype(I32), axis=1, keepdims=True)
        return jnp.where(cnt >= topk, cand_u, tu)

    tu = lax.fori_loop(0, 32, bit_iter, jnp.zeros((nb, 1), I32))
    thr = tu ^ INT_MIN
    valid = key > INT_MIN
    gt = key > thr
    eq = key == thr
    tri = (lax.broadcasted_iota(I32, (LANES, LANES), 0)
           <= lax.broadcasted_iota(I32, (LANES, LANES), 1)).astype(BF16)
    need = (topk - jnp.sum(gt.astype(I32), axis=1, keepdims=True)).astype(F32)
    eq_rank = _prefix_exclusive(eq.astype(F32), tri)
    sel = (gt | (eq & (eq_rank < need))) & valid
    bias = jnp.where(sel, 0.0, -jnp.inf)
    bias_ref[...] = bias[:, :past]
    biasnew_ref[...] = bias[:, past:]


def _sample_select(scores, score_new, *, topk):
    nb, past = scores.shape
    return pl.pallas_call(
        functools.partial(_sample_select_body, topk=topk),
        out_shape=[jax.ShapeDtypeStruct((nb, past), F32),
                   jax.ShapeDtypeStruct(score_new.shape, F32)],
        compiler_params=pltpu.CompilerParams(vmem_limit_bytes=VMEM_LIMIT_BYTES),
        name="sample_topk_select",
    )(scores, score_new)


def _sample_attend_body(pt_ref, q_ref, kn_ref, vn_ref, bias_ref, biasnew_ref, *rest,
                        n_pages_step, page):
    k_refs = rest[:n_pages_step]
    v_refs = rest[n_pages_step:2 * n_pages_step]
    o_ref, m_ref, l_ref, acc_ref = rest[2 * n_pages_step:]
    g = pl.program_id(1)

    @pl.when(g == 0)
    def _():
        m_ref[...] = jnp.full(m_ref.shape, NEG_BIG, F32)
        l_ref[...] = jnp.zeros(l_ref.shape, F32)
        acc_ref[...] = jnp.zeros(acc_ref.shape, F32)

    q = q_ref[...]

    def update(scores, values):
        m_old = m_ref[...]
        m_new = m_old
        for s in scores:
            m_new = jnp.maximum(m_new, s)
        alpha = jnp.exp2(m_old - m_new)
        l = alpha * l_ref[...]
        acc = alpha * acc_ref[...]
        for s, v in zip(scores, values):
            p = jnp.exp2(s - m_new)
            l = l + p
            acc = acc + p * v
        m_ref[...] = m_new
        l_ref[...] = l
        acc_ref[...] = acc

    scores = []
    for j, k_ref in enumerate(k_refs):
        s = jnp.sum(k_ref[...] * q, axis=1, keepdims=True)
        scores.append(s + bias_ref[:, j * page:(j + 1) * page][None])
    update(scores, [v_ref[...] for v_ref in v_refs])

    @pl.when(g == pl.num_programs(1) - 1)
    def _():
        s_new = jnp.sum(kn_ref[...] * q, axis=1, keepdims=True)
        s_new = jnp.broadcast_to(s_new, (N_HEADS, 1, page)) + biasnew_ref[...][None]
        update([s_new], [jnp.broadcast_to(vn_ref[...], (N_HEADS, HEAD_DIM, page))])
        m = m_ref[...]
        f = jnp.exp2(m - jnp.max(m, axis=-1, keepdims=True))
        denom = jnp.sum(l_ref[...] * f, axis=-1, keepdims=True)
        o_ref[...] = jnp.sum(acc_ref[...] * f, axis=-1, keepdims=True) / denom


def _sample_attend(page_table, q, k_new, v_new, bias, bias_new, ck_t, cv_t, *, pages_step):
    nb, n_pages = page_table.shape
    page = ck_t.shape[4]
    steps = n_pages // pages_step
    head_block = pl.BlockSpec((None, N_HEADS, HEAD_DIM, 1), lambda b, g, pt: (b, 0, 0, 0))

    def page_spec(j):
        return pl.BlockSpec((None, None, N_HEADS, HEAD_DIM, page),
                            lambda b, g, pt: (0, pt[b, g * pages_step + j], 0, 0, 0))

    pages = [page_spec(j) for j in range(pages_step)]
    grid_spec = pltpu.PrefetchScalarGridSpec(
        num_scalar_prefetch=1,
        grid=(nb, steps),
        in_specs=[head_block, head_block, head_block,
                  pl.BlockSpec((None, 1, pages_step * page), lambda b, g, pt: (b, 0, g)),
                  pl.BlockSpec((None, 1, page), lambda b, g, pt: (b, 0, 0))] + pages + pages,
        out_specs=head_block,
        scratch_shapes=[
            pltpu.VMEM((N_HEADS, 1, page), F32),
            pltpu.VMEM((N_HEADS, 1, page), F32),
            pltpu.VMEM((N_HEADS, HEAD_DIM, page), F32),
        ],
    )
    return pl.pallas_call(
        functools.partial(_sample_attend_body, n_pages_step=pages_step, page=page),
        grid_spec=grid_spec,
        out_shape=jax.ShapeDtypeStruct((nb, N_HEADS, HEAD_DIM, 1), F32),
        compiler_params=_params("parallel", "arbitrary"),
        name="sample_paged_attend",
    )(page_table, q, k_new, v_new, bias, bias_new, *([ck_t] * pages_step), *([cv_t] * pages_step))


def kernel(x_prompt, x_sample, cache_k, cache_v, cache_kidx, state_pool, page_table, norm_mix_g,
           w_in, b_gate, w_attn_out, w_pool_group, pool_scale, w_pool_out, w_out, norm_ffn_g,
           w_up, w_down, norm_final_g):
    depth = w_in.shape[0]
    assert depth == 1, "single-layer step"
    batch, seq, d_model = x_prompt.shape
    dec_batch, dec_seq, _ = x_sample.shape
    assert dec_seq == 1, "one new token per sample"
    n_pages = page_table.shape[1]
    page = cache_k.shape[2]
    past = n_pages * page
    pool_width = pool_scale.shape[1]
    score_pages = min(SCORE_PAGES, n_pages)
    attend_pages = min(ATTEND_PAGES, n_pages)
    assert page == LANES and n_pages % score_pages == 0 and n_pages % attend_pages == 0
    l = 0

    tn = PROJ_COLS
    assert ATTN_WIDTH == tn and pool_width == tn and d_model == 2 * tn and IDX_HEADS * IDX_DIM == tn
    c_ki = 3 * ATTN_WIDTH + IDX_HEADS * IDX_DIM
    c_u = c_ki + IDX_DIM + IDX_HEADS
    c_ga = c_u + pool_width
    n_head_rows = c_ki + LANES
    wt = jnp.swapaxes(w_in[l], 0, 1)
    wt_head = jnp.concatenate([wt[:ATTN_WIDTH] * Q_SCALE, wt[ATTN_WIDTH:n_head_rows]],
                              axis=0).astype(BF16)
    wt_tail = jnp.concatenate([wt[c_ga:], wt[c_u:c_ga]], axis=0).astype(BF16)
    K_COL, V_COL, QI_COL = 1, 2, 3
    SIDE_BLOCK = c_ki // LANES
    GA_COL, GB_COL, U_COL = 0, 1, 4
    g_mix = norm_mix_g[l][None, :]
    wa = w_attn_out[l].astype(BF16)
    wg = w_pool_group[l].astype(BF16)
    wp = w_pool_out[l].astype(BF16)
    wo = w_out[l].astype(BF16)
    wu = w_up[l].astype(BF16)
    wd = w_down[l].astype(BF16)
    g_ffn = norm_ffn_g[l][None, :]
    g_fin = norm_final_g[None, :]
    scale = pool_scale[l][None, :]

    def col(p, c, width=tn):
        return p[:, c * tn:c * tn + width]

    xp = x_prompt.reshape(batch * seq, d_model)
    tm_p = PROJ_ROWS
    assert seq % tm_p == 0
    qqi_t = _proj_qqi(xp, g_mix, wt_head, batch=batch, seq=seq, tm=tm_p, row_blocks=(0, QI_COL))
    kv_t, k_nat, vt_bf, side_t, ki_nat = _proj_kv(xp, g_mix, wt_head, batch=batch, seq=seq,
                                                  tm=tm_p, side_block=SIDE_BLOCK)
    pp = _proj_nat(xp, g_mix, wt_tail, tm_p, tn)

    tq, kc = ATTN_QUERIES, ATTN_KEYS
    topk_p = min(TOPK_MAX, seq // 4)
    attn_p = _prompt_attention(qqi_t, side_t, ki_nat, k_nat, vt_bf, batch=batch, seq=seq, tq=tq,
                               kc=kc, topk=topk_p)
    pool_p = _pool_prompt(pp, wg, scale, batch=batch, seq=seq, tm=POOL_ROWS, u_col=U_COL)
    h_p, z_p = _merge(xp, attn_p, pool_p, pp, b_gate[l], wa, wp, wo, g_ffn, tm=MERGE_ROWS,
                      ga_col=GA_COL, gb_col=GB_COL)
    y_p = _ffn(z_p, h_p, wu, wd, g_fin, tm=FFN_ROWS, tf=FFN_COLS)

    xs = x_sample.reshape(dec_batch, d_model)
    head_tile = max(t for t in (1024, 768, 512, 384, 256, 128) if n_head_rows % t == 0)
    ps_head = _proj_nat(xs, g_mix, wt_head, dec_batch, head_tile)
    ps = _proj_nat(xs, g_mix, wt_tail, dec_batch, tn)
    topk_s = min(TOPK_MAX, (past + dec_seq) // 4)
    q_s = col(ps_head, 0)
    k_s = col(ps_head, K_COL)
    v_s = col(ps_head, V_COL)
    ki_s = ps_head[:, c_ki:c_ki + IDX_DIM]
    wi_s = ps_head[:, c_ki + IDX_DIM:c_u]
    u_s = col(ps, U_COL)
    qi_s = col(ps_head, QI_COL).reshape(dec_batch, IDX_HEADS, IDX_DIM)
    kidx_t = jnp.swapaxes(cache_kidx, 2, 3)
    ck_t = jnp.transpose(cache_k, (0, 1, 3, 4, 2))
    cv_t = jnp.transpose(cache_v, (0, 1, 3, 4, 2))
    newpage = jnp.zeros((dec_batch, IDX_DIM, page), F32).at[:, :, 0].set(ki_s)
    scores, score_new = _sample_scores(page_table, qi_s, wi_s[:, :, None], newpage, kidx_t,
                                       pages_step=score_pages)
    bias, bias_new = _sample_select(scores.reshape(dec_batch, past),
                                    score_new.reshape(dec_batch, page), topk=topk_s)
    head4 = (dec_batch, N_HEADS, HEAD_DIM, 1)
    attn_s = _sample_attend(page_table, q_s.reshape(head4), k_s.reshape(head4),
                            v_s.reshape(head4), bias[:, None, :], bias_new[:, None, :],
                            ck_t, cv_t, pages_step=attend_pages)
    pool_s = _pool_sample(state_pool[l], u_s, wg, scale)
    h_s, z_s = _merge(xs, attn_s.reshape(dec_batch, ATTN_WIDTH).astype(BF16), pool_s, ps,
                      b_gate[l], wa, wp, wo, g_ffn, tm=dec_batch, ga_col=GA_COL, gb_col=GB_COL)
    y_s = _ffn(z_s, h_s, wu, wd, g_fin, tm=dec_batch, tf=FFN_COLS)

    n_state = state_pool.shape[2]
    u_p = pp.reshape(batch, seq, -1)[:, -n_state:, U_COL * tn:(U_COL + 1) * tn]
    pool_state_s = jnp.concatenate([state_pool[l].astype(F32), u_s[:, None, :]], axis=1)[:, -n_state:]

    def heads_last(t):
        return t.reshape(batch, N_HEADS, HEAD_DIM, seq).transpose(0, 3, 1, 2)[None]

    return (
        y_p.reshape(batch, seq, d_model),
        y_s.reshape(dec_batch, dec_seq, d_model),
        heads_last(kv_t[0]),
        heads_last(kv_t[1]),
        side_t[:, :IDX_DIM, :].transpose(0, 2, 1)[None],
        u_p[None],
        k_s.reshape(1, dec_batch, dec_seq, N_HEADS, HEAD_DIM),
        v_s.reshape(1, dec_batch, dec_seq, N_HEADS, HEAD_DIM),
        ki_s.reshape(1, dec_batch, dec_seq, IDX_DIM),
        pool_state_s[None],
    )
```

```python
import functools

import jax
import jax.numpy as jnp
from jax import lax
from jax.experimental import pallas as pl
from jax.experimental.pallas import tpu as pltpu

F32 = jnp.float32
BF16 = jnp.bfloat16
I32 = jnp.int32

N_HEADS = 16
HEAD_DIM = 64
ATTN_WIDTH = N_HEADS * HEAD_DIM
IDX_HEADS = 16
IDX_DIM = 64
TOPK_MAX = 256
POOL_WINDOWS = (2, 4, 8, 16)
POOL_GROUP_WIDTH = 256
POOL_HALO = 16
EPS = 1e-6
INT_MIN = -(2 ** 31)
LANES = 128
NEG_BIG = -1e30
Q_SCALE = HEAD_DIM ** -0.5 * 1.4426950408889634
V_ROWS = HEAD_DIM + 16

VMEM_LIMIT_BYTES = 56 * 1024 * 1024

PROJ_ROWS = 512
PROJ_COLS = 1024
ATTN_QUERIES = 256
ATTN_KEYS = 128
POOL_ROWS = 256
MERGE_ROWS = 256
FFN_ROWS = 512
FFN_COLS = 512
SCORE_PAGES = 32


def _params(*sem):
    return pltpu.CompilerParams(dimension_semantics=sem, vmem_limit_bytes=VMEM_LIMIT_BYTES)


def _sortable_key(x):
    b = lax.bitcast_convert_type(x, I32)
    return jnp.where(b >= 0, b, b ^ jnp.int32(0x7FFFFFFF))


_CONTRACT_LAST = (((1,), (1,)), ((), ()))


def _rms_to_bf16(x_ref, g_ref, z_ref):
    x = x_ref[...]
    ms = jnp.mean(x * x, axis=-1, keepdims=True)
    z_ref[...] = (x * lax.rsqrt(ms + EPS) * g_ref[...]).astype(BF16)


def _proj_nat_body(x_ref, g_ref, wt_ref, o_ref, z_ref):
    @pl.when(pl.program_id(1) == 0)
    def _():
        _rms_to_bf16(x_ref, g_ref, z_ref)

    o_ref[...] = lax.dot_general(z_ref[...], wt_ref[...], _CONTRACT_LAST,
                                 preferred_element_type=F32)


def _proj_nat(x, g, wt, tm, tn):
    m, d = x.shape
    n = wt.shape[0]
    return pl.pallas_call(
        _proj_nat_body,
        grid=(m // tm, n // tn),
        in_specs=[
            pl.BlockSpec((tm, d), lambda i, j: (i, 0)),
            pl.BlockSpec((1, d), lambda i, j: (0, 0)),
            pl.BlockSpec((tn, d), lambda i, j: (j, 0)),
        ],
        out_specs=pl.BlockSpec((tm, tn), lambda i, j: (i, j)),
        out_shape=jax.ShapeDtypeStruct((m, n), F32),
        scratch_shapes=[pltpu.VMEM((tm, d), BF16)],
        compiler_params=_params("parallel", "arbitrary"),
        name="rms_proj_rows",
    )(x, g, wt)


def _proj_qqi_body(x_ref, g_ref, wt_ref, o_ref, z_ref):
    @pl.when(pl.program_id(1) == 0)
    def _():
        _rms_to_bf16(x_ref, g_ref, z_ref)

    o_ref[...] = lax.dot_general(wt_ref[...], z_ref[...], _CONTRACT_LAST,
                                 preferred_element_type=F32).astype(BF16)


def _proj_qqi(x, g, wt_head, *, batch, seq, tm, row_blocks):
    d = x.shape[1]
    nblk = seq // tm
    blocks = tuple(row_blocks)
    assert blocks == (0, 3)
    return pl.pallas_call(
        _proj_qqi_body,
        grid=(batch * nblk, len(blocks)),
        in_specs=[
            pl.BlockSpec((tm, d), lambda i, j: (i, 0)),
            pl.BlockSpec((1, d), lambda i, j: (0, 0)),
            pl.BlockSpec((ATTN_WIDTH, d), lambda i, j: (3 * j, 0)),
        ],
        out_specs=pl.BlockSpec((None, None, ATTN_WIDTH, tm),
                               lambda i, j: (j, i // nblk, 0, i % nblk)),
        out_shape=jax.ShapeDtypeStruct((len(blocks), batch, ATTN_WIDTH, seq), BF16),
        scratch_shapes=[pltpu.VMEM((tm, d), BF16)],
        compiler_params=_params("parallel", "arbitrary"),
        name="rms_proj_q_qi",
    )(x, g, wt_head)


def _proj_kv_body(x_ref, g_ref, wt_ref, ws_ref, kt_ref, vt_ref, knat_ref, vtb_ref, sidet_ref,
                  kinat_ref, z_ref):
    j = pl.program_id(1)

    @pl.when(j == 0)
    def _():
        _rms_to_bf16(x_ref, g_ref, z_ref)
        side = lax.dot_general(ws_ref[...], z_ref[...], _CONTRACT_LAST,
                               preferred_element_type=F32)
        sidet_ref[...] = side
        kinat_ref[...] = side.T.astype(BF16)

    o = lax.dot_general(wt_ref[...], z_ref[...], _CONTRACT_LAST, preferred_element_type=F32)

    @pl.when(j == 0)
    def _():
        kt_ref[...] = o
        knat_ref[...] = o.T.astype(BF16)

    @pl.when(j == 1)
    def _():
        vt_ref[...] = o
        tm = o.shape[1]
        vtb_ref[:, 0:HEAD_DIM, :] = o.reshape(N_HEADS, HEAD_DIM, tm).astype(BF16)
        row = lax.broadcasted_iota(I32, (N_HEADS, V_ROWS - HEAD_DIM, tm), 1)
        vtb_ref[:, HEAD_DIM:V_ROWS, :] = jnp.where(row == 0, 1.0, 0.0).astype(BF16)


def _proj_kv(x, g, wt_head, *, batch, seq, tm, side_block):
    d = x.shape[1]
    nblk = seq // tm
    tok = batch * seq
    bi = lambda i, j: (i // nblk, 0, i % nblk)
    return pl.pallas_call(
        _proj_kv_body,
        grid=(batch * nblk, 2),
        in_specs=[
            pl.BlockSpec((tm, d), lambda i, j: (i, 0)),
            pl.BlockSpec((1, d), lambda i, j: (0, 0)),
            pl.BlockSpec((ATTN_WIDTH, d), lambda i, j: (1 + j, 0)),
            pl.BlockSpec((LANES, d), lambda i, j: (side_block, 0)),
        ],
        out_specs=[
            pl.BlockSpec((None, ATTN_WIDTH, tm), bi),
            pl.BlockSpec((None, ATTN_WIDTH, tm), bi),
            pl.BlockSpec((tm, ATTN_WIDTH), lambda i, j: (i, 0)),
            pl.BlockSpec((None, N_HEADS, V_ROWS, tm), lambda i, j: (i // nblk, 0, 0, i % nblk)),
            pl.BlockSpec((None, LANES, tm), bi),
            pl.BlockSpec((tm, LANES), lambda i, j: (i, 0)),
        ],
        out_shape=[
            jax.ShapeDtypeStruct((batch, ATTN_WIDTH, seq), F32),
            jax.ShapeDtypeStruct((batch, ATTN_WIDTH, seq), F32),
            jax.ShapeDtypeStruct((tok, ATTN_WIDTH), BF16),
            jax.ShapeDtypeStruct((batch, N_HEADS, V_ROWS, seq), BF16),
            jax.ShapeDtypeStruct((batch, LANES, seq), F32),
            jax.ShapeDtypeStruct((tok, LANES), BF16),
        ],
        scratch_shapes=[pltpu.VMEM((tm, d), BF16)],
        compiler_params=_params("parallel", "arbitrary"),
        name="rms_proj_k_v",
    )(x, g, wt_head, wt_head)


def _prompt_attn_body(qit_ref, side_ref, qt_ref, ki_ref, k_ref, vt_ref, o_ref,
                      keys_ref, qtp_ref, ot_ref, m_ref, l_ref, a_ref, s_ref, p_ref,
                      *, tq, kc, topk):
    i = pl.program_id(1)
    nchunk = (i + 1) * (tq // kc)
    q0 = i * tq

    w16 = side_ref[IDX_DIM:IDX_DIM + IDX_HEADS, :] * (IDX_HEADS ** -0.5 * IDX_DIM ** -0.5)
    qt = qt_ref[...].astype(F32).reshape(N_HEADS // 2, 2 * HEAD_DIM, tq)
    first_half = lax.broadcasted_iota(I32, (2 * HEAD_DIM, tq), 0) < HEAD_DIM
    for j in range(N_HEADS // 2):
        qtp_ref[2 * j] = jnp.where(first_half, qt[j], 0.0).astype(BF16)
        qtp_ref[2 * j + 1] = jnp.where(first_half, 0.0, qt[j]).astype(BF16)

    def idx_chunk(c, carry):
        r0 = pl.multiple_of(c * kc, kc)
        kic = ki_ref[pl.ds(r0, kc), 0:IDX_DIM]
        acc = jnp.zeros((kc, tq), F32)
        for h in range(IDX_HEADS):
            s = jnp.dot(kic, qit_ref[h * IDX_DIM:(h + 1) * IDX_DIM, :],
                        preferred_element_type=F32)
            acc = acc + jnp.maximum(s, 0.0) * w16[h:h + 1, :]
        key = _sortable_key(acc)
        krow = r0 + lax.broadcasted_iota(I32, (kc, tq), 0)
        qcol = q0 + lax.broadcasted_iota(I32, (kc, tq), 1)
        keys_ref[pl.ds(r0, kc), :] = jnp.where(krow <= qcol, key, INT_MIN)
        return carry

    lax.fori_loop(0, nchunk, idx_chunk, 0)

    def count_ge(cand):
        def body(c, cnt):
            r0 = pl.multiple_of(c * tq, tq)
            ge = (keys_ref[pl.ds(r0, tq), :] >= cand).astype(I32)
            return cnt + jnp.sum(ge.reshape(tq // 8, 8, tq), axis=0)

        cnt8 = lax.fori_loop(0, i + 1, body, jnp.zeros((8, tq), I32))
        return jnp.sum(cnt8, axis=0, keepdims=True)

    def bit_iter(bi, tu):
        cand_u = tu | jnp.left_shift(jnp.int32(1), 31 - bi)
        cnt = count_ge(cand_u ^ INT_MIN)
        return jnp.where(cnt >= topk, cand_u, tu)

    tu = lax.fori_loop(0, 32, bit_iter, jnp.zeros((1, tq), I32))
    thr = tu ^ INT_MIN
    tsel = jnp.maximum(thr, INT_MIN + 1)

    n_ge = count_ge(tsel)
    has_tie = jnp.max((n_ge > topk).astype(I32)) > 0

    def count_eq_below(bound):
        def body(c, cnt):
            r0 = pl.multiple_of(c * kc, kc)
            krow = r0 + lax.broadcasted_iota(I32, (kc, tq), 0)
            hit = ((keys_ref[pl.ds(r0, kc), :] == tsel) & (krow < bound)).astype(I32)
            return cnt + jnp.sum(hit.reshape(kc // 8, 8, tq), axis=0)

        cnt8 = lax.fori_loop(0, nchunk, body, jnp.zeros((8, tq), I32))
        return jnp.sum(cnt8, axis=0, keepdims=True)

    @pl.when(has_tie)
    def _():
        need = topk - (n_ge - count_eq_below(jnp.int32(2 ** 30)))

        def jbit(bi, jlo):
            cand = jlo | jnp.left_shift(jnp.int32(1), 29 - bi)
            return jnp.where(count_eq_below(cand) < need, cand, jlo)

        jstar = lax.fori_loop(0, 30, jbit, jnp.zeros((1, tq), I32))

        def drop(c, carry):
            r0 = pl.multiple_of(c * kc, kc)
            krow = r0 + lax.broadcasted_iota(I32, (kc, tq), 0)
            kk = keys_ref[pl.ds(r0, kc), :]
            keys_ref[pl.ds(r0, kc), :] = jnp.where((kk == tsel) & (krow > jstar), INT_MIN, kk)
            return carry

        lax.fori_loop(0, nchunk, drop, 0)

    m_ref[...] = jnp.full(m_ref.shape, NEG_BIG, F32)
    l_ref[...] = jnp.zeros(l_ref.shape, F32)
    ot_ref[...] = jnp.zeros(ot_ref.shape, F32)

    def chunk_body(c, carry):
        r0 = pl.multiple_of(c * kc, kc)
        bias = jnp.where(keys_ref[pl.ds(r0, kc), :] >= tsel, 0.0, -jnp.inf)
        for h in range(N_HEADS):
            pair = slice((h // 2) * 2 * HEAD_DIM, (h // 2 + 1) * 2 * HEAD_DIM)
            s_ref[h] = jnp.dot(k_ref[pl.ds(r0, kc), pair], qtp_ref[h],
                               preferred_element_type=F32) + bias
        for h in range(N_HEADS):
            s = s_ref[h]
            m_old = m_ref[h]
            m_new = jnp.maximum(m_old, jnp.max(s, axis=0, keepdims=True))
            p_ref[h] = jnp.exp2(s - m_new).astype(BF16)
            a_ref[h] = jnp.exp2(m_old - m_new)
            m_ref[h] = m_new
        for h in range(N_HEADS):
            pv = jnp.dot(vt_ref[h, :, pl.ds(r0, kc)], p_ref[h], preferred_element_type=F32)
            ot_ref[h] = a_ref[h] * ot_ref[h] + pv[0:HEAD_DIM]
            l_ref[h] = a_ref[h] * l_ref[h] + pv[HEAD_DIM:HEAD_DIM + 1]
        return carry

    lax.fori_loop(0, nchunk, chunk_body, 0)
    o = ot_ref[...] / l_ref[...]
    o_ref[...] = o.reshape(ATTN_WIDTH, tq).T.astype(BF16)


def _prompt_attention(qqi_t, side_t, ki_nat, k_nat, vt, *, batch, seq, tq, kc, topk):
    nq = seq // tq
    body = functools.partial(_prompt_attn_body, tq=tq, kc=kc, topk=topk)
    return pl.pallas_call(
        body,
        grid=(batch, nq),
        in_specs=[
            pl.BlockSpec((None, None, ATTN_WIDTH, tq), lambda b, i: (1, b, 0, i)),
            pl.BlockSpec((None, LANES, tq), lambda b, i: (b, 0, i)),
            pl.BlockSpec((None, None, ATTN_WIDTH, tq), lambda b, i: (0, b, 0, i)),
            pl.BlockSpec((seq, LANES), lambda b, i: (b, 0)),
            pl.BlockSpec((seq, ATTN_WIDTH), lambda b, i: (b, 0)),
            pl.BlockSpec((None, N_HEADS, V_ROWS, seq), lambda b, i: (b, 0, 0, 0)),
        ],
        out_specs=pl.BlockSpec((tq, ATTN_WIDTH), lambda b, i: (b * nq + i, 0)),
        out_shape=jax.ShapeDtypeStruct((batch * seq, ATTN_WIDTH), BF16),
        scratch_shapes=[
            pltpu.VMEM((seq, tq), I32),
            pltpu.VMEM((N_HEADS, 2 * HEAD_DIM, tq), BF16),
            pltpu.VMEM((N_HEADS, HEAD_DIM, tq), F32),
            pltpu.VMEM((N_HEADS, 1, tq), F32),
            pltpu.VMEM((N_HEADS, 1, tq), F32),
            pltpu.VMEM((N_HEADS, 1, tq), F32),
            pltpu.VMEM((N_HEADS, kc, tq), F32),
            pltpu.VMEM((N_HEADS, kc, tq), BF16),
        ],
        compiler_params=_params("parallel", "arbitrary"),
        name="prompt_sparse_attn",
    )(qqi_t, side_t, qqi_t, ki_nat, k_nat, vt)


def _pool_prompt_body(u_ref, wg_ref, sc_ref, o_ref, ue_ref, *, tm):
    i = pl.program_id(1)

    @pl.when(i == 0)
    def _():
        ue_ref[0:POOL_HALO, :] = jnp.zeros((POOL_HALO, ue_ref.shape[1]), F32)

    u = u_ref[...]
    ue_ref[POOL_HALO:POOL_HALO + tm, :] = u
    pos = i * tm + lax.broadcasted_iota(I32, (tm, 1), 0)
    outs = []
    for g, w in enumerate(POOL_WINDOWS):
        cols = slice(g * POOL_GROUP_WIDTH, (g + 1) * POOL_GROUP_WIDTH)
        win = u[:, cols]
        for back in range(1, w):
            win = win + ue_ref[POOL_HALO - back:POOL_HALO - back + tm, cols]
        cnt = jnp.minimum(pos + 1, w).astype(F32)
        mixed = win / cnt - u[:, cols]
        outs.append(jnp.dot(mixed.astype(BF16), wg_ref[g], preferred_element_type=F32))
    o_ref[...] = (jnp.concatenate(outs, axis=1) * sc_ref[...]).astype(BF16)
    ue_ref[0:POOL_HALO, :] = u[tm - POOL_HALO:, :]


def _pool_prompt(p_main, wg, scale, *, batch, seq, tm, u_col):
    nblk = seq // tm
    width = wg.shape[0] * wg.shape[1]
    return pl.pallas_call(
        functools.partial(_pool_prompt_body, tm=tm),
        grid=(batch, nblk),
        in_specs=[
            pl.BlockSpec((tm, width), lambda b, i: (b * nblk + i, u_col)),
            pl.BlockSpec(wg.shape, lambda b, i: (0, 0, 0)),
            pl.BlockSpec((1, width), lambda b, i: (0, 0)),
        ],
        out_specs=pl.BlockSpec((tm, width), lambda b, i: (b * nblk + i, 0)),
        out_shape=jax.ShapeDtypeStruct((batch * seq, width), BF16),
        scratch_shapes=[pltpu.VMEM((POOL_HALO + tm, width), F32)],
        compiler_params=_params("parallel", "arbitrary"),
        name="pool_prompt",
    )(p_main, wg, scale)


def _pool_sample_body(st_ref, u_ref, wg_ref, sc_ref, o_ref):
    st = st_ref[...]
    u = u_ref[...]
    n_state = st.shape[1]
    outs = []
    for g, w in enumerate(POOL_WINDOWS):
        cols = slice(g * POOL_GROUP_WIDTH, (g + 1) * POOL_GROUP_WIDTH)
        win = u[:, cols] + jnp.sum(st[:, n_state - (w - 1):, cols], axis=1)
        mixed = win / float(w) - u[:, cols]
        outs.append(jnp.dot(mixed.astype(BF16), wg_ref[g], preferred_element_type=F32))
    o_ref[...] = (jnp.concatenate(outs, axis=1) * sc_ref[...]).astype(BF16)


def _pool_sample(state, u_new, wg, scale):
    nb, width = u_new.shape
    return pl.pallas_call(
        _pool_sample_body,
        out_shape=jax.ShapeDtypeStruct((nb, width), BF16),
        compiler_params=pltpu.CompilerParams(vmem_limit_bytes=VMEM_LIMIT_BYTES),
        name="pool_sample",
    )(state, u_new, wg, scale)


def _merge_body(x_ref, ao_ref, po_ref, ga_ref, gb_ref, bg_ref, wa_ref, wp_ref, wo_ref, g2_ref,
                h_ref, z_ref):
    ya = jnp.dot(ao_ref[...], wa_ref[...], preferred_element_type=F32)
    yb = jnp.dot(po_ref[...], wp_ref[...], preferred_element_type=F32)
    merged = (jax.nn.sigmoid(ga_ref[...] + bg_ref[0:1, :]) * ya
              + jax.nn.sigmoid(gb_ref[...] + bg_ref[1:2, :]) * yb)
    h = x_ref[...] + jnp.dot(merged.astype(BF16), wo_ref[...], preferred_element_type=F32)
    h_ref[...] = h
    ms = jnp.mean(h * h, axis=-1, keepdims=True)
    z_ref[...] = (h * lax.rsqrt(ms + EPS) * g2_ref[...]).astype(BF16)


def _merge(x, attn_o, pool_o, p_main, b_gate, wa, wp, wo, g2, *, tm, ga_col, gb_col):
    m, d = x.shape
    const = lambda i: (0, 0)
    return pl.pallas_call(
        _merge_body,
        grid=(m // tm,),
        in_specs=[
            pl.BlockSpec((tm, d), lambda i: (i, 0)),
            pl.BlockSpec((tm, attn_o.shape[1]), lambda i: (i, 0)),
            pl.BlockSpec((tm, pool_o.shape[1]), lambda i: (i, 0)),
            pl.BlockSpec((tm, d), lambda i: (i, ga_col)),
            pl.BlockSpec((tm, d), lambda i: (i, gb_col)),
            pl.BlockSpec(b_gate.shape, const),
            pl.BlockSpec(wa.shape, const),
            pl.BlockSpec(wp.shape, const),
            pl.BlockSpec(wo.shape, const),
            pl.BlockSpec((1, d), const),
        ],
        out_specs=[pl.BlockSpec((tm, d), lambda i: (i, 0)), pl.BlockSpec((tm, d), lambda i: (i, 0))],
        out_shape=[jax.ShapeDtypeStruct((m, d), F32), jax.ShapeDtypeStruct((m, d), BF16)],
        compiler_params=_params("parallel"),
        name="gated_merge",
    )(x, attn_o, pool_o, p_main, p_main, b_gate, wa, wp, wo, g2)


def _ffn_body(z_ref, h_ref, wu_ref, wd_ref, gf_ref, y_ref, acc_ref):
    f = pl.program_id(1)

    @pl.when(f == 0)
    def _():
        acc_ref[...] = h_ref[...]

    a = jnp.dot(z_ref[...], wu_ref[...], preferred_element_type=F32)
    a = jnp.square(jnp.maximum(a, 0.0))
    acc_ref[...] += jnp.dot(a.astype(BF16), wd_ref[...], preferred_element_type=F32)

    @pl.when(f == pl.num_programs(1) - 1)
    def _():
        h2 = acc_ref[...]
        ms = jnp.mean(h2 * h2, axis=-1, keepdims=True)
        y_ref[...] = h2 * lax.rsqrt(ms + EPS) * gf_ref[...]


def _ffn(z, h, wu, wd, gf, *, tm, tf):
    m, d = h.shape
    dff = wu.shape[1]
    return pl.pallas_call(
        _ffn_body,
        grid=(m // tm, dff // tf),
        in_specs=[
            pl.BlockSpec((tm, d), lambda i, f: (i, 0)),
            pl.BlockSpec((tm, d), lambda i, f: (i, 0)),
            pl.BlockSpec((d, tf), lambda i, f: (0, f)),
            pl.BlockSpec((tf, d), lambda i, f: (f, 0)),
            pl.BlockSpec((1, d), lambda i, f: (0, 0)),
        ],
        out_specs=pl.BlockSpec((tm, d), lambda i, f: (i, 0)),
        out_shape=jax.ShapeDtypeStruct((m, d), F32),
        scratch_shapes=[pltpu.VMEM((tm, d), F32)],
        compiler_params=_params("parallel", "arbitrary"),
        name="ffn_final_norm",
    )(z, h, wu, wd, gf)


def _sample_scores_body(pt_ref, qi_ref, w_ref, newpage_ref, *rest, n_pages_step):
    page_refs = rest[:n_pages_step]
    o_ref, onew_ref = rest[n_pages_step:]
    qi = qi_ref[...].astype(BF16)
    wcol = w_ref[...] * (IDX_HEADS ** -0.5 * IDX_DIM ** -0.5)

    def score(pages_t):
        s = jnp.dot(qi, pages_t.astype(BF16), preferred_element_type=F32)
        return jnp.sum(jnp.maximum(s, 0.0) * wcol, axis=0, keepdims=True)

    o_ref[...] = score(jnp.concatenate([r[...] for r in page_refs], axis=1))
    onew_ref[...] = score(newpage_ref[...])


def _sample_scores(page_table, qi, w, newpage, kidx_t, *, pages_step):
    nb, n_pages = page_table.shape
    page = kidx_t.shape[3]
    steps = n_pages // pages_step

    def page_spec(j):
        return pl.BlockSpec((None, None, IDX_DIM, page),
                            lambda b, g, pt: (0, pt[b, g * pages_step + j], 0, 0))

    grid_spec = pltpu.PrefetchScalarGridSpec(
        num_scalar_prefetch=1,
        grid=(nb, steps),
        in_specs=[
            pl.BlockSpec((None, IDX_HEADS, IDX_DIM), lambda b, g, pt: (b, 0, 0)),
            pl.BlockSpec((None, IDX_HEADS, 1), lambda b, g, pt: (b, 0, 0)),
            pl.BlockSpec((None, IDX_DIM, page), lambda b, g, pt: (b, 0, 0)),
        ] + [page_spec(j) for j in range(pages_step)],
        out_specs=[
            pl.BlockSpec((None, 1, pages_step * page), lambda b, g, pt: (b, 0, g)),
            pl.BlockSpec((None, 1, page), lambda b, g, pt: (b, 0, 0)),
        ],
    )
    return pl.pallas_call(
        functools.partial(_sample_scores_body, n_pages_step=pages_step),
        grid_spec=grid_spec,
        out_shape=[jax.ShapeDtypeStruct((nb, 1, n_pages * page), F32),
                   jax.ShapeDtypeStruct((nb, 1, page), F32)],
        compiler_params=_params("parallel", "arbitrary"),
        name="sample_indexer_scores",
    )(page_table, qi, w, newpage, *([kidx_t] * pages_step))


def _prefix_exclusive(mask_f32, tri):
    rows, length = mask_f32.shape
    off = jnp.zeros((rows, 1), F32)
    parts = []
    for c in range(length // LANES):
        mc = mask_f32[:, c * LANES:(c + 1) * LANES]
        inc = jnp.dot(mc.astype(BF16), tri, preferred_element_type=F32)
        parts.append(inc - mc + off)
        off = off + inc[:, LANES - 1:LANES]
    return jnp.concatenate(parts, axis=1)


def _sample_select_body(sc_ref, scnew_ref, bias_ref, biasnew_ref, *, topk):
    past = sc_ref.shape[1]
    sc = jnp.concatenate([sc_ref[...], scnew_ref[...]], axis=1)
    nb, length = sc.shape
    pos = lax.broadcasted_iota(I32, (nb, length), 1)
    key = jnp.where(pos <= past, _sortable_key(sc), INT_MIN)

    def bit_iter(bi, tu):
        cand_u = tu | jnp.left_shift(jnp.int32(1), 31 - bi)
        cnt = jnp.sum((key >= (cand_u ^ INT_MIN)).astype(I32), axis=1, keepdims=True)
        return jnp.where(cnt >= topk, cand_u, tu)

    tu = lax.fori_loop(0, 32, bit_iter, jnp.zeros((nb, 1), I32))
    thr = tu ^ INT_MIN
    valid = key > INT_MIN
    gt = key > thr
    eq = key == thr
    tri = (lax.broadcasted_iota(I32, (LANES, LANES), 0)
           <= lax.broadcasted_iota(I32, (LANES, LANES), 1)).astype(BF16)
    need = (topk - jnp.sum(gt.astype(I32), axis=1, keepdims=True)).astype(F32)
    eq_rank = _prefix_exclusive(eq.astype(F32), tri)
    sel = (gt | (eq & (eq_rank < need))) & valid
    bias = jnp.where(sel, 0.0, -jnp.inf)
    bias_ref[...] = bias[:, :past]
    biasnew_ref[...] = bias[:, past:]


def _sample_select(scores, score_new, *, topk):
    nb, past = scores.shape
    return pl.pallas_call(
        functools.partial(_sample_select_body, topk=topk),
        out_shape=[jax.ShapeDtypeStruct((nb, past), F32),
                   jax.ShapeDtypeStruct(score_new.shape, F32)],
        compiler_params=pltpu.CompilerParams(vmem_limit_bytes=VMEM_LIMIT_BYTES),
        name="sample_topk_select",
    )(scores, score_new)


def _heads_on_lanes_to_slabs(x_t, page):
    return jnp.stack([jnp.broadcast_to(x_t[:, h:h + 1], (HEAD_DIM, page))
                      for h in range(N_HEADS)])


def _paged_attend_step(g, last, qt_ref, knt_ref, vnt_ref, bias_ref, biasnew_ref, k_refs, v_refs,
                       o_ref, m_ref, l_ref, acc_ref, *, page):
    @pl.when(g == 0)
    def _():
        m_ref[...] = jnp.full(m_ref.shape, NEG_BIG, F32)
        l_ref[...] = jnp.zeros(l_ref.shape, F32)
        acc_ref[...] = jnp.zeros(acc_ref.shape, F32)

    q = _heads_on_lanes_to_slabs(qt_ref[...], page)

    def update(scores, values):
        m_old = m_ref[...]
        m_new = m_old
        for s in scores:
            m_new = jnp.maximum(m_new, s)
        alpha = jnp.exp2(m_old - m_new)
        l = alpha * l_ref[...]
        acc = alpha * acc_ref[...]
        for s, v in zip(scores, values):
            p = jnp.exp2(s - m_new)
            l = l + p
            acc = acc + p * v
        m_ref[...] = m_new
        l_ref[...] = l
        acc_ref[...] = acc

    scores = []
    for j, k_ref in enumerate(k_refs):
        s = jnp.sum(k_ref[...] * q, axis=1, keepdims=True)
        scores.append(s + bias_ref[:, j * page:(j + 1) * page][None])
    update(scores, [v_ref[...] for v_ref in v_refs])

    @pl.when(last)
    def _():
        kn = _heads_on_lanes_to_slabs(knt_ref[...], page)
        s_new = jnp.sum(kn * q, axis=1, keepdims=True) + biasnew_ref[...][None]
        update([s_new], [_heads_on_lanes_to_slabs(vnt_ref[...], page)])
        m = m_ref[...]
        f = jnp.exp2(m - jnp.max(m, axis=-1, keepdims=True))
        denom = jnp.sum(l_ref[...] * f, axis=-1, keepdims=True)
        o_ref[...] = jnp.sum(acc_ref[...] * f, axis=-1, keepdims=True) / denom


def _ffn_attend_body(pt_ref, z_ref, h_ref, wu_ref, wd_ref, gf_ref,
                     qt_ref, knt_ref, vnt_ref, bias_ref, biasnew_ref, *rest,
                     n_pages_step, page, steps_per_sample):
    k_refs = rest[:n_pages_step]
    v_refs = rest[n_pages_step:2 * n_pages_step]
    y_ref, o_ref, acc_ref, m_ref, l_ref, pacc_ref = rest[2 * n_pages_step:]
    _ffn_body(z_ref, h_ref, wu_ref, wd_ref, gf_ref, y_ref, acc_ref)
    step = pl.program_id(0) * pl.num_programs(1) + pl.program_id(1)
    g = step % steps_per_sample
    _paged_attend_step(g, g == steps_per_sample - 1, qt_ref, knt_ref, vnt_ref, bias_ref,
                       biasnew_ref, k_refs, v_refs, o_ref, m_ref, l_ref, pacc_ref, page=page)


def _ffn_attend(z, h, wu, wd, gf, page_table, qt, knt, vnt, bias, bias_new, ck_t, cv_t, *, tm, tf):
    m, d = h.shape
    dff = wu.shape[1]
    nb, n_pages = page_table.shape
    page = ck_t.shape[4]
    ni, nf = m // tm, dff // tf
    total_steps = ni * nf
    assert (nb * n_pages) % total_steps == 0
    pages_step = nb * n_pages // total_steps
    assert n_pages % pages_step == 0
    sps = n_pages // pages_step

    def sample(i, f):
        return (i * nf + f) // sps

    def page_spec(j):
        def index(i, f, pt):
            t = i * nf + f
            return (0, pt[t // sps, (t % sps) * pages_step + j], 0, 0, 0)

        return pl.BlockSpec((None, None, N_HEADS, HEAD_DIM, page), index)

    pages = [page_spec(j) for j in range(pages_step)]
    tq_block = pl.BlockSpec((None, HEAD_DIM, N_HEADS), lambda i, f, pt: (sample(i, f), 0, 0))
    grid_spec = pltpu.PrefetchScalarGridSpec(
        num_scalar_prefetch=1,
        grid=(ni, nf),
        in_specs=[
            pl.BlockSpec((tm, d), lambda i, f, pt: (i, 0)),
            pl.BlockSpec((tm, d), lambda i, f, pt: (i, 0)),
            pl.BlockSpec((d, tf), lambda i, f, pt: (0, f)),
            pl.BlockSpec((tf, d), lambda i, f, pt: (f, 0)),
            pl.BlockSpec((1, d), lambda i, f, pt: (0, 0)),
            tq_block, tq_block, tq_block,
            pl.BlockSpec((None, 1, pages_step * page),
                         lambda i, f, pt: (sample(i, f), 0, (i * nf + f) % sps)),
            pl.BlockSpec((None, 1, page), lambda i, f, pt: (sample(i, f), 0, 0)),
        ] + pages + pages,
        out_specs=[
            pl.BlockSpec((tm, d), lambda i, f, pt: (i, 0)),
            pl.BlockSpec((None, N_HEADS, HEAD_DIM, 1), lambda i, f, pt: (sample(i, f), 0, 0, 0)),
        ],
        scratch_shapes=[
            pltpu.VMEM((tm, d), F32),
            pltpu.VMEM((N_HEADS, 1, page), F32),
            pltpu.VMEM((N_HEADS, 1, page), F32),
            pltpu.VMEM((N_HEADS, HEAD_DIM, page), F32),
        ],
    )
    return pl.pallas_call(
        functools.partial(_ffn_attend_body, n_pages_step=pages_step, page=page,
                          steps_per_sample=sps),
        grid_spec=grid_spec,
        out_shape=[jax.ShapeDtypeStruct((m, d), F32),
                   jax.ShapeDtypeStruct((nb, N_HEADS, HEAD_DIM, 1), F32)],
        compiler_params=_params("arbitrary", "arbitrary"),
        name="ffn_with_paged_attend",
    )(page_table, z, h, wu, wd, gf, qt, knt, vnt, bias, bias_new,
      *([ck_t] * pages_step), *([cv_t] * pages_step))


def kernel(x_prompt, x_sample, cache_k, cache_v, cache_kidx, state_pool, page_table, norm_mix_g,
           w_in, b_gate, w_attn_out, w_pool_group, pool_scale, w_pool_out, w_out, norm_ffn_g,
           w_up, w_down, norm_final_g):
    depth = w_in.shape[0]
    assert depth == 1, "single-layer step"
    batch, seq, d_model = x_prompt.shape
    dec_batch, dec_seq, _ = x_sample.shape
    assert dec_seq == 1, "one new token per sample"
    n_pages = page_table.shape[1]
    page = cache_k.shape[2]
    past = n_pages * page
    pool_width = pool_scale.shape[1]
    score_pages = min(SCORE_PAGES, n_pages)
    assert page == LANES and n_pages % score_pages == 0
    l = 0

    tn = PROJ_COLS
    assert ATTN_WIDTH == tn and pool_width == tn and d_model == 2 * tn and IDX_HEADS * IDX_DIM == tn
    c_ki = 3 * ATTN_WIDTH + IDX_HEADS * IDX_DIM
    c_u = c_ki + IDX_DIM + IDX_HEADS
    c_ga = c_u + pool_width
    n_head_rows = c_ki + LANES
    wt = jnp.swapaxes(w_in[l], 0, 1)
    wt_head = jnp.concatenate([wt[:ATTN_WIDTH] * Q_SCALE, wt[ATTN_WIDTH:n_head_rows]],
                              axis=0).astype(BF16)
    wt_tail = jnp.concatenate([wt[c_ga:], wt[c_u:c_ga]], axis=0).astype(BF16)
    K_COL, V_COL, QI_COL = 1, 2, 3
    SIDE_BLOCK = c_ki // LANES
    GA_COL, GB_COL, U_COL = 0, 1, 4
    g_mix = norm_mix_g[l][None, :]
    wa = w_attn_out[l].astype(BF16)
    wg = w_pool_group[l].astype(BF16)
    wp = w_pool_out[l].astype(BF16)
    wo = w_out[l].astype(BF16)
    wu = w_up[l].astype(BF16)
    wd = w_down[l].astype(BF16)
    g_ffn = norm_ffn_g[l][None, :]
    g_fin = norm_final_g[None, :]
    scale = pool_scale[l][None, :]

    def col(p, c, width=tn):
        return p[:, c * tn:c * tn + width]

    xp = x_prompt.reshape(batch * seq, d_model)
    tm_p = PROJ_ROWS
    assert seq % tm_p == 0
    qqi_t = _proj_qqi(xp, g_mix, wt_head, batch=batch, seq=seq, tm=tm_p, row_blocks=(0, QI_COL))
    k_t, v_t, k_nat, vt_bf, side_t, ki_nat = _proj_kv(xp, g_mix, wt_head, batch=batch, seq=seq,
                                                      tm=tm_p, side_block=SIDE_BLOCK)
    pp = _proj_nat(xp, g_mix, wt_tail, tm_p, tn)

    tq, kc = ATTN_QUERIES, ATTN_KEYS
    topk_p = min(TOPK_MAX, seq // 4)
    attn_p = _prompt_attention(qqi_t, side_t, ki_nat, k_nat, vt_bf, batch=batch, seq=seq, tq=tq,
                               kc=kc, topk=topk_p)
    pool_p = _pool_prompt(pp, wg, scale, batch=batch, seq=seq, tm=POOL_ROWS, u_col=U_COL)
    h_p, z_p = _merge(xp, attn_p, pool_p, pp, b_gate[l], wa, wp, wo, g_ffn, tm=MERGE_ROWS,
                      ga_col=GA_COL, gb_col=GB_COL)

    xs = x_sample.reshape(dec_batch, d_model)
    head_tile = max(t for t in (1024, 768, 512, 384, 256, 128) if n_head_rows % t == 0)
    ps_head = _proj_nat(xs, g_mix, wt_head, dec_batch, head_tile)
    ps = _proj_nat(xs, g_mix, wt_tail, dec_batch, tn)
    topk_s = min(TOPK_MAX, (past + dec_seq) // 4)
    q_s = col(ps_head, 0)
    k_s = col(ps_head, K_COL)
    v_s = col(ps_head, V_COL)
    ki_s = ps_head[:, c_ki:c_ki + IDX_DIM]
    wi_s = ps_head[:, c_ki + IDX_DIM:c_u]
    u_s = col(ps, U_COL)
    qi_s = col(ps_head, QI_COL).reshape(dec_batch, IDX_HEADS, IDX_DIM)
    kidx_t = jnp.swapaxes(cache_kidx, 2, 3)
    ck_t = jnp.transpose(cache_k, (0, 1, 3, 4, 2))
    cv_t = jnp.transpose(cache_v, (0, 1, 3, 4, 2))
    newpage = jnp.zeros((dec_batch, IDX_DIM, page), F32).at[:, :, 0].set(ki_s)
    scores, score_new = _sample_scores(page_table, qi_s, wi_s[:, :, None], newpage, kidx_t,
                                       pages_step=score_pages)
    bias, bias_new = _sample_select(scores.reshape(dec_batch, past),
                                    score_new.reshape(dec_batch, page), topk=topk_s)

    def dims_by_heads(t):
        return t.reshape(dec_batch, N_HEADS, HEAD_DIM).transpose(0, 2, 1)

    y_p, attn_s = _ffn_attend(z_p, h_p, wu, wd, g_fin, page_table, dims_by_heads(q_s),
                              dims_by_heads(k_s), dims_by_heads(v_s), bias[:, None, :],
                              bias_new[:, None, :], ck_t, cv_t, tm=FFN_ROWS, tf=FFN_COLS)
    pool_s = _pool_sample(state_pool[l], u_s, wg, scale)
    h_s, z_s = _merge(xs, attn_s.reshape(dec_batch, ATTN_WIDTH).astype(BF16), pool_s, ps,
                      b_gate[l], wa, wp, wo, g_ffn, tm=dec_batch, ga_col=GA_COL, gb_col=GB_COL)
    y_s = _ffn(z_s, h_s, wu, wd, g_fin, tm=dec_batch, tf=FFN_COLS)

    n_state = state_pool.shape[2]
    u_p = pp.reshape(batch, seq, -1)[:, -n_state:, U_COL * tn:(U_COL + 1) * tn]
    pool_state_s = jnp.concatenate([state_pool[l].astype(F32), u_s[:, None, :]], axis=1)[:, -n_state:]

    def heads_last(t):
        return t.reshape(batch, N_HEADS, HEAD_DIM, seq).transpose(0, 3, 1, 2)[None]

    return (
        y_p.reshape(batch, seq, d_model),
        y_s.reshape(dec_batch, dec_seq, d_model),
        heads_last(k_t),
        heads_last(v_t),
        side_t[:, :IDX_DIM, :].transpose(0, 2, 1)[None],
        u_p[None],
        k_s.reshape(1, dec_batch, dec_seq, N_HEADS, HEAD_DIM),
        v_s.reshape(1, dec_batch, dec_seq, N_HEADS, HEAD_DIM),
        ki_s.reshape(1, dec_batch, dec_seq, IDX_DIM),
        pool_state_s[None],
    )
```

```python
import functools

import jax
import jax.numpy as jnp
from jax import lax
from jax.experimental import pallas as pl
from jax.experimental.pallas import tpu as pltpu

F32 = jnp.float32
BF16 = jnp.bfloat16
I32 = jnp.int32

N_HEADS = 16
HEAD_DIM = 64
ATTN_WIDTH = N_HEADS * HEAD_DIM
IDX_HEADS = 16
IDX_DIM = 64
TOPK_MAX = 256
POOL_WINDOWS = (2, 4, 8, 16)
POOL_GROUP_WIDTH = 256
POOL_HALO = 16
EPS = 1e-6
INT_MIN = -(2 ** 31)
LANES = 128
NEG_BIG = -1e30
Q_SCALE = HEAD_DIM ** -0.5 * 1.4426950408889634
V_ROWS = HEAD_DIM + 16

VMEM_LIMIT_BYTES = 56 * 1024 * 1024

PROJ_ROWS = 512
PROJ_ROWS_WIDE = 1024
PROJ_COLS = 1024
ATTN_QUERIES = 256
ATTN_KEYS = 128
POOL_ROWS = 256
MERGE_ROWS = 256
FFN_ROWS = 512
FFN_COLS = 512
SCORE_PAGES = 64
SEARCH_FIXED_STEPS = 12
SEARCH_GROUP = 4
BISECT_MAX_STEPS = 512


def _params(*sem):
    return pltpu.CompilerParams(dimension_semantics=sem, vmem_limit_bytes=VMEM_LIMIT_BYTES)


_CONTRACT_LAST = (((1,), (1,)), ((), ()))


def _rms_to_bf16(x_ref, g_ref, z_ref):
    x = x_ref[...]
    ms = jnp.mean(x * x, axis=-1, keepdims=True)
    z_ref[...] = (x * lax.rsqrt(ms + EPS) * g_ref[...]).astype(BF16)


def _proj_nat_body(x_ref, g_ref, wt_ref, o_ref, z_ref):
    @pl.when(pl.program_id(1) == 0)
    def _():
        _rms_to_bf16(x_ref, g_ref, z_ref)

    o_ref[...] = lax.dot_general(z_ref[...], wt_ref[...], _CONTRACT_LAST,
                                 preferred_element_type=F32)


def _proj_nat(x, g, wt, tm, tn):
    m, d = x.shape
    n = wt.shape[0]
    return pl.pallas_call(
        _proj_nat_body,
        grid=(m // tm, n // tn),
        in_specs=[
            pl.BlockSpec((tm, d), lambda i, j: (i, 0)),
            pl.BlockSpec((1, d), lambda i, j: (0, 0)),
            pl.BlockSpec((tn, d), lambda i, j: (j, 0)),
        ],
        out_specs=pl.BlockSpec((tm, tn), lambda i, j: (i, j)),
        out_shape=jax.ShapeDtypeStruct((m, n), F32),
        scratch_shapes=[pltpu.VMEM((tm, d), BF16)],
        compiler_params=_params("parallel", "arbitrary"),
        name="rms_proj_rows",
    )(x, g, wt)


def _proj_qqi_body(x_ref, g_ref, wt_ref, o_ref, z_ref):
    @pl.when(pl.program_id(1) == 0)
    def _():
        _rms_to_bf16(x_ref, g_ref, z_ref)

    o_ref[...] = lax.dot_general(wt_ref[...], z_ref[...], _CONTRACT_LAST,
                                 preferred_element_type=F32).astype(BF16)


def _proj_qqi(x, g, wt_head, *, batch, seq, tm, row_blocks):
    d = x.shape[1]
    nblk = seq // tm
    blocks = tuple(row_blocks)
    assert blocks == (0, 3)
    return pl.pallas_call(
        _proj_qqi_body,
        grid=(batch * nblk, len(blocks)),
        in_specs=[
            pl.BlockSpec((tm, d), lambda i, j: (i, 0)),
            pl.BlockSpec((1, d), lambda i, j: (0, 0)),
            pl.BlockSpec((ATTN_WIDTH, d), lambda i, j: (3 * j, 0)),
        ],
        out_specs=pl.BlockSpec((None, None, ATTN_WIDTH, tm),
                               lambda i, j: (j, i // nblk, 0, i % nblk)),
        out_shape=jax.ShapeDtypeStruct((len(blocks), batch, ATTN_WIDTH, seq), BF16),
        scratch_shapes=[pltpu.VMEM((tm, d), BF16)],
        compiler_params=_params("parallel", "arbitrary"),
        name="rms_proj_q_qi",
    )(x, g, wt_head)


def _proj_kv_body(x_ref, g_ref, wt_ref, ws_ref, kt_ref, vt_ref, knat_ref, vtb_ref, sidet_ref,
                  kinat_ref, z_ref):
    j = pl.program_id(1)

    @pl.when(j == 0)
    def _():
        _rms_to_bf16(x_ref, g_ref, z_ref)
        side = lax.dot_general(ws_ref[...], z_ref[...], _CONTRACT_LAST,
                               preferred_element_type=F32)
        sidet_ref[...] = side
        kinat_ref[...] = side.T.astype(BF16)

    o = lax.dot_general(wt_ref[...], z_ref[...], _CONTRACT_LAST, preferred_element_type=F32)

    @pl.when(j == 0)
    def _():
        kt_ref[...] = o
        knat_ref[...] = o.T.astype(BF16)

    @pl.when(j == 1)
    def _():
        vt_ref[...] = o
        tm = o.shape[1]
        vtb_ref[:, 0:HEAD_DIM, :] = o.reshape(N_HEADS, HEAD_DIM, tm).astype(BF16)
        row = lax.broadcasted_iota(I32, (N_HEADS, V_ROWS - HEAD_DIM, tm), 1)
        vtb_ref[:, HEAD_DIM:V_ROWS, :] = jnp.where(row == 0, 1.0, 0.0).astype(BF16)


def _proj_kv(x, g, wt_head, *, batch, seq, tm, side_block):
    d = x.shape[1]
    nblk = seq // tm
    tok = batch * seq
    bi = lambda i, j: (i // nblk, 0, i % nblk)
    return pl.pallas_call(
        _proj_kv_body,
        grid=(batch * nblk, 2),
        in_specs=[
            pl.BlockSpec((tm, d), lambda i, j: (i, 0)),
            pl.BlockSpec((1, d), lambda i, j: (0, 0)),
            pl.BlockSpec((ATTN_WIDTH, d), lambda i, j: (1 + j, 0)),
            pl.BlockSpec((LANES, d), lambda i, j: (side_block, 0)),
        ],
        out_specs=[
            pl.BlockSpec((None, ATTN_WIDTH, tm), bi),
            pl.BlockSpec((None, ATTN_WIDTH, tm), bi),
            pl.BlockSpec((tm, ATTN_WIDTH), lambda i, j: (i, 0)),
            pl.BlockSpec((None, N_HEADS, V_ROWS, tm), lambda i, j: (i // nblk, 0, 0, i % nblk)),
            pl.BlockSpec((None, LANES, tm), bi),
            pl.BlockSpec((tm, LANES), lambda i, j: (i, 0)),
        ],
        out_shape=[
            jax.ShapeDtypeStruct((batch, ATTN_WIDTH, seq), F32),
            jax.ShapeDtypeStruct((batch, ATTN_WIDTH, seq), F32),
            jax.ShapeDtypeStruct((tok, ATTN_WIDTH), BF16),
            jax.ShapeDtypeStruct((batch, N_HEADS, V_ROWS, seq), BF16),
            jax.ShapeDtypeStruct((batch, LANES, seq), F32),
            jax.ShapeDtypeStruct((tok, LANES), BF16),
        ],
        scratch_shapes=[pltpu.VMEM((tm, d), BF16)],
        compiler_params=_params("parallel", "arbitrary"),
        name="rms_proj_k_v",
    )(x, g, wt_head, wt_head)


def _prompt_attn_body(qit_ref, side_ref, qt_ref, ki_ref, k_ref, vt_ref, o_ref,
                      sc_ref, thr_ref, qtp_ref, ot_ref, m_ref, l_ref, a_ref, s_ref, p_ref,
                      *, tq, kc, topk):
    i = pl.program_id(1)
    nchunk = (i + 1) * (tq // kc)
    q0 = i * tq

    w16 = side_ref[IDX_DIM:IDX_DIM + IDX_HEADS, :] * (IDX_HEADS ** -0.5 * IDX_DIM ** -0.5)
    qt = qt_ref[...].astype(F32).reshape(N_HEADS // 2, 2 * HEAD_DIM, tq)
    first_half = lax.broadcasted_iota(I32, (2 * HEAD_DIM, tq), 0) < HEAD_DIM
    for j in range(N_HEADS // 2):
        qtp_ref[2 * j] = jnp.where(first_half, qt[j], 0.0).astype(BF16)
        qtp_ref[2 * j + 1] = jnp.where(first_half, 0.0, qt[j]).astype(BF16)

    def idx_chunk(c, carry):
        lo8, hi8 = carry
        r0 = pl.multiple_of(c * kc, kc)
        kic = ki_ref[pl.ds(r0, kc), 0:IDX_DIM]
        acc = jnp.zeros((kc, tq), F32)
        for h in range(IDX_HEADS):
            s = jnp.dot(kic, qit_ref[h * IDX_DIM:(h + 1) * IDX_DIM, :],
                        preferred_element_type=F32)
            acc = acc + jnp.maximum(s, 0.0) * w16[h:h + 1, :]
        krow = r0 + lax.broadcasted_iota(I32, (kc, tq), 0)
        qcol = q0 + lax.broadcasted_iota(I32, (kc, tq), 1)
        causal = krow <= qcol
        sc_ref[pl.ds(r0, kc), :] = jnp.where(causal, acc, -jnp.inf)
        lo8 = jnp.minimum(lo8, jnp.min(jnp.where(causal, acc, jnp.inf).reshape(kc // 8, 8, tq),
                                       axis=0))
        hi8 = jnp.maximum(hi8, jnp.max(jnp.where(causal, acc, -jnp.inf).reshape(kc // 8, 8, tq),
                                       axis=0))
        return lo8, hi8

    lo8, hi8 = lax.fori_loop(0, nchunk, idx_chunk,
                             (jnp.full((8, tq), jnp.inf, F32), jnp.full((8, tq), -jnp.inf, F32)))
    lo0 = jnp.min(lo8, axis=0, keepdims=True)
    hi0 = jnp.max(hi8, axis=0, keepdims=True)

    def count_rows(pred):
        def body(c, cnt):
            r0 = pl.multiple_of(c * tq, tq)
            krow = r0 + lax.broadcasted_iota(I32, (tq, tq), 0)
            hit = pred(sc_ref[pl.ds(r0, tq), :], krow).astype(I32)
            return cnt + jnp.sum(hit.reshape(tq // 8, 8, tq), axis=0)

        cnt8 = lax.fori_loop(0, i + 1, body, jnp.zeros((8, tq), I32))
        return jnp.sum(cnt8, axis=0, keepdims=True)

    def count_ge(t):
        return count_rows(lambda s, _: s >= t)

    def bisect(_, state):
        lo, hi, n_lo = state
        mid = 0.5 * lo + 0.5 * hi
        cnt = count_ge(mid)
        take = cnt >= topk
        return jnp.where(take, mid, lo), jnp.where(take, hi, mid), jnp.where(take, cnt, n_lo)

    def unresolved(state):
        lo, hi, n_lo = state
        mid = 0.5 * lo + 0.5 * hi
        return (n_lo > topk) & (mid > lo) & (mid < hi)

    n_causal = jnp.minimum(q0 + lax.broadcasted_iota(I32, (1, tq), 1) + 1, (i + 1) * tq)
    state = lax.fori_loop(0, SEARCH_FIXED_STEPS, bisect, (lo0, hi0, n_causal))

    def more_steps(carry):
        it, st = carry
        return jnp.logical_and(it < BISECT_MAX_STEPS, jnp.max(unresolved(st).astype(I32)) > 0)

    def step_group(carry):
        it, st = carry
        return it + SEARCH_GROUP, lax.fori_loop(0, SEARCH_GROUP, bisect, st)

    _, (lo, hi, n_lo) = lax.while_loop(more_steps, step_group,
                                       (jnp.int32(SEARCH_FIXED_STEPS), state))
    thr_ref[...] = lo

    has_tie = jnp.max((n_lo > topk).astype(I32)) > 0

    @pl.when(has_tie)
    def _():
        n_hi = count_ge(hi)
        thr = jnp.where((n_lo > topk) & (n_hi >= topk), hi, lo)
        thr_ref[...] = thr
        need = topk - count_rows(lambda s, _: s > thr)

        def jbit(bi, jlo):
            cand = jlo | jnp.left_shift(jnp.int32(1), 29 - bi)
            below = count_rows(lambda s, krow: (s == thr) & (krow < cand))
            return jnp.where(below < need, cand, jlo)

        jstar = lax.fori_loop(0, 30, jbit, jnp.zeros((1, tq), I32))

        def drop(c, carry):
            r0 = pl.multiple_of(c * kc, kc)
            krow = r0 + lax.broadcasted_iota(I32, (kc, tq), 0)
            s = sc_ref[pl.ds(r0, kc), :]
            sc_ref[pl.ds(r0, kc), :] = jnp.where((s == thr) & (krow > jstar), -jnp.inf, s)
            return carry

        lax.fori_loop(0, nchunk, drop, 0)

    tsel = thr_ref[...]

    m_ref[...] = jnp.full(m_ref.shape, NEG_BIG, F32)
    l_ref[...] = jnp.zeros(l_ref.shape, F32)
    ot_ref[...] = jnp.zeros(ot_ref.shape, F32)

    def chunk_body(c, carry):
        r0 = pl.multiple_of(c * kc, kc)
        bias = jnp.where(sc_ref[pl.ds(r0, kc), :] >= tsel, 0.0, -jnp.inf)
        for h in range(N_HEADS):
            pair = slice((h // 2) * 2 * HEAD_DIM, (h // 2 + 1) * 2 * HEAD_DIM)
            s_ref[h] = jnp.dot(k_ref[pl.ds(r0, kc), pair], qtp_ref[h],
                               preferred_element_type=F32) + bias
        for h in range(N_HEADS):
            s = s_ref[h]
            m_old = m_ref[h]
            m_new = jnp.maximum(m_old, jnp.max(s, axis=0, keepdims=True))
            p_ref[h] = jnp.exp2(s - m_new).astype(BF16)
            a_ref[h] = jnp.exp2(m_old - m_new)
            m_ref[h] = m_new
        for h in range(N_HEADS):
            pv = jnp.dot(vt_ref[h, :, pl.ds(r0, kc)], p_ref[h], preferred_element_type=F32)
            ot_ref[h] = a_ref[h] * ot_ref[h] + pv[0:HEAD_DIM]
            l_ref[h] = a_ref[h] * l_ref[h] + pv[HEAD_DIM:HEAD_DIM + 1]
        return carry

    lax.fori_loop(0, nchunk, chunk_body, 0)
    o = ot_ref[...] / l_ref[...]
    o_ref[...] = o.reshape(ATTN_WIDTH, tq).T.astype(BF16)


def _prompt_attention(qqi_t, side_t, ki_nat, k_nat, vt, *, batch, seq, tq, kc, topk):
    nq = seq // tq
    body = functools.partial(_prompt_attn_body, tq=tq, kc=kc, topk=topk)
    return pl.pallas_call(
        body,
        grid=(batch, nq),
        in_specs=[
            pl.BlockSpec((None, None, ATTN_WIDTH, tq), lambda b, i: (1, b, 0, i)),
            pl.BlockSpec((None, LANES, tq), lambda b, i: (b, 0, i)),
            pl.BlockSpec((None, None, ATTN_WIDTH, tq), lambda b, i: (0, b, 0, i)),
            pl.BlockSpec((seq, LANES), lambda b, i: (b, 0)),
            pl.BlockSpec((seq, ATTN_WIDTH), lambda b, i: (b, 0)),
            pl.BlockSpec((None, N_HEADS, V_ROWS, seq), lambda b, i: (b, 0, 0, 0)),
        ],
        out_specs=pl.BlockSpec((tq, ATTN_WIDTH), lambda b, i: (b * nq + i, 0)),
        out_shape=jax.ShapeDtypeStruct((batch * seq, ATTN_WIDTH), BF16),
        scratch_shapes=[
            pltpu.VMEM((seq, tq), F32),
            pltpu.VMEM((1, tq), F32),
            pltpu.VMEM((N_HEADS, 2 * HEAD_DIM, tq), BF16),
            pltpu.VMEM((N_HEADS, HEAD_DIM, tq), F32),
            pltpu.VMEM((N_HEADS, 1, tq), F32),
            pltpu.VMEM((N_HEADS, 1, tq), F32),
            pltpu.VMEM((N_HEADS, 1, tq), F32),
            pltpu.VMEM((N_HEADS, kc, tq), F32),
            pltpu.VMEM((N_HEADS, kc, tq), BF16),
        ],
        compiler_params=_params("parallel", "arbitrary"),
        name="prompt_sparse_attn",
    )(qqi_t, side_t, qqi_t, ki_nat, k_nat, vt)


def _pool_prompt_body(u_ref, wg_ref, sc_ref, o_ref, ue_ref, *, tm):
    i = pl.program_id(1)

    @pl.when(i == 0)
    def _():
        ue_ref[0:POOL_HALO, :] = jnp.zeros((POOL_HALO, ue_ref.shape[1]), F32)

    u = u_ref[...]
    ue_ref[POOL_HALO:POOL_HALO + tm, :] = u
    pos = i * tm + lax.broadcasted_iota(I32, (tm, 1), 0)
    outs = []
    for g, w in enumerate(POOL_WINDOWS):
        cols = slice(g * POOL_GROUP_WIDTH, (g + 1) * POOL_GROUP_WIDTH)
        win = u[:, cols]
        for back in range(1, w):
            win = win + ue_ref[POOL_HALO - back:POOL_HALO - back + tm, cols]
        cnt = jnp.minimum(pos + 1, w).astype(F32)
        mixed = win / cnt - u[:, cols]
        outs.append(jnp.dot(mixed.astype(BF16), wg_ref[g], preferred_element_type=F32))
    o_ref[...] = (jnp.concatenate(outs, axis=1) * sc_ref[...]).astype(BF16)
    ue_ref[0:POOL_HALO, :] = u[tm - POOL_HALO:, :]


def _pool_prompt(p_main, wg, scale, *, batch, seq, tm, u_col):
    nblk = seq // tm
    width = wg.shape[0] * wg.shape[1]
    return pl.pallas_call(
        functools.partial(_pool_prompt_body, tm=tm),
        grid=(batch, nblk),
        in_specs=[
            pl.BlockSpec((tm, width), lambda b, i: (b * nblk + i, u_col)),
            pl.BlockSpec(wg.shape, lambda b, i: (0, 0, 0)),
            pl.BlockSpec((1, width), lambda b, i: (0, 0)),
        ],
        out_specs=pl.BlockSpec((tm, width), lambda b, i: (b * nblk + i, 0)),
        out_shape=jax.ShapeDtypeStruct((batch * seq, width), BF16),
        scratch_shapes=[pltpu.VMEM((POOL_HALO + tm, width), F32)],
        compiler_params=_params("parallel", "arbitrary"),
        name="pool_prompt",
    )(p_main, wg, scale)


def _pool_sample_body(st_ref, u_ref, wg_ref, sc_ref, o_ref):
    st = st_ref[...]
    u = u_ref[...]
    n_state = st.shape[1]
    outs = []
    for g, w in enumerate(POOL_WINDOWS):
        cols = slice(g * POOL_GROUP_WIDTH, (g + 1) * POOL_GROUP_WIDTH)
        win = u[:, cols] + jnp.sum(st[:, n_state - (w - 1):, cols], axis=1)
        mixed = win / float(w) - u[:, cols]
        outs.append(jnp.dot(mixed.astype(BF16), wg_ref[g], preferred_element_type=F32))
    o_ref[...] = (jnp.concatenate(outs, axis=1) * sc_ref[...]).astype(BF16)


def _pool_sample(state, u_new, wg, scale):
    nb, width = u_new.shape
    return pl.pallas_call(
        _pool_sample_body,
        out_shape=jax.ShapeDtypeStruct((nb, width), BF16),
        compiler_params=pltpu.CompilerParams(vmem_limit_bytes=VMEM_LIMIT_BYTES),
        name="pool_sample",
    )(state, u_new, wg, scale)


def _merge_body(x_ref, ao_ref, po_ref, ga_ref, gb_ref, bg_ref, wa_ref, wp_ref, wo_ref, g2_ref,
                h_ref, z_ref):
    ya = jnp.dot(ao_ref[...], wa_ref[...], preferred_element_type=F32)
    yb = jnp.dot(po_ref[...], wp_ref[...], preferred_element_type=F32)
    merged = (jax.nn.sigmoid(ga_ref[...] + bg_ref[0:1, :]) * ya
              + jax.nn.sigmoid(gb_ref[...] + bg_ref[1:2, :]) * yb)
    h = x_ref[...] + jnp.dot(merged.astype(BF16), wo_ref[...], preferred_element_type=F32)
    h_ref[...] = h
    ms = jnp.mean(h * h, axis=-1, keepdims=True)
    z_ref[...] = (h * lax.rsqrt(ms + EPS) * g2_ref[...]).astype(BF16)


def _merge(x, attn_o, pool_o, p_main, b_gate, wa, wp, wo, g2, *, tm, ga_col, gb_col):
    m, d = x.shape
    const = lambda i: (0, 0)
    return pl.pallas_call(
        _merge_body,
        grid=(m // tm,),
        in_specs=[
            pl.BlockSpec((tm, d), lambda i: (i, 0)),
            pl.BlockSpec((tm, attn_o.shape[1]), lambda i: (i, 0)),
            pl.BlockSpec((tm, pool_o.shape[1]), lambda i: (i, 0)),
            pl.BlockSpec((tm, d), lambda i: (i, ga_col)),
            pl.BlockSpec((tm, d), lambda i: (i, gb_col)),
            pl.BlockSpec(b_gate.shape, const),
            pl.BlockSpec(wa.shape, const),
            pl.BlockSpec(wp.shape, const),
            pl.BlockSpec(wo.shape, const),
            pl.BlockSpec((1, d), const),
        ],
        out_specs=[pl.BlockSpec((tm, d), lambda i: (i, 0)), pl.BlockSpec((tm, d), lambda i: (i, 0))],
        out_shape=[jax.ShapeDtypeStruct((m, d), F32), jax.ShapeDtypeStruct((m, d), BF16)],
        compiler_params=_params("parallel"),
        name="gated_merge",
    )(x, attn_o, pool_o, p_main, p_main, b_gate, wa, wp, wo, g2)


def _ffn_body(z_ref, h_ref, wu_ref, wd_ref, gf_ref, y_ref, acc_ref):
    f = pl.program_id(1)

    @pl.when(f == 0)
    def _():
        acc_ref[...] = h_ref[...]

    a = jnp.dot(z_ref[...], wu_ref[...], preferred_element_type=F32)
    a = jnp.square(jnp.maximum(a, 0.0))
    acc_ref[...] += jnp.dot(a.astype(BF16), wd_ref[...], preferred_element_type=F32)

    @pl.when(f == pl.num_programs(1) - 1)
    def _():
        h2 = acc_ref[...]
        ms = jnp.mean(h2 * h2, axis=-1, keepdims=True)
        y_ref[...] = h2 * lax.rsqrt(ms + EPS) * gf_ref[...]


def _ffn(z, h, wu, wd, gf, *, tm, tf):
    m, d = h.shape
    dff = wu.shape[1]
    return pl.pallas_call(
        _ffn_body,
        grid=(m // tm, dff // tf),
        in_specs=[
            pl.BlockSpec((tm, d), lambda i, f: (i, 0)),
            pl.BlockSpec((tm, d), lambda i, f: (i, 0)),
            pl.BlockSpec((d, tf), lambda i, f: (0, f)),
            pl.BlockSpec((tf, d), lambda i, f: (f, 0)),
            pl.BlockSpec((1, d), lambda i, f: (0, 0)),
        ],
        out_specs=pl.BlockSpec((tm, d), lambda i, f: (i, 0)),
        out_shape=jax.ShapeDtypeStruct((m, d), F32),
        scratch_shapes=[pltpu.VMEM((tm, d), F32)],
        compiler_params=_params("parallel", "arbitrary"),
        name="ffn_final_norm",
    )(z, h, wu, wd, gf)


def _sample_scores_body(pt_ref, qi_ref, w_ref, newpage_ref, *rest, n_pages_step):
    page_refs = rest[:n_pages_step]
    o_ref, onew_ref = rest[n_pages_step:]
    qi = qi_ref[...].astype(BF16)
    wcol = w_ref[...] * (IDX_HEADS ** -0.5 * IDX_DIM ** -0.5)

    def score(pages_t):
        s = jnp.dot(qi, pages_t.astype(BF16), preferred_element_type=F32)
        return jnp.sum(jnp.maximum(s, 0.0) * wcol, axis=0, keepdims=True)

    o_ref[...] = score(jnp.concatenate([r[...] for r in page_refs], axis=1))
    onew_ref[...] = score(newpage_ref[...])


def _sample_scores(page_table, qi, w, newpage, kidx_t, *, pages_step):
    nb, n_pages = page_table.shape
    page = kidx_t.shape[3]
    steps = n_pages // pages_step

    def page_spec(j):
        return pl.BlockSpec((None, None, IDX_DIM, page),
                            lambda b, g, pt: (0, pt[b, g * pages_step + j], 0, 0))

    grid_spec = pltpu.PrefetchScalarGridSpec(
        num_scalar_prefetch=1,
        grid=(nb, steps),
        in_specs=[
            pl.BlockSpec((None, IDX_HEADS, IDX_DIM), lambda b, g, pt: (b, 0, 0)),
            pl.BlockSpec((None, IDX_HEADS, 1), lambda b, g, pt: (b, 0, 0)),
            pl.BlockSpec((None, IDX_DIM, page), lambda b, g, pt: (b, 0, 0)),
        ] + [page_spec(j) for j in range(pages_step)],
        out_specs=[
            pl.BlockSpec((None, 1, pages_step * page), lambda b, g, pt: (b, 0, g)),
            pl.BlockSpec((None, 1, page), lambda b, g, pt: (b, 0, 0)),
        ],
    )
    return pl.pallas_call(
        functools.partial(_sample_scores_body, n_pages_step=pages_step),
        grid_spec=grid_spec,
        out_shape=[jax.ShapeDtypeStruct((nb, 1, n_pages * page), F32),
                   jax.ShapeDtypeStruct((nb, 1, page), F32)],
        compiler_params=_params("parallel", "arbitrary"),
        name="sample_indexer_scores",
    )(page_table, qi, w, newpage, *([kidx_t] * pages_step))


def _prefix_exclusive(mask_f32, tri):
    rows, length = mask_f32.shape
    off = jnp.zeros((rows, 1), F32)
    parts = []
    for c in range(length // LANES):
        mc = mask_f32[:, c * LANES:(c + 1) * LANES]
        inc = jnp.dot(mc.astype(BF16), tri, preferred_element_type=F32)
        parts.append(inc - mc + off)
        off = off + inc[:, LANES - 1:LANES]
    return jnp.concatenate(parts, axis=1)


def _sample_select_body(sc_ref, scnew_ref, bias_ref, biasnew_ref, *, topk):
    past = sc_ref.shape[1]
    sc = jnp.concatenate([sc_ref[...], scnew_ref[...]], axis=1)
    nb, length = sc.shape
    pos = lax.broadcasted_iota(I32, (nb, length), 1)
    valid = pos <= past
    scv = jnp.where(valid, sc, -jnp.inf)

    def count_ge(t):
        return jnp.sum((scv >= t).astype(I32), axis=1, keepdims=True)

    lo0 = jnp.min(jnp.where(valid, sc, jnp.inf), axis=1, keepdims=True)
    hi0 = jnp.max(scv, axis=1, keepdims=True)

    def unresolved(lo, hi, n_lo):
        mid = 0.5 * lo + 0.5 * hi
        return (n_lo > topk) & (mid > lo) & (mid < hi)

    def more(carry):
        it, lo, hi, n_lo = carry
        return jnp.logical_and(it < BISECT_MAX_STEPS,
                               jnp.max(unresolved(lo, hi, n_lo).astype(I32)) > 0)

    def step(carry):
        it, lo, hi, n_lo = carry
        mid = 0.5 * lo + 0.5 * hi
        cnt = count_ge(mid)
        take = cnt >= topk
        return it + 1, jnp.where(take, mid, lo), jnp.where(take, hi, mid), jnp.where(take, cnt, n_lo)

    _, lo, hi, n_lo = lax.while_loop(more, step, (jnp.int32(0), lo0, hi0, count_ge(lo0)))
    thr = jnp.where(count_ge(hi) >= topk, hi, lo)
    gt = scv > thr
    eq = (scv == thr) & valid
    tri = (lax.broadcasted_iota(I32, (LANES, LANES), 0)
           <= lax.broadcasted_iota(I32, (LANES, LANES), 1)).astype(BF16)
    need = (topk - jnp.sum(gt.astype(I32), axis=1, keepdims=True)).astype(F32)
    eq_rank = _prefix_exclusive(eq.astype(F32), tri)
    sel = (gt | (eq & (eq_rank < need))) & valid
    bias = jnp.where(sel, 0.0, -jnp.inf)
    bias_ref[...] = bias[:, :past]
    biasnew_ref[...] = bias[:, past:]


def _sample_select(scores, score_new, *, topk):
    nb, past = scores.shape
    return pl.pallas_call(
        functools.partial(_sample_select_body, topk=topk),
        out_shape=[jax.ShapeDtypeStruct((nb, past), F32),
                   jax.ShapeDtypeStruct(score_new.shape, F32)],
        compiler_params=pltpu.CompilerParams(vmem_limit_bytes=VMEM_LIMIT_BYTES),
        name="sample_topk_select",
    )(scores, score_new)


def _heads_on_lanes_to_slabs(x_t, page):
    return jnp.stack([jnp.broadcast_to(x_t[:, h:h + 1], (HEAD_DIM, page))
                      for h in range(N_HEADS)])


def _lane_softmax_update(scores, values, m_ref, l_ref, acc_ref):
    m_old = m_ref[...]
    m_new = m_old
    for s in scores:
        m_new = jnp.maximum(m_new, s)
    alpha = jnp.exp2(m_old - m_new)
    l = alpha * l_ref[...]
    acc = alpha * acc_ref[...]
    for s, v in zip(scores, values):
        p = jnp.exp2(s - m_new)
        l = l + p
        acc = acc + p * v
    m_ref[...] = m_new
    l_ref[...] = l
    acc_ref[...] = acc


def _attend_pages(qt_ref, bias_ref, k_refs, v_refs, m_ref, l_ref, acc_ref, *, page):
    q = _heads_on_lanes_to_slabs(qt_ref[...], page)
    scores = []
    for j, k_ref in enumerate(k_refs):
        s = jnp.sum(k_ref[...] * q, axis=1, keepdims=True)
        scores.append(s + bias_ref[:, j * page:(j + 1) * page][None])
    _lane_softmax_update(scores, [v_ref[...] for v_ref in v_refs], m_ref, l_ref, acc_ref)


def _attend_finish(qt_ref, knt_ref, vnt_ref, biasnew_ref, o_ref, m_ref, l_ref, acc_ref, *, page):
    q = _heads_on_lanes_to_slabs(qt_ref[...], page)
    kn = _heads_on_lanes_to_slabs(knt_ref[...], page)
    s_new = jnp.sum(kn * q, axis=1, keepdims=True) + biasnew_ref[...][None]
    _lane_softmax_update([s_new], [_heads_on_lanes_to_slabs(vnt_ref[...], page)],
                         m_ref, l_ref, acc_ref)
    m = m_ref[...]
    f = jnp.exp2(m - jnp.max(m, axis=-1, keepdims=True))
    denom = jnp.sum(l_ref[...] * f, axis=-1, keepdims=True)
    o_ref[...] = jnp.sum(acc_ref[...] * f, axis=-1, keepdims=True) / denom


def _ffn_attend_body(pt_ref, z_ref, h_ref, wu_ref, wd_ref, gf_ref,
                     qt_ref, knt_ref, vnt_ref, bias_ref, biasnew_ref, *rest,
                     n_pages_step, page, steps_per_sample):
    k_refs = rest[:n_pages_step]
    v_refs = rest[n_pages_step:2 * n_pages_step]
    y_ref, o_ref, m_ref, l_ref, pacc_ref = rest[2 * n_pages_step:]
    acc_ref = y_ref
    f = pl.program_id(1)
    g = (pl.program_id(0) * pl.num_programs(1) + f) % steps_per_sample

    @pl.when(f == 0)
    def _():
        acc_ref[...] = h_ref[...]

    @pl.when(g == 0)
    def _():
        m_ref[...] = jnp.full(m_ref.shape, NEG_BIG, F32)
        l_ref[...] = jnp.zeros(l_ref.shape, F32)
        pacc_ref[...] = jnp.zeros(pacc_ref.shape, F32)

    a = jnp.dot(z_ref[...], wu_ref[...], preferred_element_type=F32)
    a = jnp.square(jnp.maximum(a, 0.0))
    acc_ref[...] += jnp.dot(a.astype(BF16), wd_ref[...], preferred_element_type=F32)
    _attend_pages(qt_ref, bias_ref, k_refs, v_refs, m_ref, l_ref, pacc_ref, page=page)

    @pl.when(f == pl.num_programs(1) - 1)
    def _():
        h2 = acc_ref[...]
        ms = jnp.mean(h2 * h2, axis=-1, keepdims=True)
        y_ref[...] = h2 * lax.rsqrt(ms + EPS) * gf_ref[...]

    @pl.when(g == steps_per_sample - 1)
    def _():
        _attend_finish(qt_ref, knt_ref, vnt_ref, biasnew_ref, o_ref, m_ref, l_ref, pacc_ref,
                       page=page)


def _ffn_attend(z, h, wu, wd, gf, page_table, qt, knt, vnt, bias, bias_new, ck_t, cv_t, *, tm, tf):
    m, d = h.shape
    dff = wu.shape[1]
    nb, n_pages = page_table.shape
    page = ck_t.shape[4]
    ni, nf = m // tm, dff // tf
    total_steps = ni * nf
    assert (nb * n_pages) % total_steps == 0
    pages_step = nb * n_pages // total_steps
    assert n_pages % pages_step == 0
    sps = n_pages // pages_step

    def sample(i, f):
        return (i * nf + f) // sps

    def page_spec(j):
        def index(i, f, pt):
            t = i * nf + f
            return (0, pt[t // sps, (t % sps) * pages_step + j], 0, 0, 0)

        return pl.BlockSpec((None, None, N_HEADS, HEAD_DIM, page), index)

    pages = [page_spec(j) for j in range(pages_step)]
    tq_block = pl.BlockSpec((None, HEAD_DIM, N_HEADS), lambda i, f, pt: (sample(i, f), 0, 0))
    grid_spec = pltpu.PrefetchScalarGridSpec(
        num_scalar_prefetch=1,
        grid=(ni, nf),
        in_specs=[
            pl.BlockSpec((tm, d), lambda i, f, pt: (i, 0)),
            pl.BlockSpec((tm, d), lambda i, f, pt: (i, 0), pipeline_mode=pl.Buffered(1)),
            pl.BlockSpec((d, tf), lambda i, f, pt: (0, f)),
            pl.BlockSpec((tf, d), lambda i, f, pt: (f, 0)),
            pl.BlockSpec((1, d), lambda i, f, pt: (0, 0)),
            tq_block, tq_block, tq_block,
            pl.BlockSpec((None, 1, pages_step * page),
                         lambda i, f, pt: (sample(i, f), 0, (i * nf + f) % sps)),
            pl.BlockSpec((None, 1, page), lambda i, f, pt: (sample(i, f), 0, 0)),
        ] + pages + pages,
        out_specs=[
            pl.BlockSpec((tm, d), lambda i, f, pt: (i, 0)),
            pl.BlockSpec((None, N_HEADS, HEAD_DIM, 1), lambda i, f, pt: (sample(i, f), 0, 0, 0)),
        ],
        scratch_shapes=[
            pltpu.VMEM((N_HEADS, 1, page), F32),
            pltpu.VMEM((N_HEADS, 1, page), F32),
            pltpu.VMEM((N_HEADS, HEAD_DIM, page), F32),
        ],
    )
    return pl.pallas_call(
        functools.partial(_ffn_attend_body, n_pages_step=pages_step, page=page,
                          steps_per_sample=sps),
        grid_spec=grid_spec,
        out_shape=[jax.ShapeDtypeStruct((m, d), F32),
                   jax.ShapeDtypeStruct((nb, N_HEADS, HEAD_DIM, 1), F32)],
        compiler_params=_params("arbitrary", "arbitrary"),
        name="ffn_with_paged_attend",
    )(page_table, z, h, wu, wd, gf, qt, knt, vnt, bias, bias_new,
      *([ck_t] * pages_step), *([cv_t] * pages_step))


def kernel(x_prompt, x_sample, cache_k, cache_v, cache_kidx, state_pool, page_table, norm_mix_g,
           w_in, b_gate, w_attn_out, w_pool_group, pool_scale, w_pool_out, w_out, norm_ffn_g,
           w_up, w_down, norm_final_g):
    depth = w_in.shape[0]
    assert depth == 1, "single-layer step"
    batch, seq, d_model = x_prompt.shape
    dec_batch, dec_seq, _ = x_sample.shape
    assert dec_seq == 1, "one new token per sample"
    n_pages = page_table.shape[1]
    page = cache_k.shape[2]
    past = n_pages * page
    pool_width = pool_scale.shape[1]
    score_pages = min(SCORE_PAGES, n_pages)
    assert page == LANES and n_pages % score_pages == 0
    l = 0

    tn = PROJ_COLS
    assert ATTN_WIDTH == tn and pool_width == tn and d_model == 2 * tn and IDX_HEADS * IDX_DIM == tn
    c_ki = 3 * ATTN_WIDTH + IDX_HEADS * IDX_DIM
    c_u = c_ki + IDX_DIM + IDX_HEADS
    c_ga = c_u + pool_width
    n_head_rows = c_ki + LANES
    wt = jnp.swapaxes(w_in[l], 0, 1)
    wt_head = jnp.concatenate([wt[:ATTN_WIDTH] * Q_SCALE, wt[ATTN_WIDTH:n_head_rows]],
                              axis=0).astype(BF16)
    wt_tail = jnp.concatenate([wt[c_ga:], wt[c_u:c_ga]], axis=0).astype(BF16)
    K_COL, V_COL, QI_COL = 1, 2, 3
    SIDE_BLOCK = c_ki // LANES
    GA_COL, GB_COL, U_COL = 0, 1, 4
    g_mix = norm_mix_g[l][None, :]
    wa = w_attn_out[l].astype(BF16)
    wg = w_pool_group[l].astype(BF16)
    wp = w_pool_out[l].astype(BF16)
    wo = w_out[l].astype(BF16)
    wu = w_up[l].astype(BF16)
    wd = w_down[l].astype(BF16)
    g_ffn = norm_ffn_g[l][None, :]
    g_fin = norm_final_g[None, :]
    scale = pool_scale[l][None, :]

    def col(p, c, width=tn):
        return p[:, c * tn:c * tn + width]

    xp = x_prompt.reshape(batch * seq, d_model)
    tm_p = PROJ_ROWS
    tm_wide = PROJ_ROWS_WIDE if seq % PROJ_ROWS_WIDE == 0 else PROJ_ROWS
    assert seq % tm_p == 0
    qqi_t = _proj_qqi(xp, g_mix, wt_head, batch=batch, seq=seq, tm=tm_wide,
                      row_blocks=(0, QI_COL))
    k_t, v_t, k_nat, vt_bf, side_t, ki_nat = _proj_kv(xp, g_mix, wt_head, batch=batch, seq=seq,
                                                      tm=tm_p, side_block=SIDE_BLOCK)
    pp = _proj_nat(xp, g_mix, wt_tail, tm_wide, tn)

    tq, kc = ATTN_QUERIES, ATTN_KEYS
    topk_p = min(TOPK_MAX, seq // 4)
    attn_p = _prompt_attention(qqi_t, side_t, ki_nat, k_nat, vt_bf, batch=batch, seq=seq, tq=tq,
                               kc=kc, topk=topk_p)
    pool_p = _pool_prompt(pp, wg, scale, batch=batch, seq=seq, tm=POOL_ROWS, u_col=U_COL)
    h_p, z_p = _merge(xp, attn_p, pool_p, pp, b_gate[l], wa, wp, wo, g_ffn, tm=MERGE_ROWS,
                      ga_col=GA_COL, gb_col=GB_COL)

    xs = x_sample.reshape(dec_batch, d_model)
    head_tile = max(t for t in (1024, 768, 512, 384, 256, 128) if n_head_rows % t == 0)
    ps_head = _proj_nat(xs, g_mix, wt_head, dec_batch, head_tile)
    ps = _proj_nat(xs, g_mix, wt_tail, dec_batch, tn)
    topk_s = min(TOPK_MAX, (past + dec_seq) // 4)
    q_s = col(ps_head, 0)
    k_s = col(ps_head, K_COL)
    v_s = col(ps_head, V_COL)
    ki_s = ps_head[:, c_ki:c_ki + IDX_DIM]
    wi_s = ps_head[:, c_ki + IDX_DIM:c_u]
    u_s = col(ps, U_COL)
    qi_s = col(ps_head, QI_COL).reshape(dec_batch, IDX_HEADS, IDX_DIM)
    kidx_t = jnp.swapaxes(cache_kidx, 2, 3)
    ck_t = jnp.transpose(cache_k, (0, 1, 3, 4, 2))
    cv_t = jnp.transpose(cache_v, (0, 1, 3, 4, 2))
    newpage = jnp.zeros((dec_batch, IDX_DIM, page), F32).at[:, :, 0].set(ki_s)
    scores, score_new = _sample_scores(page_table, qi_s, wi_s[:, :, None], newpage, kidx_t,
                                       pages_step=score_pages)
    bias, bias_new = _sample_select(scores.reshape(dec_batch, past),
                                    score_new.reshape(dec_batch, page), topk=topk_s)

    def dims_by_heads(t):
        return t.reshape(dec_batch, N_HEADS, HEAD_DIM).transpose(0, 2, 1)

    y_p, attn_s = _ffn_attend(z_p, h_p, wu, wd, g_fin, page_table, dims_by_heads(q_s),
                              dims_by_heads(k_s), dims_by_heads(v_s), bias[:, None, :],
                              bias_new[:, None, :], ck_t, cv_t, tm=FFN_ROWS, tf=FFN_COLS)
    pool_s = _pool_sample(state_pool[l], u_s, wg, scale)
    h_s, z_s = _merge(xs, attn_s.reshape(dec_batch, ATTN_WIDTH).astype(BF16), pool_s, ps,
                      b_gate[l], wa, wp, wo, g_ffn, tm=dec_batch, ga_col=GA_COL, gb_col=GB_COL)
    y_s = _ffn(z_s, h_s, wu, wd, g_fin, tm=dec_batch, tf=FFN_COLS)

    n_state = state_pool.shape[2]
    u_p = pp.reshape(batch, seq, -1)[:, -n_state:, U_COL * tn:(U_COL + 1) * tn]
    pool_state_s = jnp.concatenate([state_pool[l].astype(F32), u_s[:, None, :]], axis=1)[:, -n_state:]

    def heads_last(t):
        return t.reshape(batch, N_HEADS, HEAD_DIM, seq).transpose(0, 3, 1, 2)[None]

    return (
        y_p.reshape(batch, seq, d_model),
        y_s.reshape(dec_batch, dec_seq, d_model),
        heads_last(k_t),
        heads_last(v_t),
        side_t[:, :IDX_DIM, :].transpose(0, 2, 1)[None],
        u_p[None],
        k_s.reshape(1, dec_batch, dec_seq, N_HEADS, HEAD_DIM),
        v_s.reshape(1, dec_batch, dec_seq, N_HEADS, HEAD_DIM),
        ki_s.reshape(1, dec_batch, dec_seq, IDX_DIM),
        pool_state_s[None],
    )
```

```python
import functools

import jax
import jax.numpy as jnp
from jax import lax
from jax.experimental import pallas as pl
from jax.experimental.pallas import tpu as pltpu

F32 = jnp.float32
BF16 = jnp.bfloat16
I32 = jnp.int32

N_HEADS = 16
HEAD_DIM = 64
ATTN_WIDTH = N_HEADS * HEAD_DIM
IDX_HEADS = 16
IDX_DIM = 64
TOPK_MAX = 256
POOL_WINDOWS = (2, 4, 8, 16)
POOL_GROUP_WIDTH = 256
POOL_HALO = 16
EPS = 1e-6
INT_MIN = -(2 ** 31)
LANES = 128
NEG_BIG = -1e30
Q_SCALE = HEAD_DIM ** -0.5 * 1.4426950408889634
V_ROWS = HEAD_DIM + 16

VMEM_LIMIT_BYTES = 56 * 1024 * 1024

PROJ_ROWS = 512
PROJ_ROWS_WIDE = 1024
PROJ_COLS = 1024
ATTN_QUERIES = 256
ATTN_KEYS = 128
ATTN_PV_KEYS = 256
POOL_ROWS = 256
MERGE_ROWS = 256
FFN_ROWS = 512
FFN_COLS = 512
SCORE_PAGES = 64
SEARCH_FIXED_STEPS = 12
SEARCH_GROUP = 4
BISECT_MAX_STEPS = 512


def _params(*sem):
    return pltpu.CompilerParams(dimension_semantics=sem, vmem_limit_bytes=VMEM_LIMIT_BYTES)


_CONTRACT_LAST = (((1,), (1,)), ((), ()))


def _rms_to_bf16(x_ref, g_ref, z_ref):
    x = x_ref[...]
    ms = jnp.mean(x * x, axis=-1, keepdims=True)
    z_ref[...] = (x * lax.rsqrt(ms + EPS) * g_ref[...]).astype(BF16)


def _proj_nat_body(x_ref, g_ref, wt_ref, o_ref, z_ref):
    @pl.when(pl.program_id(1) == 0)
    def _():
        _rms_to_bf16(x_ref, g_ref, z_ref)

    o_ref[...] = lax.dot_general(z_ref[...], wt_ref[...], _CONTRACT_LAST,
                                 preferred_element_type=F32)


def _proj_nat(x, g, wt, tm, tn):
    m, d = x.shape
    n = wt.shape[0]
    return pl.pallas_call(
        _proj_nat_body,
        grid=(m // tm, n // tn),
        in_specs=[
            pl.BlockSpec((tm, d), lambda i, j: (i, 0)),
            pl.BlockSpec((1, d), lambda i, j: (0, 0)),
            pl.BlockSpec((tn, d), lambda i, j: (j, 0)),
        ],
        out_specs=pl.BlockSpec((tm, tn), lambda i, j: (i, j)),
        out_shape=jax.ShapeDtypeStruct((m, n), F32),
        scratch_shapes=[pltpu.VMEM((tm, d), BF16)],
        compiler_params=_params("parallel", "arbitrary"),
        name="rms_proj_rows",
    )(x, g, wt)


def _proj_qqi_body(x_ref, g_ref, wt_ref, o_ref, z_ref):
    @pl.when(pl.program_id(1) == 0)
    def _():
        _rms_to_bf16(x_ref, g_ref, z_ref)

    o_ref[...] = lax.dot_general(wt_ref[...], z_ref[...], _CONTRACT_LAST,
                                 preferred_element_type=F32).astype(BF16)


def _proj_qqi(x, g, wt_head, *, batch, seq, tm, row_blocks):
    d = x.shape[1]
    nblk = seq // tm
    blocks = tuple(row_blocks)
    assert blocks == (0, 3)
    return pl.pallas_call(
        _proj_qqi_body,
        grid=(batch * nblk, len(blocks)),
        in_specs=[
            pl.BlockSpec((tm, d), lambda i, j: (i, 0)),
            pl.BlockSpec((1, d), lambda i, j: (0, 0)),
            pl.BlockSpec((ATTN_WIDTH, d), lambda i, j: (3 * j, 0)),
        ],
        out_specs=pl.BlockSpec((None, None, ATTN_WIDTH, tm),
                               lambda i, j: (j, i // nblk, 0, i % nblk)),
        out_shape=jax.ShapeDtypeStruct((len(blocks), batch, ATTN_WIDTH, seq), BF16),
        scratch_shapes=[pltpu.VMEM((tm, d), BF16)],
        compiler_params=_params("parallel", "arbitrary"),
        name="rms_proj_q_qi",
    )(x, g, wt_head)


def _proj_kv_body(x_ref, g_ref, wt_ref, ws_ref, kt_ref, vt_ref, knat_ref, vtb_ref, sidet_ref,
                  kinat_ref, z_ref):
    j = pl.program_id(1)

    @pl.when(j == 0)
    def _():
        _rms_to_bf16(x_ref, g_ref, z_ref)
        side = lax.dot_general(ws_ref[...], z_ref[...], _CONTRACT_LAST,
                               preferred_element_type=F32)
        sidet_ref[...] = side
        kinat_ref[...] = side.T.astype(BF16)

    o = lax.dot_general(wt_ref[...], z_ref[...], _CONTRACT_LAST, preferred_element_type=F32)

    @pl.when(j == 0)
    def _():
        kt_ref[...] = o
        knat_ref[...] = o.T.astype(BF16)

    @pl.when(j == 1)
    def _():
        vt_ref[...] = o
        tm = o.shape[1]
        vtb_ref[:, 0:HEAD_DIM, :] = o.reshape(N_HEADS, HEAD_DIM, tm).astype(BF16)
        row = lax.broadcasted_iota(I32, (N_HEADS, V_ROWS - HEAD_DIM, tm), 1)
        vtb_ref[:, HEAD_DIM:V_ROWS, :] = jnp.where(row == 0, 1.0, 0.0).astype(BF16)


def _proj_kv(x, g, wt_head, *, batch, seq, tm, side_block):
    d = x.shape[1]
    nblk = seq // tm
    tok = batch * seq
    bi = lambda i, j: (i // nblk, 0, i % nblk)
    return pl.pallas_call(
        _proj_kv_body,
        grid=(batch * nblk, 2),
        in_specs=[
            pl.BlockSpec((tm, d), lambda i, j: (i, 0)),
            pl.BlockSpec((1, d), lambda i, j: (0, 0)),
            pl.BlockSpec((ATTN_WIDTH, d), lambda i, j: (1 + j, 0)),
            pl.BlockSpec((LANES, d), lambda i, j: (side_block, 0)),
        ],
        out_specs=[
            pl.BlockSpec((None, ATTN_WIDTH, tm), bi),
            pl.BlockSpec((None, ATTN_WIDTH, tm), bi),
            pl.BlockSpec((tm, ATTN_WIDTH), lambda i, j: (i, 0)),
            pl.BlockSpec((None, N_HEADS, V_ROWS, tm), lambda i, j: (i // nblk, 0, 0, i % nblk)),
            pl.BlockSpec((None, LANES, tm), bi),
            pl.BlockSpec((tm, LANES), lambda i, j: (i, 0)),
        ],
        out_shape=[
            jax.ShapeDtypeStruct((batch, ATTN_WIDTH, seq), F32),
            jax.ShapeDtypeStruct((batch, ATTN_WIDTH, seq), F32),
            jax.ShapeDtypeStruct((tok, ATTN_WIDTH), BF16),
            jax.ShapeDtypeStruct((batch, N_HEADS, V_ROWS, seq), BF16),
            jax.ShapeDtypeStruct((batch, LANES, seq), F32),
            jax.ShapeDtypeStruct((tok, LANES), BF16),
        ],
        scratch_shapes=[pltpu.VMEM((tm, d), BF16)],
        compiler_params=_params("parallel", "arbitrary"),
        name="rms_proj_k_v",
    )(x, g, wt_head, wt_head)


def _prompt_attn_body(qit_ref, side_ref, qt_ref, ki_ref, k_ref, vt_ref, *rest,
                      tq, kc, topk, n_cast):
    w_f32_refs = rest[:n_cast]
    o_ref = rest[n_cast]
    w_bf16_refs = rest[n_cast + 1:2 * n_cast + 1]
    sc_ref, thr_ref, qtp_ref, ot_ref, m_ref, l_ref, a_ref, s_ref, p_ref = rest[2 * n_cast + 1:]
    for src, dst in zip(w_f32_refs, w_bf16_refs):
        dst[...] = src[...].astype(BF16)
    i = pl.program_id(1)
    nchunk = (i + 1) * (tq // kc)
    q0 = i * tq

    w16 = side_ref[IDX_DIM:IDX_DIM + IDX_HEADS, :] * (IDX_HEADS ** -0.5 * IDX_DIM ** -0.5)
    qt = qt_ref[...].astype(F32).reshape(N_HEADS // 2, 2 * HEAD_DIM, tq)
    first_half = lax.broadcasted_iota(I32, (2 * HEAD_DIM, tq), 0) < HEAD_DIM
    for j in range(N_HEADS // 2):
        qtp_ref[2 * j] = jnp.where(first_half, qt[j], 0.0).astype(BF16)
        qtp_ref[2 * j + 1] = jnp.where(first_half, 0.0, qt[j]).astype(BF16)

    def idx_chunk(c, carry):
        lo8, hi8 = carry
        r0 = pl.multiple_of(c * kc, kc)
        kic = ki_ref[pl.ds(r0, kc), 0:IDX_DIM]
        acc = jnp.zeros((kc, tq), F32)
        for h in range(IDX_HEADS):
            s = jnp.dot(kic, qit_ref[h * IDX_DIM:(h + 1) * IDX_DIM, :],
                        preferred_element_type=F32)
            acc = acc + jnp.maximum(s, 0.0) * w16[h:h + 1, :]
        krow = r0 + lax.broadcasted_iota(I32, (kc, tq), 0)
        qcol = q0 + lax.broadcasted_iota(I32, (kc, tq), 1)
        causal = krow <= qcol
        sc_ref[pl.ds(r0, kc), :] = jnp.where(causal, acc, -jnp.inf)
        lo8 = jnp.minimum(lo8, jnp.min(jnp.where(causal, acc, jnp.inf).reshape(kc // 8, 8, tq),
                                       axis=0))
        hi8 = jnp.maximum(hi8, jnp.max(jnp.where(causal, acc, -jnp.inf).reshape(kc // 8, 8, tq),
                                       axis=0))
        return lo8, hi8

    lo8, hi8 = lax.fori_loop(0, nchunk, idx_chunk,
                             (jnp.full((8, tq), jnp.inf, F32), jnp.full((8, tq), -jnp.inf, F32)))
    lo0 = jnp.min(lo8, axis=0, keepdims=True)
    hi0 = jnp.max(hi8, axis=0, keepdims=True)

    def count_rows(pred):
        def body(c, cnt):
            r0 = pl.multiple_of(c * tq, tq)
            krow = r0 + lax.broadcasted_iota(I32, (tq, tq), 0)
            hit = pred(sc_ref[pl.ds(r0, tq), :], krow).astype(I32)
            return cnt + jnp.sum(hit.reshape(tq // 8, 8, tq), axis=0)

        cnt8 = lax.fori_loop(0, i + 1, body, jnp.zeros((8, tq), I32))
        return jnp.sum(cnt8, axis=0, keepdims=True)

    def count_ge(t):
        return count_rows(lambda s, _: s >= t)

    def bisect(_, state):
        lo, hi, n_lo = state
        mid = 0.5 * lo + 0.5 * hi
        cnt = count_ge(mid)
        take = cnt >= topk
        return jnp.where(take, mid, lo), jnp.where(take, hi, mid), jnp.where(take, cnt, n_lo)

    def unresolved(state):
        lo, hi, n_lo = state
        mid = 0.5 * lo + 0.5 * hi
        return (n_lo > topk) & (mid > lo) & (mid < hi)

    n_causal = jnp.minimum(q0 + lax.broadcasted_iota(I32, (1, tq), 1) + 1, (i + 1) * tq)
    state = lax.fori_loop(0, SEARCH_FIXED_STEPS, bisect, (lo0, hi0, n_causal))

    def more_steps(carry):
        it, st = carry
        return jnp.logical_and(it < BISECT_MAX_STEPS, jnp.max(unresolved(st).astype(I32)) > 0)

    def step_group(carry):
        it, st = carry
        return it + SEARCH_GROUP, lax.fori_loop(0, SEARCH_GROUP, bisect, st)

    _, (lo, hi, n_lo) = lax.while_loop(more_steps, step_group,
                                       (jnp.int32(SEARCH_FIXED_STEPS), state))
    thr_ref[...] = lo

    has_tie = jnp.max((n_lo > topk).astype(I32)) > 0

    @pl.when(has_tie)
    def _():
        n_hi = count_ge(hi)
        thr = jnp.where((n_lo > topk) & (n_hi >= topk), hi, lo)
        thr_ref[...] = thr
        need = topk - count_rows(lambda s, _: s > thr)

        def jbit(bi, jlo):
            cand = jlo | jnp.left_shift(jnp.int32(1), 29 - bi)
            below = count_rows(lambda s, krow: (s == thr) & (krow < cand))
            return jnp.where(below < need, cand, jlo)

        jstar = lax.fori_loop(0, 30, jbit, jnp.zeros((1, tq), I32))

        def drop(c, carry):
            r0 = pl.multiple_of(c * kc, kc)
            krow = r0 + lax.broadcasted_iota(I32, (kc, tq), 0)
            s = sc_ref[pl.ds(r0, kc), :]
            sc_ref[pl.ds(r0, kc), :] = jnp.where((s == thr) & (krow > jstar), -jnp.inf, s)
            return carry

        lax.fori_loop(0, nchunk, drop, 0)

    tsel = thr_ref[...]

    m_ref[...] = jnp.full(m_ref.shape, NEG_BIG, F32)
    l_ref[...] = jnp.zeros(l_ref.shape, F32)
    ot_ref[...] = jnp.zeros(ot_ref.shape, F32)

    ka = s_ref.shape[1]

    def chunk_body(c, carry):
        r0 = pl.multiple_of(c * ka, ka)
        bias = jnp.where(sc_ref[pl.ds(r0, ka), :] >= tsel, 0.0, -jnp.inf)
        for h in range(N_HEADS):
            pair = slice((h // 2) * 2 * HEAD_DIM, (h // 2 + 1) * 2 * HEAD_DIM)
            s_ref[h] = jnp.dot(k_ref[pl.ds(r0, ka), pair], qtp_ref[h],
                               preferred_element_type=F32) + bias
        for h in range(N_HEADS):
            s = s_ref[h]
            m_old = m_ref[h]
            m_new = jnp.maximum(m_old, jnp.max(s, axis=0, keepdims=True))
            p_ref[h] = jnp.exp2(s - m_new).astype(BF16)
            a_ref[h] = jnp.exp2(m_old - m_new)
            m_ref[h] = m_new
        for h in range(N_HEADS):
            pv = jnp.dot(vt_ref[h, :, pl.ds(r0, ka)], p_ref[h], preferred_element_type=F32)
            ot_ref[h] = a_ref[h] * ot_ref[h] + pv[0:HEAD_DIM]
            l_ref[h] = a_ref[h] * l_ref[h] + pv[HEAD_DIM:HEAD_DIM + 1]
        return carry

    lax.fori_loop(0, (i + 1) * (tq // ka), chunk_body, 0)
    o = ot_ref[...] / l_ref[...]
    o_ref[...] = o.reshape(ATTN_WIDTH, tq).T.astype(BF16)


def _prompt_attention(qqi_t, side_t, ki_nat, k_nat, vt, cast_weights, *, batch, seq, tq, kc, topk):
    nq = seq // tq
    steps = batch * nq
    body = functools.partial(_prompt_attn_body, tq=tq, kc=kc, topk=topk,
                             n_cast=len(cast_weights))
    once = pl.Buffered(1)

    def slab_spec(w):
        rows = w.shape[0] // steps
        assert w.shape[0] % steps == 0 and rows % 16 == 0
        return pl.BlockSpec((rows, w.shape[1]), lambda b, i: (b * nq + i, 0))

    slabs = [slab_spec(w) for w in cast_weights]
    return pl.pallas_call(
        body,
        grid=(batch, nq),
        in_specs=[
            pl.BlockSpec((None, None, ATTN_WIDTH, tq), lambda b, i: (1, b, 0, i)),
            pl.BlockSpec((None, LANES, tq), lambda b, i: (b, 0, i)),
            pl.BlockSpec((None, None, ATTN_WIDTH, tq), lambda b, i: (0, b, 0, i)),
            pl.BlockSpec((seq, LANES), lambda b, i: (b, 0), pipeline_mode=once),
            pl.BlockSpec((seq, ATTN_WIDTH), lambda b, i: (b, 0), pipeline_mode=once),
            pl.BlockSpec((None, N_HEADS, V_ROWS, seq), lambda b, i: (b, 0, 0, 0),
                         pipeline_mode=once),
        ] + slabs,
        out_specs=[pl.BlockSpec((tq, ATTN_WIDTH), lambda b, i: (b * nq + i, 0))] + slabs,
        out_shape=[jax.ShapeDtypeStruct((batch * seq, ATTN_WIDTH), BF16)]
        + [jax.ShapeDtypeStruct(w.shape, BF16) for w in cast_weights],
        scratch_shapes=[
            pltpu.VMEM((seq, tq), F32),
            pltpu.VMEM((1, tq), F32),
            pltpu.VMEM((N_HEADS, 2 * HEAD_DIM, tq), BF16),
            pltpu.VMEM((N_HEADS, HEAD_DIM, tq), F32),
            pltpu.VMEM((N_HEADS, 1, tq), F32),
            pltpu.VMEM((N_HEADS, 1, tq), F32),
            pltpu.VMEM((N_HEADS, 1, tq), F32),
            pltpu.VMEM((N_HEADS, ATTN_PV_KEYS, tq), F32),
            pltpu.VMEM((N_HEADS, ATTN_PV_KEYS, tq), BF16),
        ],
        compiler_params=_params("parallel", "arbitrary"),
        name="prompt_sparse_attn",
    )(qqi_t, side_t, qqi_t, ki_nat, k_nat, vt, *cast_weights)


def _pool_prompt_body(u_ref, wg_ref, sc_ref, o_ref, ue_ref, *, tm):
    i = pl.program_id(1)

    @pl.when(i == 0)
    def _():
        ue_ref[0:POOL_HALO, :] = jnp.zeros((POOL_HALO, ue_ref.shape[1]), F32)

    u = u_ref[...]
    ue_ref[POOL_HALO:POOL_HALO + tm, :] = u
    pos = i * tm + lax.broadcasted_iota(I32, (tm, 1), 0)
    outs = []
    for g, w in enumerate(POOL_WINDOWS):
        cols = slice(g * POOL_GROUP_WIDTH, (g + 1) * POOL_GROUP_WIDTH)
        win = u[:, cols]
        for back in range(1, w):
            win = win + ue_ref[POOL_HALO - back:POOL_HALO - back + tm, cols]
        cnt = jnp.minimum(pos + 1, w).astype(F32)
        mixed = win / cnt - u[:, cols]
        outs.append(jnp.dot(mixed.astype(BF16), wg_ref[g], preferred_element_type=F32))
    o_ref[...] = (jnp.concatenate(outs, axis=1) * sc_ref[...]).astype(BF16)
    ue_ref[0:POOL_HALO, :] = u[tm - POOL_HALO:, :]


def _pool_prompt(p_main, wg, scale, *, batch, seq, tm, u_col):
    nblk = seq // tm
    width = wg.shape[0] * wg.shape[1]
    return pl.pallas_call(
        functools.partial(_pool_prompt_body, tm=tm),
        grid=(batch, nblk),
        in_specs=[
            pl.BlockSpec((tm, width), lambda b, i: (b * nblk + i, u_col)),
            pl.BlockSpec(wg.shape, lambda b, i: (0, 0, 0)),
            pl.BlockSpec((1, width), lambda b, i: (0, 0)),
        ],
        out_specs=pl.BlockSpec((tm, width), lambda b, i: (b * nblk + i, 0)),
        out_shape=jax.ShapeDtypeStruct((batch * seq, width), BF16),
        scratch_shapes=[pltpu.VMEM((POOL_HALO + tm, width), F32)],
        compiler_params=_params("parallel", "arbitrary"),
        name="pool_prompt",
    )(p_main, wg, scale)


def _pool_sample_body(st_ref, u_ref, wg_ref, sc_ref, o_ref):
    st = st_ref[...]
    u = u_ref[...]
    n_state = st.shape[1]
    outs = []
    for g, w in enumerate(POOL_WINDOWS):
        cols = slice(g * POOL_GROUP_WIDTH, (g + 1) * POOL_GROUP_WIDTH)
        win = u[:, cols] + jnp.sum(st[:, n_state - (w - 1):, cols], axis=1)
        mixed = win / float(w) - u[:, cols]
        outs.append(jnp.dot(mixed.astype(BF16), wg_ref[g], preferred_element_type=F32))
    o_ref[...] = (jnp.concatenate(outs, axis=1) * sc_ref[...]).astype(BF16)


def _pool_sample(state, u_new, wg, scale):
    nb, width = u_new.shape
    return pl.pallas_call(
        _pool_sample_body,
        out_shape=jax.ShapeDtypeStruct((nb, width), BF16),
        compiler_params=pltpu.CompilerParams(vmem_limit_bytes=VMEM_LIMIT_BYTES),
        name="pool_sample",
    )(state, u_new, wg, scale)


def _merge_body(x_ref, ao_ref, po_ref, ga_ref, gb_ref, bg_ref, wa_ref, wp_ref, wo_ref, g2_ref,
                h_ref, z_ref):
    ya = jnp.dot(ao_ref[...], wa_ref[...], preferred_element_type=F32)
    yb = jnp.dot(po_ref[...], wp_ref[...], preferred_element_type=F32)
    merged = (jax.nn.sigmoid(ga_ref[...] + bg_ref[0:1, :]) * ya
              + jax.nn.sigmoid(gb_ref[...] + bg_ref[1:2, :]) * yb)
    h = x_ref[...] + jnp.dot(merged.astype(BF16), wo_ref[...], preferred_element_type=F32)
    h_ref[...] = h
    ms = jnp.mean(h * h, axis=-1, keepdims=True)
    z_ref[...] = (h * lax.rsqrt(ms + EPS) * g2_ref[...]).astype(BF16)


def _merge(x, attn_o, pool_o, p_main, b_gate, wa, wp, wo, g2, *, tm, ga_col, gb_col):
    m, d = x.shape
    const = lambda i: (0, 0)
    return pl.pallas_call(
        _merge_body,
        grid=(m // tm,),
        in_specs=[
            pl.BlockSpec((tm, d), lambda i: (i, 0)),
            pl.BlockSpec((tm, attn_o.shape[1]), lambda i: (i, 0)),
            pl.BlockSpec((tm, pool_o.shape[1]), lambda i: (i, 0)),
            pl.BlockSpec((tm, d), lambda i: (i, ga_col)),
            pl.BlockSpec((tm, d), lambda i: (i, gb_col)),
            pl.BlockSpec(b_gate.shape, const),
            pl.BlockSpec(wa.shape, const),
            pl.BlockSpec(wp.shape, const),
            pl.BlockSpec(wo.shape, const),
            pl.BlockSpec((1, d), const),
        ],
        out_specs=[pl.BlockSpec((tm, d), lambda i: (i, 0)), pl.BlockSpec((tm, d), lambda i: (i, 0))],
        out_shape=[jax.ShapeDtypeStruct((m, d), F32), jax.ShapeDtypeStruct((m, d), BF16)],
        compiler_params=_params("parallel"),
        name="gated_merge",
    )(x, attn_o, pool_o, p_main, p_main, b_gate, wa, wp, wo, g2)


def _ffn_body(z_ref, h_ref, wu_ref, wd_ref, gf_ref, y_ref, acc_ref):
    f = pl.program_id(1)

    @pl.when(f == 0)
    def _():
        acc_ref[...] = h_ref[...]

    a = jnp.dot(z_ref[...], wu_ref[...], preferred_element_type=F32)
    a = jnp.square(jnp.maximum(a, 0.0))
    acc_ref[...] += jnp.dot(a.astype(BF16), wd_ref[...], preferred_element_type=F32)

    @pl.when(f == pl.num_programs(1) - 1)
    def _():
        h2 = acc_ref[...]
        ms = jnp.mean(h2 * h2, axis=-1, keepdims=True)
        y_ref[...] = h2 * lax.rsqrt(ms + EPS) * gf_ref[...]


def _ffn(z, h, wu, wd, gf, *, tm, tf):
    m, d = h.shape
    dff = wu.shape[1]
    return pl.pallas_call(
        _ffn_body,
        grid=(m // tm, dff // tf),
        in_specs=[
            pl.BlockSpec((tm, d), lambda i, f: (i, 0)),
            pl.BlockSpec((tm, d), lambda i, f: (i, 0)),
            pl.BlockSpec((d, tf), lambda i, f: (0, f)),
            pl.BlockSpec((tf, d), lambda i, f: (f, 0)),
            pl.BlockSpec((1, d), lambda i, f: (0, 0)),
        ],
        out_specs=pl.BlockSpec((tm, d), lambda i, f: (i, 0)),
        out_shape=jax.ShapeDtypeStruct((m, d), F32),
        scratch_shapes=[pltpu.VMEM((tm, d), F32)],
        compiler_params=_params("parallel", "arbitrary"),
        name="ffn_final_norm",
    )(z, h, wu, wd, gf)


def _sample_scores_body(pt_ref, qi_ref, w_ref, newpage_ref, *rest, n_pages_step):
    page_refs = rest[:n_pages_step]
    o_ref, onew_ref = rest[n_pages_step:]
    qi = qi_ref[...].astype(BF16)
    wcol = w_ref[...] * (IDX_HEADS ** -0.5 * IDX_DIM ** -0.5)

    def score(pages_t):
        s = jnp.dot(qi, pages_t.astype(BF16), preferred_element_type=F32)
        return jnp.sum(jnp.maximum(s, 0.0) * wcol, axis=0, keepdims=True)

    o_ref[...] = score(jnp.concatenate([r[...] for r in page_refs], axis=1))
    onew_ref[...] = score(newpage_ref[...])


def _sample_scores(page_table, qi, w, newpage, kidx_t, *, pages_step):
    nb, n_pages = page_table.shape
    page = kidx_t.shape[3]
    steps = n_pages // pages_step

    def page_spec(j):
        return pl.BlockSpec((None, None, IDX_DIM, page),
                            lambda b, g, pt: (0, pt[b, g * pages_step + j], 0, 0))

    grid_spec = pltpu.PrefetchScalarGridSpec(
        num_scalar_prefetch=1,
        grid=(nb, steps),
        in_specs=[
            pl.BlockSpec((None, IDX_HEADS, IDX_DIM), lambda b, g, pt: (b, 0, 0)),
            pl.BlockSpec((None, IDX_HEADS, 1), lambda b, g, pt: (b, 0, 0)),
            pl.BlockSpec((None, IDX_DIM, page), lambda b, g, pt: (b, 0, 0)),
        ] + [page_spec(j) for j in range(pages_step)],
        out_specs=[
            pl.BlockSpec((None, 1, pages_step * page), lambda b, g, pt: (b, 0, g)),
            pl.BlockSpec((None, 1, page), lambda b, g, pt: (b, 0, 0)),
        ],
    )
    return pl.pallas_call(
        functools.partial(_sample_scores_body, n_pages_step=pages_step),
        grid_spec=grid_spec,
        out_shape=[jax.ShapeDtypeStruct((nb, 1, n_pages * page), F32),
                   jax.ShapeDtypeStruct((nb, 1, page), F32)],
        compiler_params=_params("parallel", "arbitrary"),
        name="sample_indexer_scores",
    )(page_table, qi, w, newpage, *([kidx_t] * pages_step))


def _prefix_exclusive(mask_f32, tri):
    rows, length = mask_f32.shape
    off = jnp.zeros((rows, 1), F32)
    parts = []
    for c in range(length // LANES):
        mc = mask_f32[:, c * LANES:(c + 1) * LANES]
        inc = jnp.dot(mc.astype(BF16), tri, preferred_element_type=F32)
        parts.append(inc - mc + off)
        off = off + inc[:, LANES - 1:LANES]
    return jnp.concatenate(parts, axis=1)


def _sample_select_body(sc_ref, scnew_ref, bias_ref, biasnew_ref, *, topk):
    past = sc_ref.shape[1]
    sc = jnp.concatenate([sc_ref[...], scnew_ref[...]], axis=1)
    nb, length = sc.shape
    pos = lax.broadcasted_iota(I32, (nb, length), 1)
    valid = pos <= past
    scv = jnp.where(valid, sc, -jnp.inf)

    def count_ge(t):
        return jnp.sum((scv >= t).astype(I32), axis=1, keepdims=True)

    lo0 = jnp.min(jnp.where(valid, sc, jnp.inf), axis=1, keepdims=True)
    hi0 = jnp.max(scv, axis=1, keepdims=True)

    def unresolved(lo, hi, n_lo):
        mid = 0.5 * lo + 0.5 * hi
        return (n_lo > topk) & (mid > lo) & (mid < hi)

    def more(carry):
        it, lo, hi, n_lo = carry
        return jnp.logical_and(it < BISECT_MAX_STEPS,
                               jnp.max(unresolved(lo, hi, n_lo).astype(I32)) > 0)

    def step(carry):
        it, lo, hi, n_lo = carry
        mid = 0.5 * lo + 0.5 * hi
        cnt = count_ge(mid)
        take = cnt >= topk
        return it + 1, jnp.where(take, mid, lo), jnp.where(take, hi, mid), jnp.where(take, cnt, n_lo)

    _, lo, hi, n_lo = lax.while_loop(more, step, (jnp.int32(0), lo0, hi0, count_ge(lo0)))
    thr = jnp.where(count_ge(hi) >= topk, hi, lo)
    gt = scv > thr
    eq = (scv == thr) & valid
    tri = (lax.broadcasted_iota(I32, (LANES, LANES), 0)
           <= lax.broadcasted_iota(I32, (LANES, LANES), 1)).astype(BF16)
    need = (topk - jnp.sum(gt.astype(I32), axis=1, keepdims=True)).astype(F32)
    eq_rank = _prefix_exclusive(eq.astype(F32), tri)
    sel = (gt | (eq & (eq_rank < need))) & valid
    bias = jnp.where(sel, 0.0, -jnp.inf)
    bias_ref[...] = bias[:, :past]
    biasnew_ref[...] = bias[:, past:]


def _sample_select(scores, score_new, *, topk):
    nb, past = scores.shape
    return pl.pallas_call(
        functools.partial(_sample_select_body, topk=topk),
        out_shape=[jax.ShapeDtypeStruct((nb, past), F32),
                   jax.ShapeDtypeStruct(score_new.shape, F32)],
        compiler_params=pltpu.CompilerParams(vmem_limit_bytes=VMEM_LIMIT_BYTES),
        name="sample_topk_select",
    )(scores, score_new)


def _heads_on_lanes_to_slabs(x_t, page):
    return jnp.stack([jnp.broadcast_to(x_t[:, h:h + 1], (HEAD_DIM, page))
                      for h in range(N_HEADS)])


def _lane_softmax_update(scores, values, m_ref, l_ref, acc_ref):
    m_old = m_ref[...]
    m_new = m_old
    for s in scores:
        m_new = jnp.maximum(m_new, s)
    alpha = jnp.exp2(m_old - m_new)
    l = alpha * l_ref[...]
    acc = alpha * acc_ref[...]
    for s, v in zip(scores, values):
        p = jnp.exp2(s - m_new)
        l = l + p
        acc = acc + p * v
    m_ref[...] = m_new
    l_ref[...] = l
    acc_ref[...] = acc


def _attend_pages(qt_ref, bias_ref, k_refs, v_refs, m_ref, l_ref, acc_ref, *, page):
    q = _heads_on_lanes_to_slabs(qt_ref[...], page)
    scores = []
    for j, k_ref in enumerate(k_refs):
        s = jnp.sum(k_ref[...] * q, axis=1, keepdims=True)
        scores.append(s + bias_ref[:, j * page:(j + 1) * page][None])
    _lane_softmax_update(scores, [v_ref[...] for v_ref in v_refs], m_ref, l_ref, acc_ref)


def _attend_finish(qt_ref, knt_ref, vnt_ref, biasnew_ref, o_ref, m_ref, l_ref, acc_ref, *, page):
    q = _heads_on_lanes_to_slabs(qt_ref[...], page)
    kn = _heads_on_lanes_to_slabs(knt_ref[...], page)
    s_new = jnp.sum(kn * q, axis=1, keepdims=True) + biasnew_ref[...][None]
    _lane_softmax_update([s_new], [_heads_on_lanes_to_slabs(vnt_ref[...], page)],
                         m_ref, l_ref, acc_ref)
    m = m_ref[...]
    f = jnp.exp2(m - jnp.max(m, axis=-1, keepdims=True))
    denom = jnp.sum(l_ref[...] * f, axis=-1, keepdims=True)
    o_ref[...] = jnp.sum(acc_ref[...] * f, axis=-1, keepdims=True) / denom


def _ffn_attend_body(pt_ref, z_ref, h_ref, wu_ref, wd_ref, gf_ref,
                     qt_ref, knt_ref, vnt_ref, bias_ref, biasnew_ref, *rest,
                     n_pages_step, page, steps_per_sample):
    k_refs = rest[:n_pages_step]
    v_refs = rest[n_pages_step:2 * n_pages_step]
    y_ref, o_ref, m_ref, l_ref, pacc_ref = rest[2 * n_pages_step:]
    acc_ref = y_ref
    f = pl.program_id(1)
    g = (pl.program_id(0) * pl.num_programs(1) + f) % steps_per_sample

    @pl.when(f == 0)
    def _():
        acc_ref[...] = h_ref[...]

    @pl.when(g == 0)
    def _():
        m_ref[...] = jnp.full(m_ref.shape, NEG_BIG, F32)
        l_ref[...] = jnp.zeros(l_ref.shape, F32)
        pacc_ref[...] = jnp.zeros(pacc_ref.shape, F32)

    a = jnp.dot(z_ref[...], wu_ref[...], preferred_element_type=F32)
    a = jnp.square(jnp.maximum(a, 0.0))
    acc_ref[...] += jnp.dot(a.astype(BF16), wd_ref[...], preferred_element_type=F32)
    _attend_pages(qt_ref, bias_ref, k_refs, v_refs, m_ref, l_ref, pacc_ref, page=page)

    @pl.when(f == pl.num_programs(1) - 1)
    def _():
        h2 = acc_ref[...]
        ms = jnp.mean(h2 * h2, axis=-1, keepdims=True)
        y_ref[...] = h2 * lax.rsqrt(ms + EPS) * gf_ref[...]

    @pl.when(g == steps_per_sample - 1)
    def _():
        _attend_finish(qt_ref, knt_ref, vnt_ref, biasnew_ref, o_ref, m_ref, l_ref, pacc_ref,
                       page=page)


def _ffn_attend(z, h, wu, wd, gf, page_table, qt, knt, vnt, bias, bias_new, ck_t, cv_t, *, tm, tf):
    m, d = h.shape
    dff = wu.shape[1]
    nb, n_pages = page_table.shape
    page = ck_t.shape[4]
    ni, nf = m // tm, dff // tf
    total_steps = ni * nf
    assert (nb * n_pages) % total_steps == 0
    pages_step = nb * n_pages // total_steps
    assert n_pages % pages_step == 0
    sps = n_pages // pages_step

    def sample(i, f):
        return (i * nf + f) // sps

    def page_spec(j):
        def index(i, f, pt):
            t = i * nf + f
            return (0, pt[t // sps, (t % sps) * pages_step + j], 0, 0, 0)

        return pl.BlockSpec((None, None, N_HEADS, HEAD_DIM, page), index)

    pages = [page_spec(j) for j in range(pages_step)]
    tq_block = pl.BlockSpec((None, HEAD_DIM, N_HEADS), lambda i, f, pt: (sample(i, f), 0, 0))
    grid_spec = pltpu.PrefetchScalarGridSpec(
        num_scalar_prefetch=1,
        grid=(ni, nf),
        in_specs=[
            pl.BlockSpec((tm, d), lambda i, f, pt: (i, 0)),
            pl.BlockSpec((tm, d), lambda i, f, pt: (i, 0), pipeline_mode=pl.Buffered(1)),
            pl.BlockSpec((d, tf), lambda i, f, pt: (0, f)),
            pl.BlockSpec((tf, d), lambda i, f, pt: (f, 0)),
            pl.BlockSpec((1, d), lambda i, f, pt: (0, 0)),
            tq_block, tq_block, tq_block,
            pl.BlockSpec((None, 1, pages_step * page),
                         lambda i, f, pt: (sample(i, f), 0, (i * nf + f) % sps)),
            pl.BlockSpec((None, 1, page), lambda i, f, pt: (sample(i, f), 0, 0)),
        ] + pages + pages,
        out_specs=[
            pl.BlockSpec((tm, d), lambda i, f, pt: (i, 0)),
            pl.BlockSpec((None, N_HEADS, HEAD_DIM, 1), lambda i, f, pt: (sample(i, f), 0, 0, 0)),
        ],
        scratch_shapes=[
            pltpu.VMEM((N_HEADS, 1, page), F32),
            pltpu.VMEM((N_HEADS, 1, page), F32),
            pltpu.VMEM((N_HEADS, HEAD_DIM, page), F32),
        ],
    )
    return pl.pallas_call(
        functools.partial(_ffn_attend_body, n_pages_step=pages_step, page=page,
                          steps_per_sample=sps),
        grid_spec=grid_spec,
        out_shape=[jax.ShapeDtypeStruct((m, d), F32),
                   jax.ShapeDtypeStruct((nb, N_HEADS, HEAD_DIM, 1), F32)],
        compiler_params=_params("arbitrary", "arbitrary"),
        name="ffn_with_paged_attend",
    )(page_table, z, h, wu, wd, gf, qt, knt, vnt, bias, bias_new,
      *([ck_t] * pages_step), *([cv_t] * pages_step))


def kernel(x_prompt, x_sample, cache_k, cache_v, cache_kidx, state_pool, page_table, norm_mix_g,
           w_in, b_gate, w_attn_out, w_pool_group, pool_scale, w_pool_out, w_out, norm_ffn_g,
           w_up, w_down, norm_final_g):
    depth = w_in.shape[0]
    assert depth == 1, "single-layer step"
    batch, seq, d_model = x_prompt.shape
    dec_batch, dec_seq, _ = x_sample.shape
    assert dec_seq == 1, "one new token per sample"
    n_pages = page_table.shape[1]
    page = cache_k.shape[2]
    past = n_pages * page
    pool_width = pool_scale.shape[1]
    score_pages = min(SCORE_PAGES, n_pages)
    assert page == LANES and n_pages % score_pages == 0
    l = 0

    tn = PROJ_COLS
    assert ATTN_WIDTH == tn and pool_width == tn and d_model == 2 * tn and IDX_HEADS * IDX_DIM == tn
    c_ki = 3 * ATTN_WIDTH + IDX_HEADS * IDX_DIM
    c_u = c_ki + IDX_DIM + IDX_HEADS
    c_ga = c_u + pool_width
    n_head_rows = c_ki + LANES
    wt = jnp.swapaxes(w_in[l], 0, 1)
    wt_head = jnp.concatenate([wt[:ATTN_WIDTH] * Q_SCALE, wt[ATTN_WIDTH:n_head_rows]],
                              axis=0).astype(BF16)
    wt_tail = jnp.concatenate([wt[c_ga:], wt[c_u:c_ga]], axis=0).astype(BF16)
    K_COL, V_COL, QI_COL = 1, 2, 3
    SIDE_BLOCK = c_ki // LANES
    GA_COL, GB_COL, U_COL = 0, 1, 4
    g_mix = norm_mix_g[l][None, :]
    wg = w_pool_group[l].astype(BF16)
    g_ffn = norm_ffn_g[l][None, :]
    g_fin = norm_final_g[None, :]
    scale = pool_scale[l][None, :]

    def col(p, c, width=tn):
        return p[:, c * tn:c * tn + width]

    xp = x_prompt.reshape(batch * seq, d_model)
    tm_p = PROJ_ROWS
    tm_wide = PROJ_ROWS_WIDE if seq % PROJ_ROWS_WIDE == 0 else PROJ_ROWS
    assert seq % tm_p == 0
    qqi_t = _proj_qqi(xp, g_mix, wt_head, batch=batch, seq=seq, tm=tm_wide,
                      row_blocks=(0, QI_COL))
    k_t, v_t, k_nat, vt_bf, side_t, ki_nat = _proj_kv(xp, g_mix, wt_head, batch=batch, seq=seq,
                                                      tm=tm_p, side_block=SIDE_BLOCK)
    pp = _proj_nat(xp, g_mix, wt_tail, tm_wide, tn)

    tq, kc = ATTN_QUERIES, ATTN_KEYS
    topk_p = min(TOPK_MAX, seq // 4)
    attn_p, wa, wp, wo, wu, wd = _prompt_attention(
        qqi_t, side_t, ki_nat, k_nat, vt_bf,
        [w_attn_out[l], w_pool_out[l], w_out[l], w_up[l], w_down[l]],
        batch=batch, seq=seq, tq=tq, kc=kc, topk=topk_p)
    pool_p = _pool_prompt(pp, wg, scale, batch=batch, seq=seq, tm=POOL_ROWS, u_col=U_COL)
    h_p, z_p = _merge(xp, attn_p, pool_p, pp, b_gate[l], wa, wp, wo, g_ffn, tm=MERGE_ROWS,
                      ga_col=GA_COL, gb_col=GB_COL)

    xs = x_sample.reshape(dec_batch, d_model)
    head_tile = max(t for t in (1024, 768, 512, 384, 256, 128) if n_head_rows % t == 0)
    ps_head = _proj_nat(xs, g_mix, wt_head, dec_batch, head_tile)
    ps = _proj_nat(xs, g_mix, wt_tail, dec_batch, tn)
    topk_s = min(TOPK_MAX, (past + dec_seq) // 4)
    q_s = col(ps_head, 0)
    k_s = col(ps_head, K_COL)
    v_s = col(ps_head, V_COL)
    ki_s = ps_head[:, c_ki:c_ki + IDX_DIM]
    wi_s = ps_head[:, c_ki + IDX_DIM:c_u]
    u_s = col(ps, U_COL)
    qi_s = col(ps_head, QI_COL).reshape(dec_batch, IDX_HEADS, IDX_DIM)
    kidx_t = jnp.swapaxes(cache_kidx, 2, 3)
    ck_t = jnp.transpose(cache_k, (0, 1, 3, 4, 2))
    cv_t = jnp.transpose(cache_v, (0, 1, 3, 4, 2))
    newpage = jnp.zeros((dec_batch, IDX_DIM, page), F32).at[:, :, 0].set(ki_s)
    scores, score_new = _sample_scores(page_table, qi_s, wi_s[:, :, None], newpage, kidx_t,
                                       pages_step=score_pages)
    bias, bias_new = _sample_select(scores.reshape(dec_batch, past),
                                    score_new.reshape(dec_batch, page), topk=topk_s)

    def dims_by_heads(t):
        return t.reshape(dec_batch, N_HEADS, HEAD_DIM).transpose(0, 2, 1)

    y_p, attn_s = _ffn_attend(z_p, h_p, wu, wd, g_fin, page_table, dims_by_heads(q_s),
                              dims_by_heads(k_s), dims_by_heads(v_s), bias[:, None, :],
                              bias_new[:, None, :], ck_t, cv_t, tm=FFN_ROWS, tf=FFN_COLS)
    pool_s = _pool_sample(state_pool[l], u_s, wg, scale)
    h_s, z_s = _merge(xs, attn_s.reshape(dec_batch, ATTN_WIDTH).astype(BF16), pool_s, ps,
                      b_gate[l], wa, wp, wo, g_ffn, tm=dec_batch, ga_col=GA_COL, gb_col=GB_COL)
    y_s = _ffn(z_s, h_s, wu, wd, g_fin, tm=dec_batch, tf=FFN_COLS)

    n_state = state_pool.shape[2]
    u_p = pp.reshape(batch, seq, -1)[:, -n_state:, U_COL * tn:(U_COL + 1) * tn]
    pool_state_s = jnp.concatenate([state_pool[l].astype(F32), u_s[:, None, :]], axis=1)[:, -n_state:]

    def heads_last(t):
        return t.reshape(batch, N_HEADS, HEAD_DIM, seq).transpose(0, 3, 1, 2)[None]

    return (
        y_p.reshape(batch, seq, d_model),
        y_s.reshape(dec_batch, dec_seq, d_model),
        heads_last(k_t),
        heads_last(v_t),
        side_t[:, :IDX_DIM, :].transpose(0, 2, 1)[None],
        u_p[None],
        k_s.reshape(1, dec_batch, dec_seq, N_HEADS, HEAD_DIM),
        v_s.reshape(1, dec_batch, dec_seq, N_HEADS, HEAD_DIM),
        ki_s.reshape(1, dec_batch, dec_seq, IDX_DIM),
        pool_state_s[None],
    )
```

```python
import functools

import jax
import jax.numpy as jnp
from jax import lax
from jax.experimental import pallas as pl
from jax.experimental.pallas import tpu as pltpu

F32 = jnp.float32
BF16 = jnp.bfloat16
I32 = jnp.int32

N_HEADS = 16
HEAD_DIM = 64
ATTN_WIDTH = N_HEADS * HEAD_DIM
IDX_HEADS = 16
IDX_DIM = 64
TOPK_MAX = 256
POOL_WINDOWS = (2, 4, 8, 16)
POOL_GROUP_WIDTH = 256
POOL_HALO = 16
EPS = 1e-6
INT_MIN = -(2 ** 31)
LANES = 128
NEG_BIG = -1e30
Q_SCALE = HEAD_DIM ** -0.5 * 1.4426950408889634
V_ROWS = HEAD_DIM + 16

VMEM_LIMIT_BYTES = 56 * 1024 * 1024

PROJ_ROWS = 512
PROJ_ROWS_WIDE = 1024
PROJ_COLS = 1024
ATTN_QUERIES = 256
ATTN_KEYS = 128
ATTN_PV_KEYS = 256
MERGE_ROWS = 256
FFN_ROWS = 512
FFN_COLS = 512
SCORE_PAGES = 64
SEARCH_FIXED_STEPS = 12
SEARCH_GROUP = 4
BISECT_MAX_STEPS = 512


def _params(*sem):
    return pltpu.CompilerParams(dimension_semantics=sem, vmem_limit_bytes=VMEM_LIMIT_BYTES)


_CONTRACT_LAST = (((1,), (1,)), ((), ()))


def _rms_to_bf16(x_ref, g_ref, z_ref):
    x = x_ref[...]
    ms = jnp.mean(x * x, axis=-1, keepdims=True)
    z_ref[...] = (x * lax.rsqrt(ms + EPS) * g_ref[...]).astype(BF16)


def _proj_nat_body(x_ref, g_ref, wt_ref, o_ref, z_ref):
    @pl.when(pl.program_id(1) == 0)
    def _():
        _rms_to_bf16(x_ref, g_ref, z_ref)

    o_ref[...] = lax.dot_general(z_ref[...], wt_ref[...], _CONTRACT_LAST,
                                 preferred_element_type=F32)


def _proj_nat(x, g, wt, tm, tn):
    m, d = x.shape
    n = wt.shape[0]
    return pl.pallas_call(
        _proj_nat_body,
        grid=(m // tm, n // tn),
        in_specs=[
            pl.BlockSpec((tm, d), lambda i, j: (i, 0)),
            pl.BlockSpec((1, d), lambda i, j: (0, 0)),
            pl.BlockSpec((tn, d), lambda i, j: (j, 0)),
        ],
        out_specs=pl.BlockSpec((tm, tn), lambda i, j: (i, j)),
        out_shape=jax.ShapeDtypeStruct((m, n), F32),
        scratch_shapes=[pltpu.VMEM((tm, d), BF16)],
        compiler_params=_params("parallel", "arbitrary"),
        name="rms_proj_rows",
    )(x, g, wt)


def _proj_qqi_body(x_ref, g_ref, wt_ref, o_ref, z_ref):
    @pl.when(pl.program_id(1) == 0)
    def _():
        _rms_to_bf16(x_ref, g_ref, z_ref)

    o_ref[...] = lax.dot_general(wt_ref[...], z_ref[...], _CONTRACT_LAST,
                                 preferred_element_type=F32).astype(BF16)


def _proj_qqi(x, g, wt_head, *, batch, seq, tm, row_blocks):
    d = x.shape[1]
    nblk = seq // tm
    blocks = tuple(row_blocks)
    assert blocks == (0, 3)
    return pl.pallas_call(
        _proj_qqi_body,
        grid=(batch * nblk, len(blocks)),
        in_specs=[
            pl.BlockSpec((tm, d), lambda i, j: (i, 0)),
            pl.BlockSpec((1, d), lambda i, j: (0, 0)),
            pl.BlockSpec((ATTN_WIDTH, d), lambda i, j: (3 * j, 0)),
        ],
        out_specs=pl.BlockSpec((None, None, ATTN_WIDTH, tm),
                               lambda i, j: (j, i // nblk, 0, i % nblk)),
        out_shape=jax.ShapeDtypeStruct((len(blocks), batch, ATTN_WIDTH, seq), BF16),
        scratch_shapes=[pltpu.VMEM((tm, d), BF16)],
        compiler_params=_params("parallel", "arbitrary"),
        name="rms_proj_q_qi",
    )(x, g, wt_head)


def _proj_kv_body(x_ref, g_ref, wt_ref, ws_ref, kt_ref, vt_ref, knat_ref, vtb_ref, sidet_ref,
                  kinat_ref, z_ref):
    j = pl.program_id(1)

    @pl.when(j == 0)
    def _():
        _rms_to_bf16(x_ref, g_ref, z_ref)
        side = lax.dot_general(ws_ref[...], z_ref[...], _CONTRACT_LAST,
                               preferred_element_type=F32)
        sidet_ref[...] = side
        kinat_ref[...] = side.T.astype(BF16)

    o = lax.dot_general(wt_ref[...], z_ref[...], _CONTRACT_LAST, preferred_element_type=F32)

    @pl.when(j == 0)
    def _():
        kt_ref[...] = o
        knat_ref[...] = o.T.astype(BF16)

    @pl.when(j == 1)
    def _():
        vt_ref[...] = o
        tm = o.shape[1]
        vtb_ref[:, 0:HEAD_DIM, :] = o.reshape(N_HEADS, HEAD_DIM, tm).astype(BF16)
        row = lax.broadcasted_iota(I32, (N_HEADS, V_ROWS - HEAD_DIM, tm), 1)
        vtb_ref[:, HEAD_DIM:V_ROWS, :] = jnp.where(row == 0, 1.0, 0.0).astype(BF16)


def _proj_kv(x, g, wt_head, *, batch, seq, tm, side_block):
    d = x.shape[1]
    nblk = seq // tm
    tok = batch * seq
    bi = lambda i, j: (i // nblk, 0, i % nblk)
    return pl.pallas_call(
        _proj_kv_body,
        grid=(batch * nblk, 2),
        in_specs=[
            pl.BlockSpec((tm, d), lambda i, j: (i, 0)),
            pl.BlockSpec((1, d), lambda i, j: (0, 0)),
            pl.BlockSpec((ATTN_WIDTH, d), lambda i, j: (1 + j, 0)),
            pl.BlockSpec((LANES, d), lambda i, j: (side_block, 0)),
        ],
        out_specs=[
            pl.BlockSpec((None, ATTN_WIDTH, tm), bi),
            pl.BlockSpec((None, ATTN_WIDTH, tm), bi),
            pl.BlockSpec((tm, ATTN_WIDTH), lambda i, j: (i, 0)),
            pl.BlockSpec((None, N_HEADS, V_ROWS, tm), lambda i, j: (i // nblk, 0, 0, i % nblk)),
            pl.BlockSpec((None, LANES, tm), bi),
            pl.BlockSpec((tm, LANES), lambda i, j: (i, 0)),
        ],
        out_shape=[
            jax.ShapeDtypeStruct((batch, ATTN_WIDTH, seq), F32),
            jax.ShapeDtypeStruct((batch, ATTN_WIDTH, seq), F32),
            jax.ShapeDtypeStruct((tok, ATTN_WIDTH), BF16),
            jax.ShapeDtypeStruct((batch, N_HEADS, V_ROWS, seq), BF16),
            jax.ShapeDtypeStruct((batch, LANES, seq), F32),
            jax.ShapeDtypeStruct((tok, LANES), BF16),
        ],
        scratch_shapes=[pltpu.VMEM((tm, d), BF16)],
        compiler_params=_params("parallel", "arbitrary"),
        name="rms_proj_k_v",
    )(x, g, wt_head, wt_head)


def _prompt_attn_body(qit_ref, side_ref, qt_ref, ki_ref, k_ref, vt_ref, *rest,
                      tq, kc, topk, n_cast):
    w_f32_refs = rest[:n_cast]
    o_ref = rest[n_cast]
    w_bf16_refs = rest[n_cast + 1:2 * n_cast + 1]
    sc_ref, thr_ref, qtp_ref, ot_ref, m_ref, l_ref, a_ref, s_ref, p_ref = rest[2 * n_cast + 1:]
    for src, dst in zip(w_f32_refs, w_bf16_refs):
        dst[...] = src[...].astype(BF16)
    i = pl.program_id(1)
    nchunk = (i + 1) * (tq // kc)
    q0 = i * tq

    w16 = side_ref[IDX_DIM:IDX_DIM + IDX_HEADS, :] * (IDX_HEADS ** -0.5 * IDX_DIM ** -0.5)
    qt = qt_ref[...].astype(F32).reshape(N_HEADS // 2, 2 * HEAD_DIM, tq)
    first_half = lax.broadcasted_iota(I32, (2 * HEAD_DIM, tq), 0) < HEAD_DIM
    for j in range(N_HEADS // 2):
        qtp_ref[2 * j] = jnp.where(first_half, qt[j], 0.0).astype(BF16)
        qtp_ref[2 * j + 1] = jnp.where(first_half, 0.0, qt[j]).astype(BF16)

    def idx_chunk(r0, carry):
        lo8, hi8 = carry
        kic = ki_ref[pl.ds(r0, kc), 0:IDX_DIM]
        acc = jnp.zeros((kc, tq), F32)
        for h in range(IDX_HEADS):
            s = jnp.dot(kic, qit_ref[h * IDX_DIM:(h + 1) * IDX_DIM, :],
                        preferred_element_type=F32)
            acc = acc + jnp.maximum(s, 0.0) * w16[h:h + 1, :]
        krow = r0 + lax.broadcasted_iota(I32, (kc, tq), 0)
        qcol = q0 + lax.broadcasted_iota(I32, (kc, tq), 1)
        causal = krow <= qcol
        sc_ref[pl.ds(r0, kc), :] = jnp.where(causal, acc, -jnp.inf)
        lo8 = jnp.minimum(lo8, jnp.min(jnp.where(causal, acc, jnp.inf).reshape(kc // 8, 8, tq),
                                       axis=0))
        hi8 = jnp.maximum(hi8, jnp.max(jnp.where(causal, acc, -jnp.inf).reshape(kc // 8, 8, tq),
                                       axis=0))
        return lo8, hi8

    def idx_block(c, carry):
        for sub in range(tq // kc):
            carry = idx_chunk(pl.multiple_of(c * tq + sub * kc, kc), carry)
        return carry

    lo8, hi8 = lax.fori_loop(0, i + 1, idx_block,
                             (jnp.full((8, tq), jnp.inf, F32), jnp.full((8, tq), -jnp.inf, F32)))
    lo0 = jnp.min(lo8, axis=0, keepdims=True)
    hi0 = jnp.max(hi8, axis=0, keepdims=True)

    def count_rows(pred):
        def body(c, cnt):
            r0 = pl.multiple_of(c * tq, tq)
            krow = r0 + lax.broadcasted_iota(I32, (tq, tq), 0)
            hit = pred(sc_ref[pl.ds(r0, tq), :], krow).astype(I32)
            return cnt + jnp.sum(hit.reshape(tq // 8, 8, tq), axis=0)

        cnt8 = lax.fori_loop(0, i + 1, body, jnp.zeros((8, tq), I32))
        return jnp.sum(cnt8, axis=0, keepdims=True)

    def count_ge(t):
        return count_rows(lambda s, _: s >= t)

    def bisect(_, state):
        lo, hi, n_lo = state
        mid = 0.5 * lo + 0.5 * hi
        cnt = count_ge(mid)
        take = cnt >= topk
        return jnp.where(take, mid, lo), jnp.where(take, hi, mid), jnp.where(take, cnt, n_lo)

    def unresolved(state):
        lo, hi, n_lo = state
        mid = 0.5 * lo + 0.5 * hi
        return (n_lo > topk) & (mid > lo) & (mid < hi)

    n_causal = jnp.minimum(q0 + lax.broadcasted_iota(I32, (1, tq), 1) + 1, (i + 1) * tq)
    state = lax.fori_loop(0, SEARCH_FIXED_STEPS, bisect, (lo0, hi0, n_causal))

    def more_steps(carry):
        it, st = carry
        return jnp.logical_and(it < BISECT_MAX_STEPS, jnp.max(unresolved(st).astype(I32)) > 0)

    def step_group(carry):
        it, st = carry
        return it + SEARCH_GROUP, lax.fori_loop(0, SEARCH_GROUP, bisect, st)

    _, (lo, hi, n_lo) = lax.while_loop(more_steps, step_group,
                                       (jnp.int32(SEARCH_FIXED_STEPS), state))
    thr_ref[...] = lo

    has_tie = jnp.max((n_lo > topk).astype(I32)) > 0

    @pl.when(has_tie)
    def _():
        n_hi = count_ge(hi)
        thr = jnp.where((n_lo > topk) & (n_hi >= topk), hi, lo)
        thr_ref[...] = thr
        need = topk - count_rows(lambda s, _: s > thr)

        def jbit(bi, jlo):
            cand = jlo | jnp.left_shift(jnp.int32(1), 29 - bi)
            below = count_rows(lambda s, krow: (s == thr) & (krow < cand))
            return jnp.where(below < need, cand, jlo)

        jstar = lax.fori_loop(0, 30, jbit, jnp.zeros((1, tq), I32))

        def drop(c, carry):
            r0 = pl.multiple_of(c * kc, kc)
            krow = r0 + lax.broadcasted_iota(I32, (kc, tq), 0)
            s = sc_ref[pl.ds(r0, kc), :]
            sc_ref[pl.ds(r0, kc), :] = jnp.where((s == thr) & (krow > jstar), -jnp.inf, s)
            return carry

        lax.fori_loop(0, nchunk, drop, 0)

    tsel = thr_ref[...]

    m_ref[...] = jnp.full(m_ref.shape, NEG_BIG, F32)
    l_ref[...] = jnp.zeros(l_ref.shape, F32)
    ot_ref[...] = jnp.zeros(ot_ref.shape, F32)

    ka = s_ref.shape[1]

    def chunk_body(c, carry):
        r0 = pl.multiple_of(c * ka, ka)
        bias = jnp.where(sc_ref[pl.ds(r0, ka), :] >= tsel, 0.0, -jnp.inf)
        for h in range(N_HEADS):
            pair = slice((h // 2) * 2 * HEAD_DIM, (h // 2 + 1) * 2 * HEAD_DIM)
            s_ref[h] = jnp.dot(k_ref[pl.ds(r0, ka), pair], qtp_ref[h],
                               preferred_element_type=F32) + bias
        for h in range(N_HEADS):
            s = s_ref[h]
            m_old = m_ref[h]
            m_new = jnp.maximum(m_old, jnp.max(s, axis=0, keepdims=True))
            p_ref[h] = jnp.exp2(s - m_new).astype(BF16)
            a_ref[h] = jnp.exp2(m_old - m_new)
            m_ref[h] = m_new
        for h in range(N_HEADS):
            pv = jnp.dot(vt_ref[h, :, pl.ds(r0, ka)], p_ref[h], preferred_element_type=F32)
            ot_ref[h] = a_ref[h] * ot_ref[h] + pv[0:HEAD_DIM]
            l_ref[h] = a_ref[h] * l_ref[h] + pv[HEAD_DIM:HEAD_DIM + 1]
        return carry

    lax.fori_loop(0, (i + 1) * (tq // ka), chunk_body, 0)
    o = ot_ref[...] / l_ref[...]
    o_ref[...] = o.reshape(ATTN_WIDTH, tq).T.astype(BF16)


def _prompt_attention(qqi_t, side_t, ki_nat, k_nat, vt, cast_weights, *, batch, seq, tq, kc, topk):
    nq = seq // tq
    steps = batch * nq
    body = functools.partial(_prompt_attn_body, tq=tq, kc=kc, topk=topk,
                             n_cast=len(cast_weights))
    once = pl.Buffered(1)

    def slab_spec(w):
        rows = w.shape[0] // steps
        assert w.shape[0] % steps == 0 and rows % 16 == 0
        return pl.BlockSpec((rows, w.shape[1]), lambda b, i: (b * nq + i, 0))

    slabs = [slab_spec(w) for w in cast_weights]
    return pl.pallas_call(
        body,
        grid=(batch, nq),
        in_specs=[
            pl.BlockSpec((None, None, ATTN_WIDTH, tq), lambda b, i: (1, b, 0, i)),
            pl.BlockSpec((None, LANES, tq), lambda b, i: (b, 0, i)),
            pl.BlockSpec((None, None, ATTN_WIDTH, tq), lambda b, i: (0, b, 0, i)),
            pl.BlockSpec((seq, LANES), lambda b, i: (b, 0), pipeline_mode=once),
            pl.BlockSpec((seq, ATTN_WIDTH), lambda b, i: (b, 0), pipeline_mode=once),
            pl.BlockSpec((None, N_HEADS, V_ROWS, seq), lambda b, i: (b, 0, 0, 0),
                         pipeline_mode=once),
        ] + slabs,
        out_specs=[pl.BlockSpec((tq, ATTN_WIDTH), lambda b, i: (b * nq + i, 0))] + slabs,
        out_shape=[jax.ShapeDtypeStruct((batch * seq, ATTN_WIDTH), BF16)]
        + [jax.ShapeDtypeStruct(w.shape, BF16) for w in cast_weights],
        scratch_shapes=[
            pltpu.VMEM((seq, tq), F32),
            pltpu.VMEM((1, tq), F32),
            pltpu.VMEM((N_HEADS, 2 * HEAD_DIM, tq), BF16),
            pltpu.VMEM((N_HEADS, HEAD_DIM, tq), F32),
            pltpu.VMEM((N_HEADS, 1, tq), F32),
            pltpu.VMEM((N_HEADS, 1, tq), F32),
            pltpu.VMEM((N_HEADS, 1, tq), F32),
            pltpu.VMEM((N_HEADS, ATTN_PV_KEYS, tq), F32),
            pltpu.VMEM((N_HEADS, ATTN_PV_KEYS, tq), BF16),
        ],
        compiler_params=_params("parallel", "arbitrary"),
        name="prompt_sparse_attn",
    )(qqi_t, side_t, qqi_t, ki_nat, k_nat, vt, *cast_weights)


def _pool_prompt_rows(i, u_ref, wg_ref, sc_ref, ue_ref, *, tm):
    @pl.when(i == 0)
    def _():
        ue_ref[0:POOL_HALO, :] = jnp.zeros((POOL_HALO, ue_ref.shape[1]), F32)

    u = u_ref[...]
    ue_ref[POOL_HALO:POOL_HALO + tm, :] = u
    pos = i * tm + lax.broadcasted_iota(I32, (tm, 1), 0)
    outs = []
    for g, w in enumerate(POOL_WINDOWS):
        cols = slice(g * POOL_GROUP_WIDTH, (g + 1) * POOL_GROUP_WIDTH)
        win = u[:, cols]
        for back in range(1, w):
            win = win + ue_ref[POOL_HALO - back:POOL_HALO - back + tm, cols]
        cnt = jnp.minimum(pos + 1, w).astype(F32)
        mixed = win / cnt - u[:, cols]
        outs.append(jnp.dot(mixed.astype(BF16), wg_ref[g], preferred_element_type=F32))
    ue_ref[0:POOL_HALO, :] = u[tm - POOL_HALO:, :]
    return (jnp.concatenate(outs, axis=1) * sc_ref[...]).astype(BF16)


def _pool_sample_body(st_ref, u_ref, wg_ref, sc_ref, o_ref):
    st = st_ref[...]
    u = u_ref[...]
    n_state = st.shape[1]
    outs = []
    for g, w in enumerate(POOL_WINDOWS):
        cols = slice(g * POOL_GROUP_WIDTH, (g + 1) * POOL_GROUP_WIDTH)
        win = u[:, cols] + jnp.sum(st[:, n_state - (w - 1):, cols], axis=1)
        mixed = win / float(w) - u[:, cols]
        outs.append(jnp.dot(mixed.astype(BF16), wg_ref[g], preferred_element_type=F32))
    o_ref[...] = (jnp.concatenate(outs, axis=1) * sc_ref[...]).astype(BF16)


def _pool_sample(state, u_new, wg, scale):
    nb, width = u_new.shape
    return pl.pallas_call(
        _pool_sample_body,
        out_shape=jax.ShapeDtypeStruct((nb, width), BF16),
        compiler_params=pltpu.CompilerParams(vmem_limit_bytes=VMEM_LIMIT_BYTES),
        name="pool_sample",
    )(state, u_new, wg, scale)


def _merge_rows(pool_o, x_ref, ao_ref, ga_ref, gb_ref, bg_ref, wa_ref, wp_ref, wo_ref, g2_ref,
                h_ref, z_ref):
    ya = jnp.dot(ao_ref[...], wa_ref[...], preferred_element_type=F32)
    yb = jnp.dot(pool_o, wp_ref[...], preferred_element_type=F32)
    merged = (jax.nn.sigmoid(ga_ref[...] + bg_ref[0:1, :]) * ya
              + jax.nn.sigmoid(gb_ref[...] + bg_ref[1:2, :]) * yb)
    h = x_ref[...] + jnp.dot(merged.astype(BF16), wo_ref[...], preferred_element_type=F32)
    h_ref[...] = h
    ms = jnp.mean(h * h, axis=-1, keepdims=True)
    z_ref[...] = (h * lax.rsqrt(ms + EPS) * g2_ref[...]).astype(BF16)


def _merge_body(x_ref, ao_ref, po_ref, ga_ref, gb_ref, bg_ref, wa_ref, wp_ref, wo_ref, g2_ref,
                h_ref, z_ref):
    _merge_rows(po_ref[...], x_ref, ao_ref, ga_ref, gb_ref, bg_ref, wa_ref, wp_ref, wo_ref,
                g2_ref, h_ref, z_ref)


def _pool_merge_body(x_ref, ao_ref, u_ref, ga_ref, gb_ref, bg_ref, wg_ref, sc_ref, wa_ref, wp_ref,
                     wo_ref, g2_ref, h_ref, z_ref, ue_ref, *, tm):
    pool_o = _pool_prompt_rows(pl.program_id(1), u_ref, wg_ref, sc_ref, ue_ref, tm=tm)
    _merge_rows(pool_o, x_ref, ao_ref, ga_ref, gb_ref, bg_ref, wa_ref, wp_ref, wo_ref, g2_ref,
                h_ref, z_ref)


def _pool_merge(x, attn_o, p_main, b_gate, wg, scale, wa, wp, wo, g2, *, batch, seq, tm, u_col,
                ga_col, gb_col):
    m, d = x.shape
    nblk = seq // tm
    width = wg.shape[0] * wg.shape[1]
    rows = lambda b, i: (b * nblk + i, 0)
    const = lambda b, i: (0, 0)
    return pl.pallas_call(
        functools.partial(_pool_merge_body, tm=tm),
        grid=(batch, nblk),
        in_specs=[
            pl.BlockSpec((tm, d), rows),
            pl.BlockSpec((tm, attn_o.shape[1]), rows),
            pl.BlockSpec((tm, width), lambda b, i: (b * nblk + i, u_col)),
            pl.BlockSpec((tm, d), lambda b, i: (b * nblk + i, ga_col)),
            pl.BlockSpec((tm, d), lambda b, i: (b * nblk + i, gb_col)),
            pl.BlockSpec(b_gate.shape, const),
            pl.BlockSpec(wg.shape, lambda b, i: (0, 0, 0)),
            pl.BlockSpec((1, width), const),
            pl.BlockSpec(wa.shape, const),
            pl.BlockSpec(wp.shape, const),
            pl.BlockSpec(wo.shape, const),
            pl.BlockSpec((1, d), const),
        ],
        out_specs=[pl.BlockSpec((tm, d), rows), pl.BlockSpec((tm, d), rows)],
        out_shape=[jax.ShapeDtypeStruct((m, d), F32), jax.ShapeDtypeStruct((m, d), BF16)],
        scratch_shapes=[pltpu.VMEM((POOL_HALO + tm, width), F32)],
        compiler_params=_params("parallel", "arbitrary"),
        name="pool_gated_merge",
    )(x, attn_o, p_main, p_main, p_main, b_gate, wg, scale, wa, wp, wo, g2)


def _merge(x, attn_o, pool_o, p_main, b_gate, wa, wp, wo, g2, *, tm, ga_col, gb_col):
    m, d = x.shape
    const = lambda i: (0, 0)
    return pl.pallas_call(
        _merge_body,
        grid=(m // tm,),
        in_specs=[
            pl.BlockSpec((tm, d), lambda i: (i, 0)),
            pl.BlockSpec((tm, attn_o.shape[1]), lambda i: (i, 0)),
            pl.BlockSpec((tm, pool_o.shape[1]), lambda i: (i, 0)),
            pl.BlockSpec((tm, d), lambda i: (i, ga_col)),
            pl.BlockSpec((tm, d), lambda i: (i, gb_col)),
            pl.BlockSpec(b_gate.shape, const),
            pl.BlockSpec(wa.shape, const),
            pl.BlockSpec(wp.shape, const),
            pl.BlockSpec(wo.shape, const),
            pl.BlockSpec((1, d), const),
        ],
        out_specs=[pl.BlockSpec((tm, d), lambda i: (i, 0)), pl.BlockSpec((tm, d), lambda i: (i, 0))],
        out_shape=[jax.ShapeDtypeStruct((m, d), F32), jax.ShapeDtypeStruct((m, d), BF16)],
        compiler_params=_params("parallel"),
        name="gated_merge",
    )(x, attn_o, pool_o, p_main, p_main, b_gate, wa, wp, wo, g2)


def _ffn_body(z_ref, h_ref, wu_ref, wd_ref, gf_ref, y_ref, acc_ref):
    f = pl.program_id(1)

    @pl.when(f == 0)
    def _():
        acc_ref[...] = h_ref[...]

    a = jnp.dot(z_ref[...], wu_ref[...], preferred_element_type=F32)
    a = jnp.square(jnp.maximum(a, 0.0))
    acc_ref[...] += jnp.dot(a.astype(BF16), wd_ref[...], preferred_element_type=F32)

    @pl.when(f == pl.num_programs(1) - 1)
    def _():
        h2 = acc_ref[...]
        ms = jnp.mean(h2 * h2, axis=-1, keepdims=True)
        y_ref[...] = h2 * lax.rsqrt(ms + EPS) * gf_ref[...]


def _ffn(z, h, wu, wd, gf, *, tm, tf):
    m, d = h.shape
    dff = wu.shape[1]
    return pl.pallas_call(
        _ffn_body,
        grid=(m // tm, dff // tf),
        in_specs=[
            pl.BlockSpec((tm, d), lambda i, f: (i, 0)),
            pl.BlockSpec((tm, d), lambda i, f: (i, 0)),
            pl.BlockSpec((d, tf), lambda i, f: (0, f)),
            pl.BlockSpec((tf, d), lambda i, f: (f, 0)),
            pl.BlockSpec((1, d), lambda i, f: (0, 0)),
        ],
        out_specs=pl.BlockSpec((tm, d), lambda i, f: (i, 0)),
        out_shape=jax.ShapeDtypeStruct((m, d), F32),
        scratch_shapes=[pltpu.VMEM((tm, d), F32)],
        compiler_params=_params("parallel", "arbitrary"),
        name="ffn_final_norm",
    )(z, h, wu, wd, gf)


def _sample_scores_body(pt_ref, qi_ref, w_ref, newpage_ref, *rest, n_pages_step):
    page_refs = rest[:n_pages_step]
    o_ref, onew_ref = rest[n_pages_step:]
    qi = qi_ref[...].astype(BF16)
    wcol = w_ref[...] * (IDX_HEADS ** -0.5 * IDX_DIM ** -0.5)

    def score(pages_t):
        s = jnp.dot(qi, pages_t.astype(BF16), preferred_element_type=F32)
        return jnp.sum(jnp.maximum(s, 0.0) * wcol, axis=0, keepdims=True)

    o_ref[...] = score(jnp.concatenate([r[...] for r in page_refs], axis=1))
    onew_ref[...] = score(newpage_ref[...])


def _sample_scores(page_table, qi, w, newpage, kidx_t, *, pages_step):
    nb, n_pages = page_table.shape
    page = kidx_t.shape[3]
    steps = n_pages // pages_step

    def page_spec(j):
        return pl.BlockSpec((None, None, IDX_DIM, page),
                            lambda b, g, pt: (0, pt[b, g * pages_step + j], 0, 0))

    grid_spec = pltpu.PrefetchScalarGridSpec(
        num_scalar_prefetch=1,
        grid=(nb, steps),
        in_specs=[
            pl.BlockSpec((None, IDX_HEADS, IDX_DIM), lambda b, g, pt: (b, 0, 0)),
            pl.BlockSpec((None, IDX_HEADS, 1), lambda b, g, pt: (b, 0, 0)),
            pl.BlockSpec((None, IDX_DIM, page), lambda b, g, pt: (b, 0, 0)),
        ] + [page_spec(j) for j in range(pages_step)],
        out_specs=[
            pl.BlockSpec((None, 1, pages_step * page), lambda b, g, pt: (b, 0, g)),
            pl.BlockSpec((None, 1, page), lambda b, g, pt: (b, 0, 0)),
        ],
    )
    return pl.pallas_call(
        functools.partial(_sample_scores_body, n_pages_step=pages_step),
        grid_spec=grid_spec,
        out_shape=[jax.ShapeDtypeStruct((nb, 1, n_pages * page), F32),
                   jax.ShapeDtypeStruct((nb, 1, page), F32)],
        compiler_params=_params("parallel", "arbitrary"),
        name="sample_indexer_scores",
    )(page_table, qi, w, newpage, *([kidx_t] * pages_step))


def _prefix_exclusive(mask_f32, tri):
    rows, length = mask_f32.shape
    off = jnp.zeros((rows, 1), F32)
    parts = []
    for c in range(length // LANES):
        mc = mask_f32[:, c * LANES:(c + 1) * LANES]
        inc = jnp.dot(mc.astype(BF16), tri, preferred_element_type=F32)
        parts.append(inc - mc + off)
        off = off + inc[:, LANES - 1:LANES]
    return jnp.concatenate(parts, axis=1)


def _sample_select_body(sc_ref, scnew_ref, bias_ref, biasnew_ref, *, topk):
    past = sc_ref.shape[1]
    sc = jnp.concatenate([sc_ref[...], scnew_ref[...]], axis=1)
    nb, length = sc.shape
    pos = lax.broadcasted_iota(I32, (nb, length), 1)
    valid = pos <= past
    scv = jnp.where(valid, sc, -jnp.inf)

    def count_ge(t):
        return jnp.sum((scv >= t).astype(I32), axis=1, keepdims=True)

    lo0 = jnp.min(jnp.where(valid, sc, jnp.inf), axis=1, keepdims=True)
    hi0 = jnp.max(scv, axis=1, keepdims=True)

    def unresolved(lo, hi, n_lo):
        mid = 0.5 * lo + 0.5 * hi
        return (n_lo > topk) & (mid > lo) & (mid < hi)

    def more(carry):
        it, lo, hi, n_lo = carry
        return jnp.logical_and(it < BISECT_MAX_STEPS,
                               jnp.max(unresolved(lo, hi, n_lo).astype(I32)) > 0)

    def step(carry):
        it, lo, hi, n_lo = carry
        mid = 0.5 * lo + 0.5 * hi
        cnt = count_ge(mid)
        take = cnt >= topk
        return it + 1, jnp.where(take, mid, lo), jnp.where(take, hi, mid), jnp.where(take, cnt, n_lo)

    _, lo, hi, n_lo = lax.while_loop(more, step, (jnp.int32(0), lo0, hi0, count_ge(lo0)))
    thr = jnp.where(count_ge(hi) >= topk, hi, lo)
    gt = scv > thr
    eq = (scv == thr) & valid
    tri = (lax.broadcasted_iota(I32, (LANES, LANES), 0)
           <= lax.broadcasted_iota(I32, (LANES, LANES), 1)).astype(BF16)
    need = (topk - jnp.sum(gt.astype(I32), axis=1, keepdims=True)).astype(F32)
    eq_rank = _prefix_exclusive(eq.astype(F32), tri)
    sel = (gt | (eq & (eq_rank < need))) & valid
    bias = jnp.where(sel, 0.0, -jnp.inf)
    bias_ref[...] = bias[:, :past]
    biasnew_ref[...] = bias[:, past:]


def _sample_select(scores, score_new, *, topk):
    nb, past = scores.shape
    return pl.pallas_call(
        functools.partial(_sample_select_body, topk=topk),
        out_shape=[jax.ShapeDtypeStruct((nb, past), F32),
                   jax.ShapeDtypeStruct(score_new.shape, F32)],
        compiler_params=pltpu.CompilerParams(vmem_limit_bytes=VMEM_LIMIT_BYTES),
        name="sample_topk_select",
    )(scores, score_new)


def _heads_on_lanes_to_slabs(x_t, page):
    return jnp.stack([jnp.broadcast_to(x_t[:, h:h + 1], (HEAD_DIM, page))
                      for h in range(N_HEADS)])


def _lane_softmax_update(scores, values, m_ref, l_ref, acc_ref):
    m_old = m_ref[...]
    m_new = m_old
    for s in scores:
        m_new = jnp.maximum(m_new, s)
    alpha = jnp.exp2(m_old - m_new)
    l = alpha * l_ref[...]
    acc = alpha * acc_ref[...]
    for s, v in zip(scores, values):
        p = jnp.exp2(s - m_new)
        l = l + p
        acc = acc + p * v
    m_ref[...] = m_new
    l_ref[...] = l
    acc_ref[...] = acc


def _attend_pages(qt_ref, bias_ref, k_refs, v_refs, m_ref, l_ref, acc_ref, *, page):
    q = _heads_on_lanes_to_slabs(qt_ref[...], page)
    scores = []
    for j, k_ref in enumerate(k_refs):
        s = jnp.sum(k_ref[...] * q, axis=1, keepdims=True)
        scores.append(s + bias_ref[:, j * page:(j + 1) * page][None])
    _lane_softmax_update(scores, [v_ref[...] for v_ref in v_refs], m_ref, l_ref, acc_ref)


def _attend_finish(qt_ref, knt_ref, vnt_ref, biasnew_ref, o_ref, m_ref, l_ref, acc_ref, *, page):
    q = _heads_on_lanes_to_slabs(qt_ref[...], page)
    kn = _heads_on_lanes_to_slabs(knt_ref[...], page)
    s_new = jnp.sum(kn * q, axis=1, keepdims=True) + biasnew_ref[...][None]
    _lane_softmax_update([s_new], [_heads_on_lanes_to_slabs(vnt_ref[...], page)],
                         m_ref, l_ref, acc_ref)
    m = m_ref[...]
    f = jnp.exp2(m - jnp.max(m, axis=-1, keepdims=True))
    denom = jnp.sum(l_ref[...] * f, axis=-1, keepdims=True)
    o_ref[...] = jnp.sum(acc_ref[...] * f, axis=-1, keepdims=True) / denom


def _ffn_attend_body(pt_ref, z_ref, h_ref, wu_ref, wd_ref, gf_ref,
                     qt_ref, knt_ref, vnt_ref, bias_ref, biasnew_ref, *rest,
                     n_pages_step, page, steps_per_sample):
    k_refs = rest[:n_pages_step]
    v_refs = rest[n_pages_step:2 * n_pages_step]
    y_ref, o_ref, m_ref, l_ref, pacc_ref = rest[2 * n_pages_step:]
    acc_ref = y_ref
    f = pl.program_id(1)
    g = (pl.program_id(0) * pl.num_programs(1) + f) % steps_per_sample

    @pl.when(f == 0)
    def _():
        acc_ref[...] = h_ref[...]

    @pl.when(g == 0)
    def _():
        m_ref[...] = jnp.full(m_ref.shape, NEG_BIG, F32)
        l_ref[...] = jnp.zeros(l_ref.shape, F32)
        pacc_ref[...] = jnp.zeros(pacc_ref.shape, F32)

    a = jnp.dot(z_ref[...], wu_ref[...], preferred_element_type=F32)
    a = jnp.square(jnp.maximum(a, 0.0))
    acc_ref[...] += jnp.dot(a.astype(BF16), wd_ref[...], preferred_element_type=F32)
    _attend_pages(qt_ref, bias_ref, k_refs, v_refs, m_ref, l_ref, pacc_ref, page=page)

    @pl.when(f == pl.num_programs(1) - 1)
    def _():
        h2 = acc_ref[...]
        ms = jnp.mean(h2 * h2, axis=-1, keepdims=True)
        y_ref[...] = h2 * lax.rsqrt(ms + EPS) * gf_ref[...]

    @pl.when(g == steps_per_sample - 1)
    def _():
        _attend_finish(qt_ref, knt_ref, vnt_ref, biasnew_ref, o_ref, m_ref, l_ref, pacc_ref,
                       page=page)


def _ffn_attend(z, h, wu, wd, gf, page_table, qt, knt, vnt, bias, bias_new, ck_t, cv_t, *, tm, tf):
    m, d = h.shape
    dff = wu.shape[1]
    nb, n_pages = page_table.shape
    page = ck_t.shape[4]
    ni, nf = m // tm, dff // tf
    total_steps = ni * nf
    assert (nb * n_pages) % total_steps == 0
    pages_step = nb * n_pages // total_steps
    assert n_pages % pages_step == 0
    sps = n_pages // pages_step

    def sample(i, f):
        return (i * nf + f) // sps

    def page_spec(j):
        def index(i, f, pt):
            t = i * nf + f
            return (0, pt[t // sps, (t % sps) * pages_step + j], 0, 0, 0)

        return pl.BlockSpec((None, None, N_HEADS, HEAD_DIM, page), index)

    pages = [page_spec(j) for j in range(pages_step)]
    tq_block = pl.BlockSpec((None, HEAD_DIM, N_HEADS), lambda i, f, pt: (sample(i, f), 0, 0))
    grid_spec = pltpu.PrefetchScalarGridSpec(
        num_scalar_prefetch=1,
        grid=(ni, nf),
        in_specs=[
            pl.BlockSpec((tm, d), lambda i, f, pt: (i, 0)),
            pl.BlockSpec((tm, d), lambda i, f, pt: (i, 0), pipeline_mode=pl.Buffered(1)),
            pl.BlockSpec((d, tf), lambda i, f, pt: (0, f)),
            pl.BlockSpec((tf, d), lambda i, f, pt: (f, 0)),
            pl.BlockSpec((1, d), lambda i, f, pt: (0, 0)),
            tq_block, tq_block, tq_block,
            pl.BlockSpec((None, 1, pages_step * page),
                         lambda i, f, pt: (sample(i, f), 0, (i * nf + f) % sps)),
            pl.BlockSpec((None, 1, page), lambda i, f, pt: (sample(i, f), 0, 0)),
        ] + pages + pages,
        out_specs=[
            pl.BlockSpec((tm, d), lambda i, f, pt: (i, 0)),
            pl.BlockSpec((None, N_HEADS, HEAD_DIM, 1), lambda i, f, pt: (sample(i, f), 0, 0, 0)),
        ],
        scratch_shapes=[
            pltpu.VMEM((N_HEADS, 1, page), F32),
            pltpu.VMEM((N_HEADS, 1, page), F32),
            pltpu.VMEM((N_HEADS, HEAD_DIM, page), F32),
        ],
    )
    return pl.pallas_call(
        functools.partial(_ffn_attend_body, n_pages_step=pages_step, page=page,
                          steps_per_sample=sps),
        grid_spec=grid_spec,
        out_shape=[jax.ShapeDtypeStruct((m, d), F32),
                   jax.ShapeDtypeStruct((nb, N_HEADS, HEAD_DIM, 1), F32)],
        compiler_params=_params("arbitrary", "arbitrary"),
        name="ffn_with_paged_attend",
    )(page_table, z, h, wu, wd, gf, qt, knt, vnt, bias, bias_new,
      *([ck_t] * pages_step), *([cv_t] * pages_step))


def kernel(x_prompt, x_sample, cache_k, cache_v, cache_kidx, state_pool, page_table, norm_mix_g,
           w_in, b_gate, w_attn_out, w_pool_group, pool_scale, w_pool_out, w_out, norm_ffn_g,
           w_up, w_down, norm_final_g):
    depth = w_in.shape[0]
    assert depth == 1, "single-layer step"
    batch, seq, d_model = x_prompt.shape
    dec_batch, dec_seq, _ = x_sample.shape
    assert dec_seq == 1, "one new token per sample"
    n_pages = page_table.shape[1]
    page = cache_k.shape[2]
    past = n_pages * page
    pool_width = pool_scale.shape[1]
    score_pages = min(SCORE_PAGES, n_pages)
    assert page == LANES and n_pages % score_pages == 0
    l = 0

    tn = PROJ_COLS
    assert ATTN_WIDTH == tn and pool_width == tn and d_model == 2 * tn and IDX_HEADS * IDX_DIM == tn
    c_ki = 3 * ATTN_WIDTH + IDX_HEADS * IDX_DIM
    c_u = c_ki + IDX_DIM + IDX_HEADS
    c_ga = c_u + pool_width
    n_head_rows = c_ki + LANES
    wt = jnp.swapaxes(w_in[l], 0, 1)
    wt_head = jnp.concatenate([wt[:ATTN_WIDTH] * Q_SCALE, wt[ATTN_WIDTH:n_head_rows]],
                              axis=0).astype(BF16)
    wt_tail = jnp.concatenate([wt[c_ga:], wt[c_u:c_ga]], axis=0).astype(BF16)
    K_COL, V_COL, QI_COL = 1, 2, 3
    SIDE_BLOCK = c_ki // LANES
    GA_COL, GB_COL, U_COL = 0, 1, 4
    g_mix = norm_mix_g[l][None, :]
    wg = w_pool_group[l].astype(BF16)
    g_ffn = norm_ffn_g[l][None, :]
    g_fin = norm_final_g[None, :]
    scale = pool_scale[l][None, :]

    def col(p, c, width=tn):
        return p[:, c * tn:c * tn + width]

    xp = x_prompt.reshape(batch * seq, d_model)
    tm_p = PROJ_ROWS
    tm_wide = PROJ_ROWS_WIDE if seq % PROJ_ROWS_WIDE == 0 else PROJ_ROWS
    assert seq % tm_p == 0
    qqi_t = _proj_qqi(xp, g_mix, wt_head, batch=batch, seq=seq, tm=tm_wide,
                      row_blocks=(0, QI_COL))
    k_t, v_t, k_nat, vt_bf, side_t, ki_nat = _proj_kv(xp, g_mix, wt_head, batch=batch, seq=seq,
                                                      tm=tm_p, side_block=SIDE_BLOCK)
    pp = _proj_nat(xp, g_mix, wt_tail, tm_wide, tn)

    tq, kc = ATTN_QUERIES, ATTN_KEYS
    topk_p = min(TOPK_MAX, seq // 4)
    attn_p, wa, wp, wo, wu, wd = _prompt_attention(
        qqi_t, side_t, ki_nat, k_nat, vt_bf,
        [w_attn_out[l], w_pool_out[l], w_out[l], w_up[l], w_down[l]],
        batch=batch, seq=seq, tq=tq, kc=kc, topk=topk_p)
    h_p, z_p = _pool_merge(xp, attn_p, pp, b_gate[l], wg, scale, wa, wp, wo, g_ffn, batch=batch,
                           seq=seq, tm=MERGE_ROWS, u_col=U_COL, ga_col=GA_COL, gb_col=GB_COL)

    xs = x_sample.reshape(dec_batch, d_model)
    head_tile = max(t for t in (1024, 768, 512, 384, 256, 128) if n_head_rows % t == 0)
    ps_head = _proj_nat(xs, g_mix, wt_head, dec_batch, head_tile)
    ps = _proj_nat(xs, g_mix, wt_tail, dec_batch, tn)
    topk_s = min(TOPK_MAX, (past + dec_seq) // 4)
    q_s = col(ps_head, 0)
    k_s = col(ps_head, K_COL)
    v_s = col(ps_head, V_COL)
    ki_s = ps_head[:, c_ki:c_ki + IDX_DIM]
    wi_s = ps_head[:, c_ki + IDX_DIM:c_u]
    u_s = col(ps, U_COL)
    qi_s = col(ps_head, QI_COL).reshape(dec_batch, IDX_HEADS, IDX_DIM)
    kidx_t = jnp.swapaxes(cache_kidx, 2, 3)
    ck_t = jnp.transpose(cache_k, (0, 1, 3, 4, 2))
    cv_t = jnp.transpose(cache_v, (0, 1, 3, 4, 2))
    newpage = jnp.zeros((dec_batch, IDX_DIM, page), F32).at[:, :, 0].set(ki_s)
    scores, score_new = _sample_scores(page_table, qi_s, wi_s[:, :, None], newpage, kidx_t,
                                       pages_step=score_pages)
    bias, bias_new = _sample_select(scores.reshape(dec_batch, past),
                                    score_new.reshape(dec_batch, page), topk=topk_s)

    def dims_by_heads(t):
        return t.reshape(dec_batch, N_HEADS, HEAD_DIM).transpose(0, 2, 1)

    y_p, attn_s = _ffn_attend(z_p, h_p, wu, wd, g_fin, page_table, dims_by_heads(q_s),
                              dims_by_heads(k_s), dims_by_heads(v_s), bias[:, None, :],
                              bias_new[:, None, :], ck_t, cv_t, tm=FFN_ROWS, tf=FFN_COLS)
    pool_s = _pool_sample(state_pool[l], u_s, wg, scale)
    h_s, z_s = _merge(xs, attn_s.reshape(dec_batch, ATTN_WIDTH).astype(BF16), pool_s, ps,
                      b_gate[l], wa, wp, wo, g_ffn, tm=dec_batch, ga_col=GA_COL, gb_col=GB_COL)
    y_s = _ffn(z_s, h_s, wu, wd, g_fin, tm=dec_batch, tf=FFN_COLS)

    n_state = state_pool.shape[2]
    u_p = pp.reshape(batch, seq, -1)[:, -n_state:, U_COL * tn:(U_COL + 1) * tn]
    pool_state_s = jnp.concatenate([state_pool[l].astype(F32), u_s[:, None, :]], axis=1)[:, -n_state:]

    def heads_last(t):
        return t.reshape(batch, N_HEADS, HEAD_DIM, seq).transpose(0, 3, 1, 2)[None]

    return (
        y_p.reshape(batch, seq, d_model),
        y_s.reshape(dec_batch, dec_seq, d_model),
        heads_last(k_t),
        heads_last(v_t),
        side_t[:, :IDX_DIM, :].transpose(0, 2, 1)[None],
        u_p[None],
        k_s.reshape(1, dec_batch, dec_seq, N_HEADS, HEAD_DIM),
        v_s.reshape(1, dec_batch, dec_seq, N_HEADS, HEAD_DIM),
        ki_s.reshape(1, dec_batch, dec_seq, IDX_DIM),
        pool_state_s[None],
    )
```

```python
import functools

import jax
import jax.numpy as jnp
from jax import lax
from jax.experimental import pallas as pl
from jax.experimental.pallas import tpu as pltpu

F32 = jnp.float32
BF16 = jnp.bfloat16
I32 = jnp.int32

N_HEADS = 16
HEAD_DIM = 64
ATTN_WIDTH = N_HEADS * HEAD_DIM
IDX_HEADS = 16
IDX_DIM = 64
TOPK_MAX = 256
POOL_WINDOWS = (2, 4, 8, 16)
POOL_GROUP_WIDTH = 256
POOL_HALO = 16
EPS = 1e-6
INT_MIN = -(2 ** 31)
LANES = 128
NEG_BIG = -1e30
Q_SCALE = HEAD_DIM ** -0.5 * 1.4426950408889634
V_ROWS = HEAD_DIM + 16

VMEM_LIMIT_BYTES = 56 * 1024 * 1024

PROJ_ROWS = 512
PROJ_ROWS_WIDE = 1024
PROJ_COLS = 1024
ATTN_QUERIES = 256
ATTN_KEYS = 128
ATTN_PV_KEYS = 256
MERGE_ROWS = 256
FFN_ROWS = 512
FFN_COLS = 512
SCORE_PAGES = 64
SEARCH_FIXED_STEPS = 12
SEARCH_GROUP = 4
BISECT_MAX_STEPS = 512


def _params(*sem):
    return pltpu.CompilerParams(dimension_semantics=sem, vmem_limit_bytes=VMEM_LIMIT_BYTES)


_CONTRACT_LAST = (((1,), (1,)), ((), ()))


def _rms_to_bf16(x_ref, g_ref, z_ref):
    x = x_ref[...]
    ms = jnp.mean(x * x, axis=-1, keepdims=True)
    z_ref[...] = (x * lax.rsqrt(ms + EPS) * g_ref[...]).astype(BF16)


def _proj_nat_body(x_ref, g_ref, wt_ref, o_ref, z_ref):
    @pl.when(pl.program_id(1) == 0)
    def _():
        _rms_to_bf16(x_ref, g_ref, z_ref)

    o_ref[...] = lax.dot_general(z_ref[...], wt_ref[...], _CONTRACT_LAST,
                                 preferred_element_type=F32)


def _proj_nat(x, g, wt, tm, tn, *, row_start=0, n=None, out_shift=0):
    m, d = x.shape
    n = wt.shape[0] if n is None else n
    nj = n // tn
    return pl.pallas_call(
        _proj_nat_body,
        grid=(m // tm, nj),
        in_specs=[
            pl.BlockSpec((tm, d), lambda i, j: (i, 0)),
            pl.BlockSpec((1, d), lambda i, j: (0, 0)),
            pl.BlockSpec((pl.Element(tn), pl.Element(d)),
                         lambda i, j: (pl.multiple_of(row_start + j * tn, 16), 0)),
        ],
        out_specs=pl.BlockSpec((tm, tn), lambda i, j: (i, (j + out_shift) % nj)),
        out_shape=jax.ShapeDtypeStruct((m, n), F32),
        scratch_shapes=[pltpu.VMEM((tm, d), BF16)],
        compiler_params=_params("parallel", "arbitrary"),
        name="rms_proj_rows",
    )(x, g, wt)


def _proj_qqi_body(x_ref, g_ref, wt_ref, o_ref, z_ref):
    @pl.when(pl.program_id(1) == 0)
    def _():
        _rms_to_bf16(x_ref, g_ref, z_ref)

    scale = jnp.where(pl.program_id(1) == 0, Q_SCALE, 1.0)
    o = lax.dot_general(wt_ref[...], z_ref[...], _CONTRACT_LAST, preferred_element_type=F32)
    o_ref[...] = (o * scale).astype(BF16)


def _proj_qqi(x, g, wt_head, *, batch, seq, tm, row_blocks):
    d = x.shape[1]
    nblk = seq // tm
    blocks = tuple(row_blocks)
    assert blocks == (0, 3)
    return pl.pallas_call(
        _proj_qqi_body,
        grid=(batch * nblk, len(blocks)),
        in_specs=[
            pl.BlockSpec((tm, d), lambda i, j: (i, 0)),
            pl.BlockSpec((1, d), lambda i, j: (0, 0)),
            pl.BlockSpec((ATTN_WIDTH, d), lambda i, j: (3 * j, 0)),
        ],
        out_specs=pl.BlockSpec((None, None, ATTN_WIDTH, tm),
                               lambda i, j: (j, i // nblk, 0, i % nblk)),
        out_shape=jax.ShapeDtypeStruct((len(blocks), batch, ATTN_WIDTH, seq), BF16),
        scratch_shapes=[pltpu.VMEM((tm, d), BF16)],
        compiler_params=_params("parallel", "arbitrary"),
        name="rms_proj_q_qi",
    )(x, g, wt_head)


def _proj_kv_body(x_ref, g_ref, wt_ref, ws_ref, kt_ref, vt_ref, knat_ref, vtb_ref, sidet_ref,
                  kinat_ref, z_ref):
    j = pl.program_id(1)

    @pl.when(j == 0)
    def _():
        _rms_to_bf16(x_ref, g_ref, z_ref)
        side = lax.dot_general(ws_ref[...], z_ref[...], _CONTRACT_LAST,
                               preferred_element_type=F32)
        sidet_ref[...] = side
        kinat_ref[...] = side.T.astype(BF16)

    o = lax.dot_general(wt_ref[...], z_ref[...], _CONTRACT_LAST, preferred_element_type=F32)

    @pl.when(j == 0)
    def _():
        kt_ref[...] = o
        knat_ref[...] = o.T.astype(BF16)

    @pl.when(j == 1)
    def _():
        vt_ref[...] = o
        tm = o.shape[1]
        vtb_ref[:, 0:HEAD_DIM, :] = o.reshape(N_HEADS, HEAD_DIM, tm).astype(BF16)
        row = lax.broadcasted_iota(I32, (N_HEADS, V_ROWS - HEAD_DIM, tm), 1)
        vtb_ref[:, HEAD_DIM:V_ROWS, :] = jnp.where(row == 0, 1.0, 0.0).astype(BF16)


def _proj_kv(x, g, wt_head, *, batch, seq, tm, side_block):
    d = x.shape[1]
    nblk = seq // tm
    tok = batch * seq
    bi = lambda i, j: (i // nblk, 0, i % nblk)
    return pl.pallas_call(
        _proj_kv_body,
        grid=(batch * nblk, 2),
        in_specs=[
            pl.BlockSpec((tm, d), lambda i, j: (i, 0)),
            pl.BlockSpec((1, d), lambda i, j: (0, 0)),
            pl.BlockSpec((ATTN_WIDTH, d), lambda i, j: (1 + j, 0)),
            pl.BlockSpec((LANES, d), lambda i, j: (side_block, 0)),
        ],
        out_specs=[
            pl.BlockSpec((None, ATTN_WIDTH, tm), bi),
            pl.BlockSpec((None, ATTN_WIDTH, tm), bi),
            pl.BlockSpec((tm, ATTN_WIDTH), lambda i, j: (i, 0)),
            pl.BlockSpec((None, N_HEADS, V_ROWS, tm), lambda i, j: (i // nblk, 0, 0, i % nblk)),
            pl.BlockSpec((None, LANES, tm), bi),
            pl.BlockSpec((tm, LANES), lambda i, j: (i, 0)),
        ],
        out_shape=[
            jax.ShapeDtypeStruct((batch, ATTN_WIDTH, seq), F32),
            jax.ShapeDtypeStruct((batch, ATTN_WIDTH, seq), F32),
            jax.ShapeDtypeStruct((tok, ATTN_WIDTH), BF16),
            jax.ShapeDtypeStruct((batch, N_HEADS, V_ROWS, seq), BF16),
            jax.ShapeDtypeStruct((batch, LANES, seq), F32),
            jax.ShapeDtypeStruct((tok, LANES), BF16),
        ],
        scratch_shapes=[pltpu.VMEM((tm, d), BF16)],
        compiler_params=_params("parallel", "arbitrary"),
        name="rms_proj_k_v",
    )(x, g, wt_head, wt_head)


def _prompt_attn_body(qit_ref, side_ref, qt_ref, ki_ref, k_ref, vt_ref, *rest,
                      tq, kc, topk, n_cast):
    w_f32_refs = rest[:n_cast]
    o_ref = rest[n_cast]
    w_bf16_refs = rest[n_cast + 1:2 * n_cast + 1]
    sc_ref, thr_ref, qtp_ref, ot_ref, m_ref, l_ref, a_ref, s_ref, p_ref = rest[2 * n_cast + 1:]
    for src, dst in zip(w_f32_refs, w_bf16_refs):
        dst[...] = src[...].astype(BF16)
    i = pl.program_id(1)
    nchunk = (i + 1) * (tq // kc)
    q0 = i * tq

    w16 = side_ref[IDX_DIM:IDX_DIM + IDX_HEADS, :] * (IDX_HEADS ** -0.5 * IDX_DIM ** -0.5)
    qt = qt_ref[...].astype(F32).reshape(N_HEADS // 2, 2 * HEAD_DIM, tq)
    first_half = lax.broadcasted_iota(I32, (2 * HEAD_DIM, tq), 0) < HEAD_DIM
    for j in range(N_HEADS // 2):
        qtp_ref[2 * j] = jnp.where(first_half, qt[j], 0.0).astype(BF16)
        qtp_ref[2 * j + 1] = jnp.where(first_half, 0.0, qt[j]).astype(BF16)

    def idx_chunk(r0, carry):
        lo8, hi8 = carry
        kic = ki_ref[pl.ds(r0, kc), 0:IDX_DIM]
        acc = jnp.zeros((kc, tq), F32)
        for h in range(IDX_HEADS):
            s = jnp.dot(kic, qit_ref[h * IDX_DIM:(h + 1) * IDX_DIM, :],
                        preferred_element_type=F32)
            acc = acc + jnp.maximum(s, 0.0) * w16[h:h + 1, :]
        krow = r0 + lax.broadcasted_iota(I32, (kc, tq), 0)
        qcol = q0 + lax.broadcasted_iota(I32, (kc, tq), 1)
        causal = krow <= qcol
        sc_ref[pl.ds(r0, kc), :] = jnp.where(causal, acc, -jnp.inf)
        lo8 = jnp.minimum(lo8, jnp.min(jnp.where(causal, acc, jnp.inf).reshape(kc // 8, 8, tq),
                                       axis=0))
        hi8 = jnp.maximum(hi8, jnp.max(jnp.where(causal, acc, -jnp.inf).reshape(kc // 8, 8, tq),
                                       axis=0))
        return lo8, hi8

    def idx_block(c, carry):
        for sub in range(tq // kc):
            carry = idx_chunk(pl.multiple_of(c * tq + sub * kc, kc), carry)
        return carry

    lo8, hi8 = lax.fori_loop(0, i + 1, idx_block,
                             (jnp.full((8, tq), jnp.inf, F32), jnp.full((8, tq), -jnp.inf, F32)))
    lo0 = jnp.min(lo8, axis=0, keepdims=True)
    hi0 = jnp.max(hi8, axis=0, keepdims=True)

    def count_rows(pred):
        def body(c, cnt):
            r0 = pl.multiple_of(c * tq, tq)
            krow = r0 + lax.broadcasted_iota(I32, (tq, tq), 0)
            hit = pred(sc_ref[pl.ds(r0, tq), :], krow).astype(I32)
            return cnt + jnp.sum(hit.reshape(tq // 8, 8, tq), axis=0)

        cnt8 = lax.fori_loop(0, i + 1, body, jnp.zeros((8, tq), I32))
        return jnp.sum(cnt8, axis=0, keepdims=True)

    def count_ge(t):
        return count_rows(lambda s, _: s >= t)

    def bisect(_, state):
        lo, hi, n_lo = state
        mid = 0.5 * lo + 0.5 * hi
        cnt = count_ge(mid)
        take = cnt >= topk
        return jnp.where(take, mid, lo), jnp.where(take, hi, mid), jnp.where(take, cnt, n_lo)

    def unresolved(state):
        lo, hi, n_lo = state
        mid = 0.5 * lo + 0.5 * hi
        return (n_lo > topk) & (mid > lo) & (mid < hi)

    n_causal = jnp.minimum(q0 + lax.broadcasted_iota(I32, (1, tq), 1) + 1, (i + 1) * tq)
    state = lax.fori_loop(0, SEARCH_FIXED_STEPS, bisect, (lo0, hi0, n_causal))

    def more_steps(carry):
        it, st = carry
        return jnp.logical_and(it < BISECT_MAX_STEPS, jnp.max(unresolved(st).astype(I32)) > 0)

    def step_group(carry):
        it, st = carry
        return it + SEARCH_GROUP, lax.fori_loop(0, SEARCH_GROUP, bisect, st)

    _, (lo, hi, n_lo) = lax.while_loop(more_steps, step_group,
                                       (jnp.int32(SEARCH_FIXED_STEPS), state))
    thr_ref[...] = lo

    has_tie = jnp.max((n_lo > topk).astype(I32)) > 0

    @pl.when(has_tie)
    def _():
        n_hi = count_ge(hi)
        thr = jnp.where((n_lo > topk) & (n_hi >= topk), hi, lo)
        thr_ref[...] = thr
        need = topk - count_rows(lambda s, _: s > thr)

        def jbit(bi, jlo):
            cand = jlo | jnp.left_shift(jnp.int32(1), 29 - bi)
            below = count_rows(lambda s, krow: (s == thr) & (krow < cand))
            return jnp.where(below < need, cand, jlo)

        jstar = lax.fori_loop(0, 30, jbit, jnp.zeros((1, tq), I32))

        def drop(c, carry):
            r0 = pl.multiple_of(c * kc, kc)
            krow = r0 + lax.broadcasted_iota(I32, (kc, tq), 0)
            s = sc_ref[pl.ds(r0, kc), :]
            sc_ref[pl.ds(r0, kc), :] = jnp.where((s == thr) & (krow > jstar), -jnp.inf, s)
            return carry

        lax.fori_loop(0, nchunk, drop, 0)

    tsel = thr_ref[...]

    m_ref[...] = jnp.full(m_ref.shape, NEG_BIG, F32)
    l_ref[...] = jnp.zeros(l_ref.shape, F32)
    ot_ref[...] = jnp.zeros(ot_ref.shape, F32)

    ka = s_ref.shape[1]

    def chunk_body(c, carry):
        r0 = pl.multiple_of(c * ka, ka)
        bias = jnp.where(sc_ref[pl.ds(r0, ka), :] >= tsel, 0.0, -jnp.inf)
        for h in range(N_HEADS):
            pair = slice((h // 2) * 2 * HEAD_DIM, (h // 2 + 1) * 2 * HEAD_DIM)
            s_ref[h] = jnp.dot(k_ref[pl.ds(r0, ka), pair], qtp_ref[h],
                               preferred_element_type=F32) + bias
        for h in range(N_HEADS):
            s = s_ref[h]
            m_old = m_ref[h]
            m_new = jnp.maximum(m_old, jnp.max(s, axis=0, keepdims=True))
            p_ref[h] = jnp.exp2(s - m_new).astype(BF16)
            a_ref[h] = jnp.exp2(m_old - m_new)
            m_ref[h] = m_new
        for h in range(N_HEADS):
            pv = jnp.dot(vt_ref[h, :, pl.ds(r0, ka)], p_ref[h], preferred_element_type=F32)
            ot_ref[h] = a_ref[h] * ot_ref[h] + pv[0:HEAD_DIM]
            l_ref[h] = a_ref[h] * l_ref[h] + pv[HEAD_DIM:HEAD_DIM + 1]
        return carry

    lax.fori_loop(0, (i + 1) * (tq // ka), chunk_body, 0)
    o = ot_ref[...] / l_ref[...]
    o_ref[...] = o.reshape(ATTN_WIDTH, tq).T.astype(BF16)


def _prompt_attention(qqi_t, side_t, ki_nat, k_nat, vt, cast_weights, *, batch, seq, tq, kc, topk):
    nq = seq // tq
    steps = batch * nq
    body = functools.partial(_prompt_attn_body, tq=tq, kc=kc, topk=topk,
                             n_cast=len(cast_weights))
    once = pl.Buffered(1)

    def slab_spec(w):
        rows = w.shape[0] // steps
        assert w.shape[0] % steps == 0 and rows % 16 == 0
        return pl.BlockSpec((rows, w.shape[1]), lambda b, i: (b * nq + i, 0))

    slabs = [slab_spec(w) for w in cast_weights]
    return pl.pallas_call(
        body,
        grid=(batch, nq),
        in_specs=[
            pl.BlockSpec((None, None, ATTN_WIDTH, tq), lambda b, i: (1, b, 0, i)),
            pl.BlockSpec((None, LANES, tq), lambda b, i: (b, 0, i)),
            pl.BlockSpec((None, None, ATTN_WIDTH, tq), lambda b, i: (0, b, 0, i)),
            pl.BlockSpec((seq, LANES), lambda b, i: (b, 0), pipeline_mode=once),
            pl.BlockSpec((seq, ATTN_WIDTH), lambda b, i: (b, 0), pipeline_mode=once),
            pl.BlockSpec((None, N_HEADS, V_ROWS, seq), lambda b, i: (b, 0, 0, 0),
                         pipeline_mode=once),
        ] + slabs,
        out_specs=[pl.BlockSpec((tq, ATTN_WIDTH), lambda b, i: (b * nq + i, 0))] + slabs,
        out_shape=[jax.ShapeDtypeStruct((batch * seq, ATTN_WIDTH), BF16)]
        + [jax.ShapeDtypeStruct(w.shape, BF16) for w in cast_weights],
        scratch_shapes=[
            pltpu.VMEM((seq, tq), F32),
            pltpu.VMEM((1, tq), F32),
            pltpu.VMEM((N_HEADS, 2 * HEAD_DIM, tq), BF16),
            pltpu.VMEM((N_HEADS, HEAD_DIM, tq), F32),
            pltpu.VMEM((N_HEADS, 1, tq), F32),
            pltpu.VMEM((N_HEADS, 1, tq), F32),
            pltpu.VMEM((N_HEADS, 1, tq), F32),
            pltpu.VMEM((N_HEADS, ATTN_PV_KEYS, tq), F32),
            pltpu.VMEM((N_HEADS, ATTN_PV_KEYS, tq), BF16),
        ],
        compiler_params=_params("parallel", "arbitrary"),
        name="prompt_sparse_attn",
    )(qqi_t, side_t, qqi_t, ki_nat, k_nat, vt, *cast_weights)


def _pool_prompt_rows(i, u_ref, wg_ref, sc_ref, ue_ref, *, tm):
    @pl.when(i == 0)
    def _():
        ue_ref[0:POOL_HALO, :] = jnp.zeros((POOL_HALO, ue_ref.shape[1]), F32)

    u = u_ref[...]
    ue_ref[POOL_HALO:POOL_HALO + tm, :] = u
    pos = i * tm + lax.broadcasted_iota(I32, (tm, 1), 0)
    outs = []
    for g, w in enumerate(POOL_WINDOWS):
        cols = slice(g * POOL_GROUP_WIDTH, (g + 1) * POOL_GROUP_WIDTH)
        win = u[:, cols]
        for back in range(1, w):
            win = win + ue_ref[POOL_HALO - back:POOL_HALO - back + tm, cols]
        cnt = jnp.minimum(pos + 1, w).astype(F32)
        mixed = win / cnt - u[:, cols]
        outs.append(jnp.dot(mixed.astype(BF16), wg_ref[g], preferred_element_type=F32))
    ue_ref[0:POOL_HALO, :] = u[tm - POOL_HALO:, :]
    return (jnp.concatenate(outs, axis=1) * sc_ref[...]).astype(BF16)


def _pool_sample_body(st_ref, u_ref, wg_ref, sc_ref, o_ref):
    st = st_ref[...]
    u = u_ref[...]
    n_state = st.shape[1]
    outs = []
    for g, w in enumerate(POOL_WINDOWS):
        cols = slice(g * POOL_GROUP_WIDTH, (g + 1) * POOL_GROUP_WIDTH)
        win = u[:, cols] + jnp.sum(st[:, n_state - (w - 1):, cols], axis=1)
        mixed = win / float(w) - u[:, cols]
        outs.append(jnp.dot(mixed.astype(BF16), wg_ref[g], preferred_element_type=F32))
    o_ref[...] = (jnp.concatenate(outs, axis=1) * sc_ref[...]).astype(BF16)


def _pool_sample(state, u_new, wg, scale):
    nb, width = u_new.shape
    return pl.pallas_call(
        _pool_sample_body,
        out_shape=jax.ShapeDtypeStruct((nb, width), BF16),
        compiler_params=pltpu.CompilerParams(vmem_limit_bytes=VMEM_LIMIT_BYTES),
        name="pool_sample",
    )(state, u_new, wg, scale)


def _merge_rows(pool_o, x_ref, ao_ref, ga_ref, gb_ref, bg_ref, wa_ref, wp_ref, wo_ref, g2_ref,
                h_ref, z_ref):
    ya = jnp.dot(ao_ref[...], wa_ref[...], preferred_element_type=F32)
    yb = jnp.dot(pool_o, wp_ref[...], preferred_element_type=F32)
    merged = (jax.nn.sigmoid(ga_ref[...] + bg_ref[0:1, :]) * ya
              + jax.nn.sigmoid(gb_ref[...] + bg_ref[1:2, :]) * yb)
    h = x_ref[...] + jnp.dot(merged.astype(BF16), wo_ref[...], preferred_element_type=F32)
    h_ref[...] = h
    ms = jnp.mean(h * h, axis=-1, keepdims=True)
    z_ref[...] = (h * lax.rsqrt(ms + EPS) * g2_ref[...]).astype(BF16)


def _merge_body(x_ref, ao_ref, po_ref, ga_ref, gb_ref, bg_ref, wa_ref, wp_ref, wo_ref, g2_ref,
                h_ref, z_ref):
    _merge_rows(po_ref[...], x_ref, ao_ref, ga_ref, gb_ref, bg_ref, wa_ref, wp_ref, wo_ref,
                g2_ref, h_ref, z_ref)


def _pool_merge_body(x_ref, ao_ref, u_ref, ga_ref, gb_ref, bg_ref, wg_ref, sc_ref, wa_ref, wp_ref,
                     wo_ref, g2_ref, h_ref, z_ref, ue_ref, *, tm):
    pool_o = _pool_prompt_rows(pl.program_id(1), u_ref, wg_ref, sc_ref, ue_ref, tm=tm)
    _merge_rows(pool_o, x_ref, ao_ref, ga_ref, gb_ref, bg_ref, wa_ref, wp_ref, wo_ref, g2_ref,
                h_ref, z_ref)


def _pool_merge(x, attn_o, p_main, b_gate, wg, scale, wa, wp, wo, g2, *, batch, seq, tm, u_col,
                ga_col, gb_col):
    m, d = x.shape
    nblk = seq // tm
    width = wg.shape[0] * wg.shape[1]
    rows = lambda b, i: (b * nblk + i, 0)
    const = lambda b, i: (0, 0)
    return pl.pallas_call(
        functools.partial(_pool_merge_body, tm=tm),
        grid=(batch, nblk),
        in_specs=[
            pl.BlockSpec((tm, d), rows),
            pl.BlockSpec((tm, attn_o.shape[1]), rows),
            pl.BlockSpec((tm, width), lambda b, i: (b * nblk + i, u_col)),
            pl.BlockSpec((tm, d), lambda b, i: (b * nblk + i, ga_col)),
            pl.BlockSpec((tm, d), lambda b, i: (b * nblk + i, gb_col)),
            pl.BlockSpec(b_gate.shape, const),
            pl.BlockSpec(wg.shape, lambda b, i: (0, 0, 0)),
            pl.BlockSpec((1, width), const),
            pl.BlockSpec(wa.shape, const),
            pl.BlockSpec(wp.shape, const),
            pl.BlockSpec(wo.shape, const),
            pl.BlockSpec((1, d), const),
        ],
        out_specs=[pl.BlockSpec((tm, d), rows), pl.BlockSpec((tm, d), rows)],
        out_shape=[jax.ShapeDtypeStruct((m, d), F32), jax.ShapeDtypeStruct((m, d), BF16)],
        scratch_shapes=[pltpu.VMEM((POOL_HALO + tm, width), F32)],
        compiler_params=_params("parallel", "arbitrary"),
        name="pool_gated_merge",
    )(x, attn_o, p_main, p_main, p_main, b_gate, wg, scale, wa, wp, wo, g2)


def _merge(x, attn_o, pool_o, p_main, b_gate, wa, wp, wo, g2, *, tm, ga_col, gb_col):
    m, d = x.shape
    const = lambda i: (0, 0)
    return pl.pallas_call(
        _merge_body,
        grid=(m // tm,),
        in_specs=[
            pl.BlockSpec((tm, d), lambda i: (i, 0)),
            pl.BlockSpec((tm, attn_o.shape[1]), lambda i: (i, 0)),
            pl.BlockSpec((tm, pool_o.shape[1]), lambda i: (i, 0)),
            pl.BlockSpec((tm, d), lambda i: (i, ga_col)),
            pl.BlockSpec((tm, d), lambda i: (i, gb_col)),
            pl.BlockSpec(b_gate.shape, const),
            pl.BlockSpec(wa.shape, const),
            pl.BlockSpec(wp.shape, const),
            pl.BlockSpec(wo.shape, const),
            pl.BlockSpec((1, d), const),
        ],
        out_specs=[pl.BlockSpec((tm, d), lambda i: (i, 0)), pl.BlockSpec((tm, d), lambda i: (i, 0))],
        out_shape=[jax.ShapeDtypeStruct((m, d), F32), jax.ShapeDtypeStruct((m, d), BF16)],
        compiler_params=_params("parallel"),
        name="gated_merge",
    )(x, attn_o, pool_o, p_main, p_main, b_gate, wa, wp, wo, g2)


def _ffn_body(z_ref, h_ref, wu_ref, wd_ref, gf_ref, y_ref, acc_ref):
    f = pl.program_id(1)

    @pl.when(f == 0)
    def _():
        acc_ref[...] = h_ref[...]

    a = jnp.dot(z_ref[...], wu_ref[...], preferred_element_type=F32)
    a = jnp.square(jnp.maximum(a, 0.0))
    acc_ref[...] += jnp.dot(a.astype(BF16), wd_ref[...], preferred_element_type=F32)

    @pl.when(f == pl.num_programs(1) - 1)
    def _():
        h2 = acc_ref[...]
        ms = jnp.mean(h2 * h2, axis=-1, keepdims=True)
        y_ref[...] = h2 * lax.rsqrt(ms + EPS) * gf_ref[...]


def _ffn(z, h, wu, wd, gf, *, tm, tf):
    m, d = h.shape
    dff = wu.shape[1]
    return pl.pallas_call(
        _ffn_body,
        grid=(m // tm, dff // tf),
        in_specs=[
            pl.BlockSpec((tm, d), lambda i, f: (i, 0)),
            pl.BlockSpec((tm, d), lambda i, f: (i, 0)),
            pl.BlockSpec((d, tf), lambda i, f: (0, f)),
            pl.BlockSpec((tf, d), lambda i, f: (f, 0)),
            pl.BlockSpec((1, d), lambda i, f: (0, 0)),
        ],
        out_specs=pl.BlockSpec((tm, d), lambda i, f: (i, 0)),
        out_shape=jax.ShapeDtypeStruct((m, d), F32),
        scratch_shapes=[pltpu.VMEM((tm, d), F32)],
        compiler_params=_params("parallel", "arbitrary"),
        name="ffn_final_norm",
    )(z, h, wu, wd, gf)


def _sample_scores_body(pt_ref, qi_ref, w_ref, newpage_ref, *rest, n_pages_step):
    page_refs = rest[:n_pages_step]
    o_ref, onew_ref = rest[n_pages_step:]
    qi = qi_ref[...].astype(BF16)
    wcol = w_ref[...] * (IDX_HEADS ** -0.5 * IDX_DIM ** -0.5)

    def score(pages_t):
        s = jnp.dot(qi, pages_t.astype(BF16), preferred_element_type=F32)
        return jnp.sum(jnp.maximum(s, 0.0) * wcol, axis=0, keepdims=True)

    o_ref[...] = score(jnp.concatenate([r[...] for r in page_refs], axis=1))
    onew_ref[...] = score(newpage_ref[...])


def _sample_scores(page_table, qi, w, newpage, kidx_t, *, pages_step):
    nb, n_pages = page_table.shape
    page = kidx_t.shape[3]
    steps = n_pages // pages_step

    def page_spec(j):
        return pl.BlockSpec((None, None, IDX_DIM, page),
                            lambda b, g, pt: (0, pt[b, g * pages_step + j], 0, 0))

    grid_spec = pltpu.PrefetchScalarGridSpec(
        num_scalar_prefetch=1,
        grid=(nb, steps),
        in_specs=[
            pl.BlockSpec((None, IDX_HEADS, IDX_DIM), lambda b, g, pt: (b, 0, 0)),
            pl.BlockSpec((None, IDX_HEADS, 1), lambda b, g, pt: (b, 0, 0)),
            pl.BlockSpec((None, IDX_DIM, page), lambda b, g, pt: (b, 0, 0)),
        ] + [page_spec(j) for j in range(pages_step)],
        out_specs=[
            pl.BlockSpec((None, 1, pages_step * page), lambda b, g, pt: (b, 0, g)),
            pl.BlockSpec((None, 1, page), lambda b, g, pt: (b, 0, 0)),
        ],
    )
    return pl.pallas_call(
        functools.partial(_sample_scores_body, n_pages_step=pages_step),
        grid_spec=grid_spec,
        out_shape=[jax.ShapeDtypeStruct((nb, 1, n_pages * page), F32),
                   jax.ShapeDtypeStruct((nb, 1, page), F32)],
        compiler_params=_params("parallel", "arbitrary"),
        name="sample_indexer_scores",
    )(page_table, qi, w, newpage, *([kidx_t] * pages_step))


def _prefix_exclusive(mask_f32, tri):
    rows, length = mask_f32.shape
    off = jnp.zeros((rows, 1), F32)
    parts = []
    for c in range(length // LANES):
        mc = mask_f32[:, c * LANES:(c + 1) * LANES]
        inc = jnp.dot(mc.astype(BF16), tri, preferred_element_type=F32)
        parts.append(inc - mc + off)
        off = off + inc[:, LANES - 1:LANES]
    return jnp.concatenate(parts, axis=1)


def _sample_select_body(sc_ref, scnew_ref, bias_ref, biasnew_ref, *, topk):
    past = sc_ref.shape[1]
    sc = jnp.concatenate([sc_ref[...], scnew_ref[...]], axis=1)
    nb, length = sc.shape
    pos = lax.broadcasted_iota(I32, (nb, length), 1)
    valid = pos <= past
    scv = jnp.where(valid, sc, -jnp.inf)

    def count_ge(t):
        return jnp.sum((scv >= t).astype(I32), axis=1, keepdims=True)

    lo0 = jnp.min(jnp.where(valid, sc, jnp.inf), axis=1, keepdims=True)
    hi0 = jnp.max(scv, axis=1, keepdims=True)

    def unresolved(lo, hi, n_lo):
        mid = 0.5 * lo + 0.5 * hi
        return (n_lo > topk) & (mid > lo) & (mid < hi)

    def more(carry):
        it, lo, hi, n_lo = carry
        return jnp.logical_and(it < BISECT_MAX_STEPS,
                               jnp.max(unresolved(lo, hi, n_lo).astype(I32)) > 0)

    def step(carry):
        it, lo, hi, n_lo = carry
        mid = 0.5 * lo + 0.5 * hi
        cnt = count_ge(mid)
        take = cnt >= topk
        return it + 1, jnp.where(take, mid, lo), jnp.where(take, hi, mid), jnp.where(take, cnt, n_lo)

    _, lo, hi, n_lo = lax.while_loop(more, step, (jnp.int32(0), lo0, hi0, count_ge(lo0)))
    thr = jnp.where(count_ge(hi) >= topk, hi, lo)
    gt = scv > thr
    eq = (scv == thr) & valid
    tri = (lax.broadcasted_iota(I32, (LANES, LANES), 0)
           <= lax.broadcasted_iota(I32, (LANES, LANES), 1)).astype(BF16)
    need = (topk - jnp.sum(gt.astype(I32), axis=1, keepdims=True)).astype(F32)
    eq_rank = _prefix_exclusive(eq.astype(F32), tri)
    sel = (gt | (eq & (eq_rank < need))) & valid
    bias = jnp.where(sel, 0.0, -jnp.inf)
    bias_ref[...] = bias[:, :past]
    biasnew_ref[...] = bias[:, past:]


def _sample_select(scores, score_new, *, topk):
    nb, past = scores.shape
    return pl.pallas_call(
        functools.partial(_sample_select_body, topk=topk),
        out_shape=[jax.ShapeDtypeStruct((nb, past), F32),
                   jax.ShapeDtypeStruct(score_new.shape, F32)],
        compiler_params=pltpu.CompilerParams(vmem_limit_bytes=VMEM_LIMIT_BYTES),
        name="sample_topk_select",
    )(scores, score_new)


def _heads_on_lanes_to_slabs(x_t, page):
    return jnp.stack([jnp.broadcast_to(x_t[:, h:h + 1], (HEAD_DIM, page))
                      for h in range(N_HEADS)])


def _lane_softmax_update(scores, values, m_ref, l_ref, acc_ref):
    m_old = m_ref[...]
    m_new = m_old
    for s in scores:
        m_new = jnp.maximum(m_new, s)
    alpha = jnp.exp2(m_old - m_new)
    l = alpha * l_ref[...]
    acc = alpha * acc_ref[...]
    for s, v in zip(scores, values):
        p = jnp.exp2(s - m_new)
        l = l + p
        acc = acc + p * v
    m_ref[...] = m_new
    l_ref[...] = l
    acc_ref[...] = acc


def _attend_pages(qt_ref, bias_ref, k_refs, v_refs, m_ref, l_ref, acc_ref, *, page):
    q = _heads_on_lanes_to_slabs(qt_ref[...], page)
    scores = []
    for j, k_ref in enumerate(k_refs):
        s = jnp.sum(k_ref[...] * q, axis=1, keepdims=True)
        scores.append(s + bias_ref[:, j * page:(j + 1) * page][None])
    _lane_softmax_update(scores, [v_ref[...] for v_ref in v_refs], m_ref, l_ref, acc_ref)


def _attend_finish(qt_ref, knt_ref, vnt_ref, biasnew_ref, o_ref, m_ref, l_ref, acc_ref, *, page):
    q = _heads_on_lanes_to_slabs(qt_ref[...], page)
    kn = _heads_on_lanes_to_slabs(knt_ref[...], page)
    s_new = jnp.sum(kn * q, axis=1, keepdims=True) + biasnew_ref[...][None]
    _lane_softmax_update([s_new], [_heads_on_lanes_to_slabs(vnt_ref[...], page)],
                         m_ref, l_ref, acc_ref)
    m = m_ref[...]
    f = jnp.exp2(m - jnp.max(m, axis=-1, keepdims=True))
    denom = jnp.sum(l_ref[...] * f, axis=-1, keepdims=True)
    o_ref[...] = jnp.sum(acc_ref[...] * f, axis=-1, keepdims=True) / denom


def _ffn_attend_body(pt_ref, z_ref, h_ref, wu_ref, wd_ref, gf_ref,
                     qt_ref, knt_ref, vnt_ref, bias_ref, biasnew_ref, *rest,
                     n_pages_step, page, steps_per_sample):
    k_refs = rest[:n_pages_step]
    v_refs = rest[n_pages_step:2 * n_pages_step]
    y_ref, o_ref, m_ref, l_ref, pacc_ref = rest[2 * n_pages_step:]
    acc_ref = y_ref
    f = pl.program_id(1)
    g = (pl.program_id(0) * pl.num_programs(1) + f) % steps_per_sample

    @pl.when(f == 0)
    def _():
        acc_ref[...] = h_ref[...]

    @pl.when(g == 0)
    def _():
        m_ref[...] = jnp.full(m_ref.shape, NEG_BIG, F32)
        l_ref[...] = jnp.zeros(l_ref.shape, F32)
        pacc_ref[...] = jnp.zeros(pacc_ref.shape, F32)

    a = jnp.dot(z_ref[...], wu_ref[...], preferred_element_type=F32)
    a = jnp.square(jnp.maximum(a, 0.0))
    acc_ref[...] += jnp.dot(a.astype(BF16), wd_ref[...], preferred_element_type=F32)
    _attend_pages(qt_ref, bias_ref, k_refs, v_refs, m_ref, l_ref, pacc_ref, page=page)

    @pl.when(f == pl.num_programs(1) - 1)
    def _():
        h2 = acc_ref[...]
        ms = jnp.mean(h2 * h2, axis=-1, keepdims=True)
        y_ref[...] = h2 * lax.rsqrt(ms + EPS) * gf_ref[...]

    @pl.when(g == steps_per_sample - 1)
    def _():
        _attend_finish(qt_ref, knt_ref, vnt_ref, biasnew_ref, o_ref, m_ref, l_ref, pacc_ref,
                       page=page)


def _ffn_attend(z, h, wu, wd, gf, page_table, qt, knt, vnt, bias, bias_new, ck_t, cv_t, *, tm, tf):
    m, d = h.shape
    dff = wu.shape[1]
    nb, n_pages = page_table.shape
    page = ck_t.shape[4]
    ni, nf = m // tm, dff // tf
    total_steps = ni * nf
    assert (nb * n_pages) % total_steps == 0
    pages_step = nb * n_pages // total_steps
    assert n_pages % pages_step == 0
    sps = n_pages // pages_step

    def sample(i, f):
        return (i * nf + f) // sps

    def page_spec(j):
        def index(i, f, pt):
            t = i * nf + f
            return (0, pt[t // sps, (t % sps) * pages_step + j], 0, 0, 0)

        return pl.BlockSpec((None, None, N_HEADS, HEAD_DIM, page), index)

    pages = [page_spec(j) for j in range(pages_step)]
    tq_block = pl.BlockSpec((None, HEAD_DIM, N_HEADS), lambda i, f, pt: (sample(i, f), 0, 0))
    grid_spec = pltpu.PrefetchScalarGridSpec(
        num_scalar_prefetch=1,
        grid=(ni, nf),
        in_specs=[
            pl.BlockSpec((tm, d), lambda i, f, pt: (i, 0)),
            pl.BlockSpec((tm, d), lambda i, f, pt: (i, 0), pipeline_mode=pl.Buffered(1)),
            pl.BlockSpec((d, tf), lambda i, f, pt: (0, f)),
            pl.BlockSpec((tf, d), lambda i, f, pt: (f, 0)),
            pl.BlockSpec((1, d), lambda i, f, pt: (0, 0)),
            tq_block, tq_block, tq_block,
            pl.BlockSpec((None, 1, pages_step * page),
                         lambda i, f, pt: (sample(i, f), 0, (i * nf + f) % sps)),
            pl.BlockSpec((None, 1, page), lambda i, f, pt: (sample(i, f), 0, 0)),
        ] + pages + pages,
        out_specs=[
            pl.BlockSpec((tm, d), lambda i, f, pt: (i, 0)),
            pl.BlockSpec((None, N_HEADS, HEAD_DIM, 1), lambda i, f, pt: (sample(i, f), 0, 0, 0)),
        ],
        scratch_shapes=[
            pltpu.VMEM((N_HEADS, 1, page), F32),
            pltpu.VMEM((N_HEADS, 1, page), F32),
            pltpu.VMEM((N_HEADS, HEAD_DIM, page), F32),
        ],
    )
    return pl.pallas_call(
        functools.partial(_ffn_attend_body, n_pages_step=pages_step, page=page,
                          steps_per_sample=sps),
        grid_spec=grid_spec,
        out_shape=[jax.ShapeDtypeStruct((m, d), F32),
                   jax.ShapeDtypeStruct((nb, N_HEADS, HEAD_DIM, 1), F32)],
        compiler_params=_params("arbitrary", "arbitrary"),
        name="ffn_with_paged_attend",
    )(page_table, z, h, wu, wd, gf, qt, knt, vnt, bias, bias_new,
      *([ck_t] * pages_step), *([cv_t] * pages_step))


def kernel(x_prompt, x_sample, cache_k, cache_v, cache_kidx, state_pool, page_table, norm_mix_g,
           w_in, b_gate, w_attn_out, w_pool_group, pool_scale, w_pool_out, w_out, norm_ffn_g,
           w_up, w_down, norm_final_g):
    depth = w_in.shape[0]
    assert depth == 1, "single-layer step"
    batch, seq, d_model = x_prompt.shape
    dec_batch, dec_seq, _ = x_sample.shape
    assert dec_seq == 1, "one new token per sample"
    n_pages = page_table.shape[1]
    page = cache_k.shape[2]
    past = n_pages * page
    pool_width = pool_scale.shape[1]
    score_pages = min(SCORE_PAGES, n_pages)
    assert page == LANES and n_pages % score_pages == 0
    l = 0

    tn = PROJ_COLS
    assert ATTN_WIDTH == tn and pool_width == tn and d_model == 2 * tn and IDX_HEADS * IDX_DIM == tn
    c_ki = 3 * ATTN_WIDTH + IDX_HEADS * IDX_DIM
    c_u = c_ki + IDX_DIM + IDX_HEADS
    c_ga = c_u + pool_width
    n_head_rows = c_ki + LANES
    wt_bf = jnp.swapaxes(w_in[l], 0, 1).astype(BF16)
    n_tail = wt_bf.shape[0] - c_u
    tail_shift = (n_tail - pool_width) // tn
    K_COL, V_COL, QI_COL = 1, 2, 3
    SIDE_BLOCK = c_ki // LANES
    GA_COL, GB_COL, U_COL = 0, 1, 4
    g_mix = norm_mix_g[l][None, :]
    wg = w_pool_group[l].astype(BF16)
    g_ffn = norm_ffn_g[l][None, :]
    g_fin = norm_final_g[None, :]
    scale = pool_scale[l][None, :]

    def col(p, c, width=tn):
        return p[:, c * tn:c * tn + width]

    xp = x_prompt.reshape(batch * seq, d_model)
    tm_p = PROJ_ROWS
    tm_wide = PROJ_ROWS_WIDE if seq % PROJ_ROWS_WIDE == 0 else PROJ_ROWS
    assert seq % tm_p == 0
    qqi_t = _proj_qqi(xp, g_mix, wt_bf, batch=batch, seq=seq, tm=tm_wide,
                      row_blocks=(0, QI_COL))
    k_t, v_t, k_nat, vt_bf, side_t, ki_nat = _proj_kv(xp, g_mix, wt_bf, batch=batch, seq=seq,
                                                      tm=tm_p, side_block=SIDE_BLOCK)
    pp = _proj_nat(xp, g_mix, wt_bf, tm_wide, tn, row_start=c_u, n=n_tail, out_shift=tail_shift)

    tq, kc = ATTN_QUERIES, ATTN_KEYS
    topk_p = min(TOPK_MAX, seq // 4)
    attn_p, wa, wp, wo, wu, wd = _prompt_attention(
        qqi_t, side_t, ki_nat, k_nat, vt_bf,
        [w_attn_out[l], w_pool_out[l], w_out[l], w_up[l], w_down[l]],
        batch=batch, seq=seq, tq=tq, kc=kc, topk=topk_p)
    h_p, z_p = _pool_merge(xp, attn_p, pp, b_gate[l], wg, scale, wa, wp, wo, g_ffn, batch=batch,
                           seq=seq, tm=MERGE_ROWS, u_col=U_COL, ga_col=GA_COL, gb_col=GB_COL)

    xs = x_sample.reshape(dec_batch, d_model)
    head_tile = max(t for t in (1024, 768, 512, 384, 256, 128) if n_head_rows % t == 0)
    ps_head = _proj_nat(xs, g_mix, wt_bf, dec_batch, head_tile, n=n_head_rows)
    ps = _proj_nat(xs, g_mix, wt_bf, dec_batch, tn, row_start=c_u, n=n_tail,
                   out_shift=tail_shift)
    topk_s = min(TOPK_MAX, (past + dec_seq) // 4)
    q_s = col(ps_head, 0) * Q_SCALE
    k_s = col(ps_head, K_COL)
    v_s = col(ps_head, V_COL)
    ki_s = ps_head[:, c_ki:c_ki + IDX_DIM]
    wi_s = ps_head[:, c_ki + IDX_DIM:c_u]
    u_s = col(ps, U_COL)
    qi_s = col(ps_head, QI_COL).reshape(dec_batch, IDX_HEADS, IDX_DIM)
    kidx_t = jnp.swapaxes(cache_kidx, 2, 3)
    ck_t = jnp.transpose(cache_k, (0, 1, 3, 4, 2))
    cv_t = jnp.transpose(cache_v, (0, 1, 3, 4, 2))
    newpage = jnp.zeros((dec_batch, IDX_DIM, page), F32).at[:, :, 0].set(ki_s)
    scores, score_new = _sample_scores(page_table, qi_s, wi_s[:, :, None], newpage, kidx_t,
                                       pages_step=score_pages)
    bias, bias_new = _sample_select(scores.reshape(dec_batch, past),
                                    score_new.reshape(dec_batch, page), topk=topk_s)

    def dims_by_heads(t):
        return t.reshape(dec_batch, N_HEADS, HEAD_DIM).transpose(0, 2, 1)

    y_p, attn_s = _ffn_attend(z_p, h_p, wu, wd, g_fin, page_table, dims_by_heads(q_s),
                              dims_by_heads(k_s), dims_by_heads(v_s), bias[:, None, :],
                              bias_new[:, None, :], ck_t, cv_t, tm=FFN_ROWS, tf=FFN_COLS)
    pool_s = _pool_sample(state_pool[l], u_s, wg, scale)
    h_s, z_s = _merge(xs, attn_s.reshape(dec_batch, ATTN_WIDTH).astype(BF16), pool_s, ps,
                      b_gate[l], wa, wp, wo, g_ffn, tm=dec_batch, ga_col=GA_COL, gb_col=GB_COL)
    y_s = _ffn(z_s, h_s, wu, wd, g_fin, tm=dec_batch, tf=FFN_COLS)

    n_state = state_pool.shape[2]
    u_p = pp.reshape(batch, seq, -1)[:, -n_state:, U_COL * tn:(U_COL + 1) * tn]
    pool_state_s = jnp.concatenate([state_pool[l].astype(F32), u_s[:, None, :]], axis=1)[:, -n_state:]

    def heads_last(t):
        return t.reshape(batch, N_HEADS, HEAD_DIM, seq).transpose(0, 3, 1, 2)[None]

    return (
        y_p.reshape(batch, seq, d_model),
        y_s.reshape(dec_batch, dec_seq, d_model),
        heads_last(k_t),
        heads_last(v_t),
        side_t[:, :IDX_DIM, :].transpose(0, 2, 1)[None],
        u_p[None],
        k_s.reshape(1, dec_batch, dec_seq, N_HEADS, HEAD_DIM),
        v_s.reshape(1, dec_batch, dec_seq, N_HEADS, HEAD_DIM),
        ki_s.reshape(1, dec_batch, dec_seq, IDX_DIM),
        pool_state_s[None],
    )
```

```python
import functools

import jax
import jax.numpy as jnp
from jax import lax
from jax.experimental import pallas as pl
from jax.experimental.pallas import tpu as pltpu

F32 = jnp.float32
BF16 = jnp.bfloat16
I32 = jnp.int32

N_HEADS = 16
HEAD_DIM = 64
ATTN_WIDTH = N_HEADS * HEAD_DIM
IDX_HEADS = 16
IDX_DIM = 64
TOPK_MAX = 256
POOL_WINDOWS = (2, 4, 8, 16)
POOL_GROUP_WIDTH = 256
POOL_HALO = 16
EPS = 1e-6
LANES = 128
NEG_BIG = -1e30
Q_SCALE = HEAD_DIM ** -0.5 * 1.4426950408889634
V_ROWS = HEAD_DIM + 16

VMEM_LIMIT_BYTES = 56 * 1024 * 1024

PROJ_ROWS = 512
PROJ_ROWS_WIDE = 1024
PROJ_COLS = 1024
ATTN_QUERIES = 256
ATTN_KEYS = 128
ATTN_PV_KEYS = 256
MERGE_ROWS = 256
FFN_ROWS = 512
FFN_COLS = 512
SCORE_PAGES = 64
SEARCH_FIXED_STEPS = 12
SEARCH_GROUP = 4
BISECT_MAX_STEPS = 512


def _params(*sem):
    return pltpu.CompilerParams(dimension_semantics=sem, vmem_limit_bytes=VMEM_LIMIT_BYTES)


_CONTRACT_LAST = (((1,), (1,)), ((), ()))


def _rms_to_bf16(x_ref, g_ref, z_ref):
    x = x_ref[...]
    ms = jnp.mean(x * x, axis=-1, keepdims=True)
    z_ref[...] = (x * lax.rsqrt(ms + EPS) * g_ref[...]).astype(BF16)


def _proj_nat_body(x_ref, g_ref, wt_ref, o_ref, z_ref):
    @pl.when(pl.program_id(1) == 0)
    def _():
        _rms_to_bf16(x_ref, g_ref, z_ref)

    o_ref[...] = lax.dot_general(z_ref[...], wt_ref[...], _CONTRACT_LAST,
                                 preferred_element_type=F32)


def _proj_nat(x, g, wt, tm, tn, *, row_start=0, n=None, out_shift=0):
    m, d = x.shape
    n = wt.shape[0] if n is None else n
    nj = n // tn
    return pl.pallas_call(
        _proj_nat_body,
        grid=(m // tm, nj),
        in_specs=[
            pl.BlockSpec((tm, d), lambda i, j: (i, 0)),
            pl.BlockSpec((1, d), lambda i, j: (0, 0)),
            pl.BlockSpec((pl.Element(tn), pl.Element(d)),
                         lambda i, j: (pl.multiple_of(row_start + j * tn, 16), 0)),
        ],
        out_specs=pl.BlockSpec((tm, tn), lambda i, j: (i, (j + out_shift) % nj)),
        out_shape=jax.ShapeDtypeStruct((m, n), F32),
        scratch_shapes=[pltpu.VMEM((tm, d), BF16)],
        compiler_params=_params("parallel", "arbitrary"),
        name="rms_proj_rows",
    )(x, g, wt)


def _proj_qqi_body(x_ref, g_ref, wt_ref, o_ref, z_ref):
    @pl.when(pl.program_id(1) == 0)
    def _():
        _rms_to_bf16(x_ref, g_ref, z_ref)

    scale = jnp.where(pl.program_id(1) == 0, Q_SCALE, 1.0)
    o = lax.dot_general(wt_ref[...], z_ref[...], _CONTRACT_LAST, preferred_element_type=F32)
    o_ref[...] = (o * scale).astype(BF16)


def _proj_qqi(x, g, wt_head, *, batch, seq, tm, row_blocks):
    d = x.shape[1]
    nblk = seq // tm
    blocks = tuple(row_blocks)
    assert blocks == (0, 3)
    return pl.pallas_call(
        _proj_qqi_body,
        grid=(batch * nblk, len(blocks)),
        in_specs=[
            pl.BlockSpec((tm, d), lambda i, j: (i, 0)),
            pl.BlockSpec((1, d), lambda i, j: (0, 0)),
            pl.BlockSpec((ATTN_WIDTH, d), lambda i, j: (3 * j, 0)),
        ],
        out_specs=pl.BlockSpec((None, None, ATTN_WIDTH, tm),
                               lambda i, j: (j, i // nblk, 0, i % nblk)),
        out_shape=jax.ShapeDtypeStruct((len(blocks), batch, ATTN_WIDTH, seq), BF16),
        scratch_shapes=[pltpu.VMEM((tm, d), BF16)],
        compiler_params=_params("parallel", "arbitrary"),
        name="rms_proj_q_qi",
    )(x, g, wt_head)


def _proj_kv_body(x_ref, g_ref, wt_ref, ws_ref, kt_ref, vt_ref, knat_ref, vtb_ref, sidet_ref,
                  kinat_ref, z_ref):
    j = pl.program_id(1)

    @pl.when(j == 0)
    def _():
        _rms_to_bf16(x_ref, g_ref, z_ref)
        side = lax.dot_general(ws_ref[...], z_ref[...], _CONTRACT_LAST,
                               preferred_element_type=F32)
        sidet_ref[...] = side
        kinat_ref[...] = side.T.astype(BF16)

    o = lax.dot_general(wt_ref[...], z_ref[...], _CONTRACT_LAST, preferred_element_type=F32)

    @pl.when(j == 0)
    def _():
        kt_ref[...] = o
        knat_ref[...] = o.T.astype(BF16)

    @pl.when(j == 1)
    def _():
        vt_ref[...] = o
        tm = o.shape[1]
        vtb_ref[:, 0:HEAD_DIM, :] = o.reshape(N_HEADS, HEAD_DIM, tm).astype(BF16)
        row = lax.broadcasted_iota(I32, (N_HEADS, V_ROWS - HEAD_DIM, tm), 1)
        vtb_ref[:, HEAD_DIM:V_ROWS, :] = jnp.where(row == 0, 1.0, 0.0).astype(BF16)


def _proj_kv(x, g, wt_head, *, batch, seq, tm, side_block):
    d = x.shape[1]
    nblk = seq // tm
    tok = batch * seq
    bi = lambda i, j: (i // nblk, 0, i % nblk)
    return pl.pallas_call(
        _proj_kv_body,
        grid=(batch * nblk, 2),
        in_specs=[
            pl.BlockSpec((tm, d), lambda i, j: (i, 0)),
            pl.BlockSpec((1, d), lambda i, j: (0, 0)),
            pl.BlockSpec((ATTN_WIDTH, d), lambda i, j: (1 + j, 0)),
            pl.BlockSpec((LANES, d), lambda i, j: (side_block, 0)),
        ],
        out_specs=[
            pl.BlockSpec((None, ATTN_WIDTH, tm), bi),
            pl.BlockSpec((None, ATTN_WIDTH, tm), bi),
            pl.BlockSpec((tm, ATTN_WIDTH), lambda i, j: (i, 0)),
            pl.BlockSpec((None, N_HEADS, V_ROWS, tm), lambda i, j: (i // nblk, 0, 0, i % nblk)),
            pl.BlockSpec((None, LANES, tm), bi),
            pl.BlockSpec((tm, LANES), lambda i, j: (i, 0)),
        ],
        out_shape=[
            jax.ShapeDtypeStruct((batch, ATTN_WIDTH, seq), F32),
            jax.ShapeDtypeStruct((batch, ATTN_WIDTH, seq), F32),
            jax.ShapeDtypeStruct((tok, ATTN_WIDTH), BF16),
            jax.ShapeDtypeStruct((batch, N_HEADS, V_ROWS, seq), BF16),
            jax.ShapeDtypeStruct((batch, LANES, seq), F32),
            jax.ShapeDtypeStruct((tok, LANES), BF16),
        ],
        scratch_shapes=[pltpu.VMEM((tm, d), BF16)],
        compiler_params=_params("parallel", "arbitrary"),
        name="rms_proj_k_v",
    )(x, g, wt_head, wt_head)


def _prompt_attn_body(qit_ref, side_ref, qt_ref, ki_ref, k_ref, vt_ref, *rest,
                      tq, kc, topk, n_cast):
    w_f32_refs = rest[:n_cast]
    o_ref = rest[n_cast]
    w_bf16_refs = rest[n_cast + 1:2 * n_cast + 1]
    sc_ref, thr_ref, qtp_ref, ot_ref, m_ref, l_ref, a_ref, s_ref, p_ref = rest[2 * n_cast + 1:]
    for src, dst in zip(w_f32_refs, w_bf16_refs):
        dst[...] = src[...].astype(BF16)
    i = pl.program_id(1)
    nchunk = (i + 1) * (tq // kc)
    q0 = i * tq

    w16 = side_ref[IDX_DIM:IDX_DIM + IDX_HEADS, :] * (IDX_HEADS ** -0.5 * IDX_DIM ** -0.5)
    qt = qt_ref[...].astype(F32).reshape(N_HEADS // 2, 2 * HEAD_DIM, tq)
    first_half = lax.broadcasted_iota(I32, (2 * HEAD_DIM, tq), 0) < HEAD_DIM
    for j in range(N_HEADS // 2):
        qtp_ref[2 * j] = jnp.where(first_half, qt[j], 0.0).astype(BF16)
        qtp_ref[2 * j + 1] = jnp.where(first_half, 0.0, qt[j]).astype(BF16)

    def idx_chunk(r0, carry):
        lo8, hi8 = carry
        kic = ki_ref[pl.ds(r0, kc), 0:IDX_DIM]
        acc = jnp.zeros((kc, tq), F32)
        for h in range(IDX_HEADS):
            s = jnp.dot(kic, qit_ref[h * IDX_DIM:(h + 1) * IDX_DIM, :],
                        preferred_element_type=F32)
            acc = acc + jnp.maximum(s, 0.0) * w16[h:h + 1, :]
        krow = r0 + lax.broadcasted_iota(I32, (kc, tq), 0)
        qcol = q0 + lax.broadcasted_iota(I32, (kc, tq), 1)
        causal = krow <= qcol
        sc_ref[pl.ds(r0, kc), :] = jnp.where(causal, acc, -jnp.inf)
        lo8 = jnp.minimum(lo8, jnp.min(jnp.where(causal, acc, jnp.inf).reshape(kc // 8, 8, tq),
                                       axis=0))
        hi8 = jnp.maximum(hi8, jnp.max(jnp.where(causal, acc, -jnp.inf).reshape(kc // 8, 8, tq),
                                       axis=0))
        return lo8, hi8

    def idx_block(c, carry):
        for sub in range(tq // kc):
            carry = idx_chunk(pl.multiple_of(c * tq + sub * kc, kc), carry)
        return carry

    lo8, hi8 = lax.fori_loop(0, i + 1, idx_block,
                             (jnp.full((8, tq), jnp.inf, F32), jnp.full((8, tq), -jnp.inf, F32)))
    lo0 = jnp.min(lo8, axis=0, keepdims=True)
    hi0 = jnp.max(hi8, axis=0, keepdims=True)

    def count_rows(pred):
        def body(c, cnt):
            r0 = pl.multiple_of(c * tq, tq)
            krow = r0 + lax.broadcasted_iota(I32, (tq, tq), 0)
            hit = pred(sc_ref[pl.ds(r0, tq), :], krow).astype(I32)
            return cnt + jnp.sum(hit.reshape(tq // 8, 8, tq), axis=0)

        cnt8 = lax.fori_loop(0, i + 1, body, jnp.zeros((8, tq), I32))
        return jnp.sum(cnt8, axis=0, keepdims=True)

    def count_ge(t):
        return count_rows(lambda s, _: s >= t)

    def bisect(_, state):
        lo, hi, n_lo = state
        mid = 0.5 * lo + 0.5 * hi
        cnt = count_ge(mid)
        take = cnt >= topk
        return jnp.where(take, mid, lo), jnp.where(take, hi, mid), jnp.where(take, cnt, n_lo)

    def unresolved(state):
        lo, hi, n_lo = state
        mid = 0.5 * lo + 0.5 * hi
        return (n_lo > topk) & (mid > lo) & (mid < hi)

    n_causal = jnp.minimum(q0 + lax.broadcasted_iota(I32, (1, tq), 1) + 1, (i + 1) * tq)
    state = lax.fori_loop(0, SEARCH_FIXED_STEPS, bisect, (lo0, hi0, n_causal))

    def more_steps(carry):
        it, st = carry
        return jnp.logical_and(it < BISECT_MAX_STEPS, jnp.max(unresolved(st).astype(I32)) > 0)

    def step_group(carry):
        it, st = carry
        return it + SEARCH_GROUP, lax.fori_loop(0, SEARCH_GROUP, bisect, st)

    _, (lo, hi, n_lo) = lax.while_loop(more_steps, step_group,
                                       (jnp.int32(SEARCH_FIXED_STEPS), state))
    thr_ref[...] = lo

    has_tie = jnp.max((n_lo > topk).astype(I32)) > 0

    @pl.when(has_tie)
    def _():
        n_hi = count_ge(hi)
        thr = jnp.where((n_lo > topk) & (n_hi >= topk), hi, lo)
        thr_ref[...] = thr
        need = topk - count_rows(lambda s, _: s > thr)

        def jbit(bi, jlo):
            cand = jlo | jnp.left_shift(jnp.int32(1), 29 - bi)
            below = count_rows(lambda s, krow: (s == thr) & (krow < cand))
            return jnp.where(below < need, cand, jlo)

        jstar = lax.fori_loop(0, 30, jbit, jnp.zeros((1, tq), I32))

        def drop(c, carry):
            r0 = pl.multiple_of(c * kc, kc)
            krow = r0 + lax.broadcasted_iota(I32, (kc, tq), 0)
            s = sc_ref[pl.ds(r0, kc), :]
            sc_ref[pl.ds(r0, kc), :] = jnp.where((s == thr) & (krow > jstar), -jnp.inf, s)
            return carry

        lax.fori_loop(0, nchunk, drop, 0)

    tsel = thr_ref[...]

    m_ref[...] = jnp.full(m_ref.shape, NEG_BIG, F32)
    l_ref[...] = jnp.zeros(l_ref.shape, F32)
    ot_ref[...] = jnp.zeros(ot_ref.shape, F32)

    ka = s_ref.shape[1]

    def chunk_body(c, carry):
        r0 = pl.multiple_of(c * ka, ka)
        bias = jnp.where(sc_ref[pl.ds(r0, ka), :] >= tsel, 0.0, -jnp.inf)
        for h in range(N_HEADS):
            pair = slice((h // 2) * 2 * HEAD_DIM, (h // 2 + 1) * 2 * HEAD_DIM)
            s_ref[h] = jnp.dot(k_ref[pl.ds(r0, ka), pair], qtp_ref[h],
                               preferred_element_type=F32) + bias
        for h in range(N_HEADS):
            s = s_ref[h]
            m_old = m_ref[h]
            m_new = jnp.maximum(m_old, jnp.max(s, axis=0, keepdims=True))
            p_ref[h] = jnp.exp2(s - m_new).astype(BF16)
            a_ref[h] = jnp.exp2(m_old - m_new)
            m_ref[h] = m_new
        for h in range(N_HEADS):
            pv = jnp.dot(vt_ref[h, :, pl.ds(r0, ka)], p_ref[h], preferred_element_type=F32)
            ot_ref[h] = a_ref[h] * ot_ref[h] + pv[0:HEAD_DIM]
            l_ref[h] = a_ref[h] * l_ref[h] + pv[HEAD_DIM:HEAD_DIM + 1]
        return carry

    lax.fori_loop(0, (i + 1) * (tq // ka), chunk_body, 0)
    o = ot_ref[...] / l_ref[...]
    o_ref[...] = o.reshape(ATTN_WIDTH, tq).T.astype(BF16)


def _prompt_attention(qqi_t, side_t, ki_nat, k_nat, vt, cast_weights, *, batch, seq, tq, kc, topk):
    nq = seq // tq
    steps = batch * nq
    body = functools.partial(_prompt_attn_body, tq=tq, kc=kc, topk=topk,
                             n_cast=len(cast_weights))
    once = pl.Buffered(1)

    def slab_spec(w):
        rows = w.shape[0] // steps
        assert w.shape[0] % steps == 0 and rows % 16 == 0
        return pl.BlockSpec((rows, w.shape[1]), lambda b, i: (b * nq + i, 0))

    slabs = [slab_spec(w) for w in cast_weights]
    return pl.pallas_call(
        body,
        grid=(batch, nq),
        in_specs=[
            pl.BlockSpec((None, None, ATTN_WIDTH, tq), lambda b, i: (1, b, 0, i)),
            pl.BlockSpec((None, LANES, tq), lambda b, i: (b, 0, i)),
            pl.BlockSpec((None, None, ATTN_WIDTH, tq), lambda b, i: (0, b, 0, i)),
            pl.BlockSpec((seq, LANES), lambda b, i: (b, 0), pipeline_mode=once),
            pl.BlockSpec((seq, ATTN_WIDTH), lambda b, i: (b, 0), pipeline_mode=once),
            pl.BlockSpec((None, N_HEADS, V_ROWS, seq), lambda b, i: (b, 0, 0, 0),
                         pipeline_mode=once),
        ] + slabs,
        out_specs=[pl.BlockSpec((tq, ATTN_WIDTH), lambda b, i: (b * nq + i, 0))] + slabs,
        out_shape=[jax.ShapeDtypeStruct((batch * seq, ATTN_WIDTH), BF16)]
        + [jax.ShapeDtypeStruct(w.shape, BF16) for w in cast_weights],
        scratch_shapes=[
            pltpu.VMEM((seq, tq), F32),
            pltpu.VMEM((1, tq), F32),
            pltpu.VMEM((N_HEADS, 2 * HEAD_DIM, tq), BF16),
            pltpu.VMEM((N_HEADS, HEAD_DIM, tq), F32),
            pltpu.VMEM((N_HEADS, 1, tq), F32),
            pltpu.VMEM((N_HEADS, 1, tq), F32),
            pltpu.VMEM((N_HEADS, 1, tq), F32),
            pltpu.VMEM((N_HEADS, ATTN_PV_KEYS, tq), F32),
            pltpu.VMEM((N_HEADS, ATTN_PV_KEYS, tq), BF16),
        ],
        compiler_params=_params("parallel", "arbitrary"),
        name="prompt_sparse_attn",
    )(qqi_t, side_t, qqi_t, ki_nat, k_nat, vt, *cast_weights)


def _pool_prompt_rows(i, u_ref, wg_ref, sc_ref, ue_ref, *, tm):
    @pl.when(i == 0)
    def _():
        ue_ref[0:POOL_HALO, :] = jnp.zeros((POOL_HALO, ue_ref.shape[1]), F32)

    u = u_ref[...]
    ue_ref[POOL_HALO:POOL_HALO + tm, :] = u
    pos = i * tm + lax.broadcasted_iota(I32, (tm, 1), 0)
    outs = []
    for g, w in enumerate(POOL_WINDOWS):
        cols = slice(g * POOL_GROUP_WIDTH, (g + 1) * POOL_GROUP_WIDTH)
        win = u[:, cols]
        for back in range(1, w):
            win = win + ue_ref[POOL_HALO - back:POOL_HALO - back + tm, cols]
        cnt = jnp.minimum(pos + 1, w).astype(F32)
        mixed = win / cnt - u[:, cols]
        outs.append(jnp.dot(mixed.astype(BF16), wg_ref[g], preferred_element_type=F32))
    ue_ref[0:POOL_HALO, :] = u[tm - POOL_HALO:, :]
    return (jnp.concatenate(outs, axis=1) * sc_ref[...]).astype(BF16)


def _pool_sample_body(st_ref, u_ref, wg_ref, sc_ref, o_ref):
    st = st_ref[...]
    u = u_ref[...]
    n_state = st.shape[1]
    outs = []
    for g, w in enumerate(POOL_WINDOWS):
        cols = slice(g * POOL_GROUP_WIDTH, (g + 1) * POOL_GROUP_WIDTH)
        win = u[:, cols] + jnp.sum(st[:, n_state - (w - 1):, cols], axis=1)
        mixed = win / float(w) - u[:, cols]
        outs.append(jnp.dot(mixed.astype(BF16), wg_ref[g], preferred_element_type=F32))
    o_ref[...] = (jnp.concatenate(outs, axis=1) * sc_ref[...]).astype(BF16)


def _pool_sample(state, u_new, wg, scale):
    nb, width = u_new.shape
    return pl.pallas_call(
        _pool_sample_body,
        out_shape=jax.ShapeDtypeStruct((nb, width), BF16),
        compiler_params=pltpu.CompilerParams(vmem_limit_bytes=VMEM_LIMIT_BYTES),
        name="pool_sample",
    )(state, u_new, wg, scale)


def _merge_rows(pool_o, x_ref, ao_ref, ga_ref, gb_ref, bg_ref, wa_ref, wp_ref, wo_ref, g2_ref,
                h_ref, z_ref):
    ya = jnp.dot(ao_ref[...], wa_ref[...], preferred_element_type=F32)
    yb = jnp.dot(pool_o, wp_ref[...], preferred_element_type=F32)
    merged = (jax.nn.sigmoid(ga_ref[...] + bg_ref[0:1, :]) * ya
              + jax.nn.sigmoid(gb_ref[...] + bg_ref[1:2, :]) * yb)
    h = x_ref[...] + jnp.dot(merged.astype(BF16), wo_ref[...], preferred_element_type=F32)
    h_ref[...] = h
    ms = jnp.mean(h * h, axis=-1, keepdims=True)
    z_ref[...] = (h * lax.rsqrt(ms + EPS) * g2_ref[...]).astype(BF16)


def _merge_body(x_ref, ao_ref, po_ref, ga_ref, gb_ref, bg_ref, wa_ref, wp_ref, wo_ref, g2_ref,
                h_ref, z_ref):
    _merge_rows(po_ref[...], x_ref, ao_ref, ga_ref, gb_ref, bg_ref, wa_ref, wp_ref, wo_ref,
                g2_ref, h_ref, z_ref)


def _pool_merge_body(x_ref, ao_ref, u_ref, ga_ref, gb_ref, bg_ref, wg_ref, sc_ref, wa_ref, wp_ref,
                     wo_ref, g2_ref, h_ref, z_ref, ue_ref, *, tm):
    pool_o = _pool_prompt_rows(pl.program_id(1), u_ref, wg_ref, sc_ref, ue_ref, tm=tm)
    _merge_rows(pool_o, x_ref, ao_ref, ga_ref, gb_ref, bg_ref, wa_ref, wp_ref, wo_ref, g2_ref,
                h_ref, z_ref)


def _pool_merge(x, attn_o, p_main, b_gate, wg, scale, wa, wp, wo, g2, *, batch, seq, tm, u_col,
                ga_col, gb_col):
    m, d = x.shape
    nblk = seq // tm
    width = wg.shape[0] * wg.shape[1]
    rows = lambda b, i: (b * nblk + i, 0)
    const = lambda b, i: (0, 0)
    return pl.pallas_call(
        functools.partial(_pool_merge_body, tm=tm),
        grid=(batch, nblk),
        in_specs=[
            pl.BlockSpec((tm, d), rows),
            pl.BlockSpec((tm, attn_o.shape[1]), rows),
            pl.BlockSpec((tm, width), lambda b, i: (b * nblk + i, u_col)),
            pl.BlockSpec((tm, d), lambda b, i: (b * nblk + i, ga_col)),
            pl.BlockSpec((tm, d), lambda b, i: (b * nblk + i, gb_col)),
            pl.BlockSpec(b_gate.shape, const),
            pl.BlockSpec(wg.shape, lambda b, i: (0, 0, 0)),
            pl.BlockSpec((1, width), const),
            pl.BlockSpec(wa.shape, const),
            pl.BlockSpec(wp.shape, const),
            pl.BlockSpec(wo.shape, const),
            pl.BlockSpec((1, d), const),
        ],
        out_specs=[pl.BlockSpec((tm, d), rows), pl.BlockSpec((tm, d), rows)],
        out_shape=[jax.ShapeDtypeStruct((m, d), F32), jax.ShapeDtypeStruct((m, d), BF16)],
        scratch_shapes=[pltpu.VMEM((POOL_HALO + tm, width), F32)],
        compiler_params=_params("parallel", "arbitrary"),
        name="pool_gated_merge",
    )(x, attn_o, p_main, p_main, p_main, b_gate, wg, scale, wa, wp, wo, g2)


def _merge(x, attn_o, pool_o, p_main, b_gate, wa, wp, wo, g2, *, tm, ga_col, gb_col):
    m, d = x.shape
    const = lambda i: (0, 0)
    return pl.pallas_call(
        _merge_body,
        grid=(m // tm,),
        in_specs=[
            pl.BlockSpec((tm, d), lambda i: (i, 0)),
            pl.BlockSpec((tm, attn_o.shape[1]), lambda i: (i, 0)),
            pl.BlockSpec((tm, pool_o.shape[1]), lambda i: (i, 0)),
            pl.BlockSpec((tm, d), lambda i: (i, ga_col)),
            pl.BlockSpec((tm, d), lambda i: (i, gb_col)),
            pl.BlockSpec(b_gate.shape, const),
            pl.BlockSpec(wa.shape, const),
            pl.BlockSpec(wp.shape, const),
            pl.BlockSpec(wo.shape, const),
            pl.BlockSpec((1, d), const),
        ],
        out_specs=[pl.BlockSpec((tm, d), lambda i: (i, 0)), pl.BlockSpec((tm, d), lambda i: (i, 0))],
        out_shape=[jax.ShapeDtypeStruct((m, d), F32), jax.ShapeDtypeStruct((m, d), BF16)],
        compiler_params=_params("parallel"),
        name="gated_merge",
    )(x, attn_o, pool_o, p_main, p_main, b_gate, wa, wp, wo, g2)


def _ffn_body(z_ref, h_ref, wu_ref, wd_ref, gf_ref, y_ref, acc_ref):
    f = pl.program_id(1)

    @pl.when(f == 0)
    def _():
        acc_ref[...] = h_ref[...]

    a = jnp.dot(z_ref[...], wu_ref[...], preferred_element_type=F32)
    a = jnp.square(jnp.maximum(a, 0.0))
    acc_ref[...] += jnp.dot(a.astype(BF16), wd_ref[...], preferred_element_type=F32)

    @pl.when(f == pl.num_programs(1) - 1)
    def _():
        h2 = acc_ref[...]
        ms = jnp.mean(h2 * h2, axis=-1, keepdims=True)
        y_ref[...] = h2 * lax.rsqrt(ms + EPS) * gf_ref[...]


def _ffn(z, h, wu, wd, gf, *, tm, tf):
    m, d = h.shape
    dff = wu.shape[1]
    return pl.pallas_call(
        _ffn_body,
        grid=(m // tm, dff // tf),
        in_specs=[
            pl.BlockSpec((tm, d), lambda i, f: (i, 0)),
            pl.BlockSpec((tm, d), lambda i, f: (i, 0)),
            pl.BlockSpec((d, tf), lambda i, f: (0, f)),
            pl.BlockSpec((tf, d), lambda i, f: (f, 0)),
            pl.BlockSpec((1, d), lambda i, f: (0, 0)),
        ],
        out_specs=pl.BlockSpec((tm, d), lambda i, f: (i, 0)),
        out_shape=jax.ShapeDtypeStruct((m, d), F32),
        scratch_shapes=[pltpu.VMEM((tm, d), F32)],
        compiler_params=_params("parallel", "arbitrary"),
        name="ffn_final_norm",
    )(z, h, wu, wd, gf)


def _sample_scores_body(pt_ref, qi_ref, w_ref, newpage_ref, *rest, n_pages_step):
    page_refs = rest[:n_pages_step]
    o_ref, onew_ref = rest[n_pages_step:]
    qi = qi_ref[...].astype(BF16)
    wcol = w_ref[...] * (IDX_HEADS ** -0.5 * IDX_DIM ** -0.5)

    def score(pages_t):
        s = jnp.dot(qi, pages_t.astype(BF16), preferred_element_type=F32)
        return jnp.sum(jnp.maximum(s, 0.0) * wcol, axis=0, keepdims=True)

    o_ref[...] = score(jnp.concatenate([r[...] for r in page_refs], axis=1))
    onew_ref[...] = score(newpage_ref[...])


def _sample_scores(page_table, qi, w, newpage, kidx_t, *, pages_step):
    nb, n_pages = page_table.shape
    page = kidx_t.shape[3]
    steps = n_pages // pages_step

    def page_spec(j):
        return pl.BlockSpec((None, None, IDX_DIM, page),
                            lambda b, g, pt: (0, pt[b, g * pages_step + j], 0, 0))

    grid_spec = pltpu.PrefetchScalarGridSpec(
        num_scalar_prefetch=1,
        grid=(nb, steps),
        in_specs=[
            pl.BlockSpec((None, IDX_HEADS, IDX_DIM), lambda b, g, pt: (b, 0, 0)),
            pl.BlockSpec((None, IDX_HEADS, 1), lambda b, g, pt: (b, 0, 0)),
            pl.BlockSpec((None, IDX_DIM, page), lambda b, g, pt: (b, 0, 0)),
        ] + [page_spec(j) for j in range(pages_step)],
        out_specs=[
            pl.BlockSpec((None, 1, pages_step * page), lambda b, g, pt: (b, 0, g)),
            pl.BlockSpec((None, 1, page), lambda b, g, pt: (b, 0, 0)),
        ],
    )
    return pl.pallas_call(
        functools.partial(_sample_scores_body, n_pages_step=pages_step),
        grid_spec=grid_spec,
        out_shape=[jax.ShapeDtypeStruct((nb, 1, n_pages * page), F32),
                   jax.ShapeDtypeStruct((nb, 1, page), F32)],
        compiler_params=_params("parallel", "arbitrary"),
        name="sample_indexer_scores",
    )(page_table, qi, w, newpage, *([kidx_t] * pages_step))


def _prefix_exclusive(mask_f32, tri):
    rows, length = mask_f32.shape
    off = jnp.zeros((rows, 1), F32)
    parts = []
    for c in range(length // LANES):
        mc = mask_f32[:, c * LANES:(c + 1) * LANES]
        inc = jnp.dot(mc.astype(BF16), tri, preferred_element_type=F32)
        parts.append(inc - mc + off)
        off = off + inc[:, LANES - 1:LANES]
    return jnp.concatenate(parts, axis=1)


def _sample_select_body(sc_ref, scnew_ref, bias_ref, biasnew_ref, *, topk):
    past = sc_ref.shape[1]
    sc = jnp.concatenate([sc_ref[...], scnew_ref[...]], axis=1)
    nb, length = sc.shape
    pos = lax.broadcasted_iota(I32, (nb, length), 1)
    valid = pos <= past
    scv = jnp.where(valid, sc, -jnp.inf)

    def count_ge(t):
        return jnp.sum((scv >= t).astype(I32), axis=1, keepdims=True)

    lo0 = jnp.min(jnp.where(valid, sc, jnp.inf), axis=1, keepdims=True)
    hi0 = jnp.max(scv, axis=1, keepdims=True)

    def unresolved(lo, hi, n_lo):
        mid = 0.5 * lo + 0.5 * hi
        return (n_lo > topk) & (mid > lo) & (mid < hi)

    def more(carry):
        it, lo, hi, n_lo = carry
        return jnp.logical_and(it < BISECT_MAX_STEPS,
                               jnp.max(unresolved(lo, hi, n_lo).astype(I32)) > 0)

    def step(carry):
        it, lo, hi, n_lo = carry
        mid = 0.5 * lo + 0.5 * hi
        cnt = count_ge(mid)
        take = cnt >= topk
        return it + 1, jnp.where(take, mid, lo), jnp.where(take, hi, mid), jnp.where(take, cnt, n_lo)

    _, lo, hi, n_lo = lax.while_loop(more, step, (jnp.int32(0), lo0, hi0, count_ge(lo0)))
    thr = jnp.where(count_ge(hi) >= topk, hi, lo)
    gt = scv > thr
    eq = (scv == thr) & valid
    tri = (lax.broadcasted_iota(I32, (LANES, LANES), 0)
           <= lax.broadcasted_iota(I32, (LANES, LANES), 1)).astype(BF16)
    need = (topk - jnp.sum(gt.astype(I32), axis=1, keepdims=True)).astype(F32)
    eq_rank = _prefix_exclusive(eq.astype(F32), tri)
    sel = (gt | (eq & (eq_rank < need))) & valid
    bias = jnp.where(sel, 0.0, -jnp.inf)
    bias_ref[...] = bias[:, :past]
    biasnew_ref[...] = bias[:, past:]


def _sample_select(scores, score_new, *, topk):
    nb, past = scores.shape
    return pl.pallas_call(
        functools.partial(_sample_select_body, topk=topk),
        out_shape=[jax.ShapeDtypeStruct((nb, past), F32),
                   jax.ShapeDtypeStruct(score_new.shape, F32)],
        compiler_params=pltpu.CompilerParams(vmem_limit_bytes=VMEM_LIMIT_BYTES),
        name="sample_topk_select",
    )(scores, score_new)


def _heads_on_lanes_to_slabs(x_t, page):
    return jnp.stack([jnp.broadcast_to(x_t[:, h:h + 1], (HEAD_DIM, page))
                      for h in range(N_HEADS)])


def _lane_softmax_update(scores, values, m_ref, l_ref, acc_ref):
    m_old = m_ref[...]
    m_new = m_old
    for s in scores:
        m_new = jnp.maximum(m_new, s)
    alpha = jnp.exp2(m_old - m_new)
    l = alpha * l_ref[...]
    acc = alpha * acc_ref[...]
    for s, v in zip(scores, values):
        p = jnp.exp2(s - m_new)
        l = l + p
        acc = acc + p * v
    m_ref[...] = m_new
    l_ref[...] = l
    acc_ref[...] = acc


def _attend_pages(qt_ref, bias_ref, k_refs, v_refs, m_ref, l_ref, acc_ref, *, page):
    q = _heads_on_lanes_to_slabs(qt_ref[...], page)
    scores = []
    for j, k_ref in enumerate(k_refs):
        s = jnp.sum(k_ref[...] * q, axis=1, keepdims=True)
        scores.append(s + bias_ref[:, j * page:(j + 1) * page][None])
    _lane_softmax_update(scores, [v_ref[...] for v_ref in v_refs], m_ref, l_ref, acc_ref)


def _attend_finish(qt_ref, knt_ref, vnt_ref, biasnew_ref, o_ref, m_ref, l_ref, acc_ref, *, page):
    q = _heads_on_lanes_to_slabs(qt_ref[...], page)
    kn = _heads_on_lanes_to_slabs(knt_ref[...], page)
    s_new = jnp.sum(kn * q, axis=1, keepdims=True) + biasnew_ref[...][None]
    _lane_softmax_update([s_new], [_heads_on_lanes_to_slabs(vnt_ref[...], page)],
                         m_ref, l_ref, acc_ref)
    m = m_ref[...]
    f = jnp.exp2(m - jnp.max(m, axis=-1, keepdims=True))
    denom = jnp.sum(l_ref[...] * f, axis=-1, keepdims=True)
    o_ref[...] = jnp.sum(acc_ref[...] * f, axis=-1, keepdims=True) / denom


def _ffn_attend_body(pt_ref, z_ref, h_ref, wu_ref, wd_ref, gf_ref,
                     qt_ref, knt_ref, vnt_ref, bias_ref, biasnew_ref, *rest,
                     n_pages_step, page, steps_per_sample):
    k_refs = rest[:n_pages_step]
    v_refs = rest[n_pages_step:2 * n_pages_step]
    y_ref, o_ref, m_ref, l_ref, pacc_ref = rest[2 * n_pages_step:]
    acc_ref = y_ref
    f = pl.program_id(1)
    g = (pl.program_id(0) * pl.num_programs(1) + f) % steps_per_sample

    @pl.when(f == 0)
    def _():
        acc_ref[...] = h_ref[...]

    @pl.when(g == 0)
    def _():
        m_ref[...] = jnp.full(m_ref.shape, NEG_BIG, F32)
        l_ref[...] = jnp.zeros(l_ref.shape, F32)
        pacc_ref[...] = jnp.zeros(pacc_ref.shape, F32)

    a = jnp.dot(z_ref[...], wu_ref[...], preferred_element_type=F32)
    a = jnp.square(jnp.maximum(a, 0.0))
    acc_ref[...] += jnp.dot(a.astype(BF16), wd_ref[...], preferred_element_type=F32)
    _attend_pages(qt_ref, bias_ref, k_refs, v_refs, m_ref, l_ref, pacc_ref, page=page)

    @pl.when(f == pl.num_programs(1) - 1)
    def _():
        h2 = acc_ref[...]
        ms = jnp.mean(h2 * h2, axis=-1, keepdims=True)
        y_ref[...] = h2 * lax.rsqrt(ms + EPS) * gf_ref[...]

    @pl.when(g == steps_per_sample - 1)
    def _():
        _attend_finish(qt_ref, knt_ref, vnt_ref, biasnew_ref, o_ref, m_ref, l_ref, pacc_ref,
                       page=page)


def _ffn_attend(z, h, wu, wd, gf, page_table, qt, knt, vnt, bias, bias_new, ck_t, cv_t, *, tm, tf):
    m, d = h.shape
    dff = wu.shape[1]
    nb, n_pages = page_table.shape
    page = ck_t.shape[4]
    ni, nf = m // tm, dff // tf
    total_steps = ni * nf
    assert (nb * n_pages) % total_steps == 0
    pages_step = nb * n_pages // total_steps
    assert n_pages % pages_step == 0
    sps = n_pages // pages_step

    def sample(i, f):
        return (i * nf + f) // sps

    def page_spec(j):
        def index(i, f, pt):
            t = i * nf + f
            return (0, pt[t // sps, (t % sps) * pages_step + j], 0, 0, 0)

        return pl.BlockSpec((None, None, N_HEADS, HEAD_DIM, page), index)

    pages = [page_spec(j) for j in range(pages_step)]
    tq_block = pl.BlockSpec((None, HEAD_DIM, N_HEADS), lambda i, f, pt: (sample(i, f), 0, 0))
    grid_spec = pltpu.PrefetchScalarGridSpec(
        num_scalar_prefetch=1,
        grid=(ni, nf),
        in_specs=[
            pl.BlockSpec((tm, d), lambda i, f, pt: (i, 0)),
            pl.BlockSpec((tm, d), lambda i, f, pt: (i, 0), pipeline_mode=pl.Buffered(1)),
            pl.BlockSpec((d, tf), lambda i, f, pt: (0, f)),
            pl.BlockSpec((tf, d), lambda i, f, pt: (f, 0)),
            pl.BlockSpec((1, d), lambda i, f, pt: (0, 0)),
            tq_block, tq_block, tq_block,
            pl.BlockSpec((None, 1, pages_step * page),
                         lambda i, f, pt: (sample(i, f), 0, (i * nf + f) % sps)),
            pl.BlockSpec((None, 1, page), lambda i, f, pt: (sample(i, f), 0, 0)),
        ] + pages + pages,
        out_specs=[
            pl.BlockSpec((tm, d), lambda i, f, pt: (i, 0)),
            pl.BlockSpec((None, N_HEADS, HEAD_DIM, 1), lambda i, f, pt: (sample(i, f), 0, 0, 0)),
        ],
        scratch_shapes=[
            pltpu.VMEM((N_HEADS, 1, page), F32),
            pltpu.VMEM((N_HEADS, 1, page), F32),
            pltpu.VMEM((N_HEADS, HEAD_DIM, page), F32),
        ],
    )
    return pl.pallas_call(
        functools.partial(_ffn_attend_body, n_pages_step=pages_step, page=page,
                          steps_per_sample=sps),
        grid_spec=grid_spec,
        out_shape=[jax.ShapeDtypeStruct((m, d), F32),
                   jax.ShapeDtypeStruct((nb, N_HEADS, HEAD_DIM, 1), F32)],
        compiler_params=_params("arbitrary", "arbitrary"),
        name="ffn_with_paged_attend",
    )(page_table, z, h, wu, wd, gf, qt, knt, vnt, bias, bias_new,
      *([ck_t] * pages_step), *([cv_t] * pages_step))


def kernel(x_prompt, x_sample, cache_k, cache_v, cache_kidx, state_pool, page_table, norm_mix_g,
           w_in, b_gate, w_attn_out, w_pool_group, pool_scale, w_pool_out, w_out, norm_ffn_g,
           w_up, w_down, norm_final_g):
    depth = w_in.shape[0]
    assert depth == 1, "single-layer step"
    batch, seq, d_model = x_prompt.shape
    dec_batch, dec_seq, _ = x_sample.shape
    assert dec_seq == 1, "one new token per sample"
    n_pages = page_table.shape[1]
    page = cache_k.shape[2]
    past = n_pages * page
    pool_width = pool_scale.shape[1]
    score_pages = min(SCORE_PAGES, n_pages)
    assert page == LANES and n_pages % score_pages == 0
    l = 0

    tn = PROJ_COLS
    assert ATTN_WIDTH == tn and pool_width == tn and d_model == 2 * tn and IDX_HEADS * IDX_DIM == tn
    c_ki = 3 * ATTN_WIDTH + IDX_HEADS * IDX_DIM
    c_u = c_ki + IDX_DIM + IDX_HEADS
    n_head_rows = c_ki + LANES
    wt_bf = jnp.swapaxes(w_in[l], 0, 1).astype(BF16)
    n_tail = wt_bf.shape[0] - c_u
    tail_shift = (n_tail - pool_width) // tn
    K_COL, V_COL, QI_COL = 1, 2, 3
    SIDE_BLOCK = c_ki // LANES
    GA_COL, GB_COL, U_COL = 0, 1, 4
    g_mix = norm_mix_g[l][None, :]
    wg = w_pool_group[l].astype(BF16)
    g_ffn = norm_ffn_g[l][None, :]
    g_fin = norm_final_g[None, :]
    scale = pool_scale[l][None, :]

    def col(p, c, width=tn):
        return p[:, c * tn:c * tn + width]

    xp = x_prompt.reshape(batch * seq, d_model)
    tm_p = PROJ_ROWS
    tm_wide = PROJ_ROWS_WIDE if seq % PROJ_ROWS_WIDE == 0 else PROJ_ROWS
    assert seq % tm_p == 0
    qqi_t = _proj_qqi(xp, g_mix, wt_bf, batch=batch, seq=seq, tm=tm_wide,
                      row_blocks=(0, QI_COL))
    k_t, v_t, k_nat, vt_bf, side_t, ki_nat = _proj_kv(xp, g_mix, wt_bf, batch=batch, seq=seq,
                                                      tm=tm_p, side_block=SIDE_BLOCK)
    pp = _proj_nat(xp, g_mix, wt_bf, tm_wide, tn, row_start=c_u, n=n_tail, out_shift=tail_shift)

    tq, kc = ATTN_QUERIES, ATTN_KEYS
    topk_p = min(TOPK_MAX, seq // 4)
    attn_p, wa, wp, wo, wu, wd = _prompt_attention(
        qqi_t, side_t, ki_nat, k_nat, vt_bf,
        [w_attn_out[l], w_pool_out[l], w_out[l], w_up[l], w_down[l]],
        batch=batch, seq=seq, tq=tq, kc=kc, topk=topk_p)
    h_p, z_p = _pool_merge(xp, attn_p, pp, b_gate[l], wg, scale, wa, wp, wo, g_ffn, batch=batch,
                           seq=seq, tm=MERGE_ROWS, u_col=U_COL, ga_col=GA_COL, gb_col=GB_COL)

    xs = x_sample.reshape(dec_batch, d_model)
    head_tile = max(t for t in (1024, 768, 512, 384, 256, 128) if n_head_rows % t == 0)
    ps_head = _proj_nat(xs, g_mix, wt_bf, dec_batch, head_tile, n=n_head_rows)
    ps = _proj_nat(xs, g_mix, wt_bf, dec_batch, tn, row_start=c_u, n=n_tail,
                   out_shift=tail_shift)
    topk_s = min(TOPK_MAX, (past + dec_seq) // 4)
    q_s = col(ps_head, 0) * Q_SCALE
    k_s = col(ps_head, K_COL)
    v_s = col(ps_head, V_COL)
    ki_s = ps_head[:, c_ki:c_ki + IDX_DIM]
    wi_s = ps_head[:, c_ki + IDX_DIM:c_u]
    u_s = col(ps, U_COL)
    qi_s = col(ps_head, QI_COL).reshape(dec_batch, IDX_HEADS, IDX_DIM)
    kidx_t = jnp.swapaxes(cache_kidx, 2, 3)
    ck_t = jnp.transpose(cache_k, (0, 1, 3, 4, 2))
    cv_t = jnp.transpose(cache_v, (0, 1, 3, 4, 2))
    newpage = jnp.zeros((dec_batch, IDX_DIM, page), F32).at[:, :, 0].set(ki_s)
    scores, score_new = _sample_scores(page_table, qi_s, wi_s[:, :, None], newpage, kidx_t,
                                       pages_step=score_pages)
    bias, bias_new = _sample_select(scores.reshape(dec_batch, past),
                                    score_new.reshape(dec_batch, page), topk=topk_s)

    def dims_by_heads(t):
        return t.reshape(dec_batch, N_HEADS, HEAD_DIM).transpose(0, 2, 1)

    y_p, attn_s = _ffn_attend(z_p, h_p, wu, wd, g_fin, page_table, dims_by_heads(q_s),
                              dims_by_heads(k_s), dims_by_heads(v_s), bias[:, None, :],
                              bias_new[:, None, :], ck_t, cv_t, tm=FFN_ROWS, tf=FFN_COLS)
    pool_s = _pool_sample(state_pool[l], u_s, wg, scale)
    h_s, z_s = _merge(xs, attn_s.reshape(dec_batch, ATTN_WIDTH).astype(BF16), pool_s, ps,
                      b_gate[l], wa, wp, wo, g_ffn, tm=dec_batch, ga_col=GA_COL, gb_col=GB_COL)
    y_s = _ffn(z_s, h_s, wu, wd, g_fin, tm=dec_batch, tf=FFN_COLS)

    n_state = state_pool.shape[2]
    u_p = pp.reshape(batch, seq, -1)[:, -n_state:, U_COL * tn:(U_COL + 1) * tn]
    pool_state_s = jnp.concatenate([state_pool[l].astype(F32), u_s[:, None, :]], axis=1)[:, -n_state:]

    def heads_last(t):
        return t.reshape(batch, N_HEADS, HEAD_DIM, seq).transpose(0, 3, 1, 2)[None]

    return (
        y_p.reshape(batch, seq, d_model),
        y_s.reshape(dec_batch, dec_seq, d_model),
        heads_last(k_t),
        heads_last(v_t),
        side_t[:, :IDX_DIM, :].transpose(0, 2, 1)[None],
        u_p[None],
        k_s.reshape(1, dec_batch, dec_seq, N_HEADS, HEAD_DIM),
        v_s.reshape(1, dec_batch, dec_seq, N_HEADS, HEAD_DIM),
        ki_s.reshape(1, dec_batch, dec_seq, IDX_DIM),
        pool_state_s[None],
    )
```
